```python
import math
import jax, jax.numpy as jnp
from jax import lax
import numpy as np

D_MODEL = 2048
BATCH = 4
SEQ = 2048
DEPTH = 2
DEC_BATCH = 128
DEC_SEQ = 8
PAST_LEN = 16384
PAGE_SIZE = 128

MIX_WIDTH = D_MODEL
M_WIDTH = MIX_WIDTH // 2
M_HEADS = 4
M_DK = M_WIDTH // M_HEADS
M_DV = M_WIDTH // M_HEADS
S_WIDTH = MIX_WIDTH - M_WIDTH
S_HEADDIM = 64
S_HEADS = S_WIDTH // S_HEADDIM
S_GROUPS = 2
S_STATE = 128
S_CONV = 4
S_CONV_DIM = S_WIDTH + 2 * S_GROUPS * S_STATE
D_FF = ((8 * D_MODEL) // 3 + 127) // 128 * 128
FFN_CONV = 3
CHUNK = 64
ALPHA = (2 * DEPTH) ** 0.25
BETA = (8 * DEPTH) ** -0.25
LN_EPS = 1e-5
GN_EPS = 1e-6
SPLIT_SIZES = (M_WIDTH, M_WIDTH, M_WIDTH, M_WIDTH, M_HEADS, M_HEADS, S_WIDTH, S_CONV_DIM, S_HEADS)
IN_DIM = M_WIDTH * 4 + M_HEADS * 2 + S_WIDTH + S_CONV_DIM + S_HEADS

kernel_name = 'hymba_mlstm_ssd_convffn_deepnorm_step'


def split_columns(u, sizes):
    idx = []
    acc = 0
    for s in sizes[:-1]:
        acc += s
        idx.append(acc)
    return jnp.split(u, idx, axis=-1)


def layer_norm(x, g, b):
    xf = x.astype(jnp.float32)
    mu = jnp.mean(xf, axis=-1, keepdims=True)
    var = jnp.mean(jnp.square(xf - mu), axis=-1, keepdims=True)
    return ((xf - mu) * lax.rsqrt(var + LN_EPS) * g + b).astype(x.dtype)


def causal_dwconv(xp, w, bias):
    K = w.shape[0]
    T = xp.shape[1] - K + 1
    out = xp[:, 0:T] * w[0]
    for j in range(1, K):
        out = out + xp[:, j:j + T] * w[j]
    return out + bias


def chunk_len(T):
    return CHUNK if T % CHUNK == 0 else T


def to_chunks(a, L):
    B, T = a.shape[0], a.shape[1]
    return jnp.moveaxis(a.reshape((B, T // L, L) + a.shape[2:]), 1, 0)


def from_chunks(a):
    nc, B, L = a.shape[0], a.shape[1], a.shape[2]
    return jnp.moveaxis(a, 0, 1).reshape((B, nc * L) + a.shape[3:])


def mlstm_chunk(carry, inp):
    C, n, m = carry
    q, k, v, ig, lf = inp
    L = q.shape[1]
    causal = jnp.tril(jnp.ones((L, L), dtype=bool))[None, :, :, None]
    b = jnp.cumsum(lf, axis=1)
    log_w = jnp.where(causal, b[:, :, None, :] - b[:, None, :, :] + ig[:, None, :, :], -jnp.inf)
    log_inter = b + m[:, None, :]
    m_t = jnp.maximum(log_inter, jnp.max(log_w, axis=2))
    w_intra = jnp.exp(log_w - m_t[:, :, None, :])
    w_inter = jnp.exp(log_inter - m_t)
    s = jnp.einsum('bthd,bshd->btsh', q, k) * w_intra
    num = jnp.einsum('btsh,bshe->bthe', s, v) + w_inter[..., None] * jnp.einsum('bthd,bhde->bthe', q, C)
    den = jnp.sum(s, axis=2) + w_inter * jnp.einsum('bthd,bhd->bth', q, n)
    h = num / jnp.maximum(jnp.abs(den), jnp.exp(-m_t))[..., None]
    m_end = m_t[:, -1]
    w_end = jnp.exp(b[:, -1:, :] - b + ig - m_end[:, None, :])
    decay = jnp.exp(b[:, -1] + m - m_end)
    C_new = decay[..., None, None] * C + jnp.einsum('bsh,bshd,bshe->bhde', w_end, k, v)
    n_new = decay[..., None] * n + jnp.einsum('bsh,bshd->bhd', w_end, k)
    return (C_new, n_new, m_end), h


def mlstm_mixer(q, k, v, o_pre, i_pre, f_pre, b_i, b_f, norm_w, C0, n0, m0):
    f32 = jnp.float32
    B, T = q.shape[0], q.shape[1]
    q = q.reshape(B, T, M_HEADS, M_DK).astype(f32)
    k = k.reshape(B, T, M_HEADS, M_DK).astype(f32) * (M_DK ** -0.5)
    v = v.reshape(B, T, M_HEADS, M_DV).astype(f32)
    ig = (i_pre + b_i).astype(f32)
    lf = jax.nn.log_sigmoid((f_pre + b_f).astype(f32))
    L = chunk_len(T)
    xs = (to_chunks(q, L), to_chunks(k, L), to_chunks(v, L), to_chunks(ig, L), to_chunks(lf, L))
    (C, n, m), h = lax.scan(mlstm_chunk, (C0.astype(f32), n0.astype(f32), m0.astype(f32)), xs)
    h = from_chunks(h)
    mu = jnp.mean(h, axis=-1, keepdims=True)
    var = jnp.mean(jnp.square(h - mu), axis=-1, keepdims=True)
    hn = ((h - mu) * lax.rsqrt(var + GN_EPS)).reshape(B, T, M_WIDTH) * norm_w.astype(f32)
    out = hn * jax.nn.sigmoid(o_pre.astype(f32))
    return out, C, n, m


def ssd_chunk(S, inp):
    xs, dt, a, Bm, Cm = inp
    L = xs.shape[1]
    causal = jnp.tril(jnp.ones((L, L), dtype=bool))[None, :, :, None, None]
    b = jnp.cumsum(a, axis=1)
    decay = jnp.exp(jnp.where(causal, b[:, :, None] - b[:, None], -jnp.inf))
    cb = jnp.einsum('btgn,bsgn->btsg', Cm, Bm)
    mw = cb[..., None] * decay * dt[:, None]
    y = jnp.einsum('btsgh,bsghp->btghp', mw, xs) + jnp.exp(b)[..., None] * jnp.einsum('btgn,bghpn->btghp', Cm, S)
    w_end = jnp.exp(b[:, -1:] - b) * dt
    S_new = jnp.exp(b[:, -1])[..., None, None] * S + jnp.einsum('bsgh,bsgn,bsghp->bghpn', w_end, Bm, xs)
    return S_new, y


def ssd_mixer(z, xbc, dt_pre, conv_w, conv_b, dt_bias, A_log, D_skip, norm_w, S0, conv0):
    f32 = jnp.float32
    B, T = z.shape[0], z.shape[1]
    HG = S_HEADS // S_GROUPS
    xp = jnp.concatenate([conv0.astype(xbc.dtype), xbc], axis=1)
    conv_new = xp[:, -(S_CONV - 1):]
    xbc = jax.nn.silu(causal_dwconv(xp, conv_w, conv_b)).astype(f32)
    xs, Bm, Cm = jnp.split(xbc, [S_WIDTH, S_WIDTH + S_GROUPS * S_STATE], axis=-1)
    xs = xs.reshape(B, T, S_GROUPS, HG, S_HEADDIM)
    Bm = Bm.reshape(B, T, S_GROUPS, S_STATE)
    Cm = Cm.reshape(B, T, S_GROUPS, S_STATE)
    dt = jax.nn.softplus(dt_pre.astype(f32) + dt_bias.astype(f32)).reshape(B, T, S_GROUPS, HG)
    A = -jnp.exp(A_log.astype(f32)).reshape(S_GROUPS, HG)
    a = dt * A
    L = chunk_len(T)
    S0 = S0.astype(f32).reshape(B, S_GROUPS, HG, S_HEADDIM, S_STATE)
    inp = (to_chunks(xs, L), to_chunks(dt, L), to_chunks(a, L), to_chunks(Bm, L), to_chunks(Cm, L))
    S, y = lax.scan(ssd_chunk, S0, inp)
    y = from_chunks(y) + D_skip.astype(f32).reshape(S_GROUPS, HG)[..., None] * xs
    g = (y.reshape(B, T, S_WIDTH) * jax.nn.silu(z.astype(f32))).reshape(B, T, S_GROUPS, S_WIDTH // S_GROUPS)
    g = g * lax.rsqrt(jnp.mean(jnp.square(g), axis=-1, keepdims=True) + GN_EPS)
    out = g.reshape(B, T, S_WIDTH) * norm_w.astype(f32)
    return out, S.reshape(B, S_HEADS, S_HEADDIM, S_STATE), conv_new


def conv_ffn(x, w_up, conv_w, conv_b, w_down, buf):
    up = x @ w_up
    xp = jnp.concatenate([buf.astype(up.dtype), up], axis=1)
    buf_new = xp[:, -(FFN_CONV - 1):]
    c = causal_dwconv(xp, conv_w, conv_b)
    g, val = jnp.split(c, 2, axis=-1)
    return (jax.nn.silu(g) * val) @ w_down, buf_new


def layer(x, st, p):
    C0, n0, m0, S0, sconv0, fconv0 = st
    (w_in, b_i, b_f, m_norm_w, s_conv_w, s_conv_b, dt_bias, A_log, D_skip, s_norm_w, w_out,
     ln1_g, ln1_b, w_up, f_conv_w, f_conv_b, w_down, ln2_g, ln2_b) = p
    u = x @ w_in
    q, k, v, o_pre, i_pre, f_pre, z, xbc, dt_pre = split_columns(u, SPLIT_SIZES)
    h_m, C, n, m = mlstm_mixer(q, k, v, o_pre, i_pre, f_pre, b_i, b_f, m_norm_w, C0, n0, m0)
    h_s, S, sconv = ssd_mixer(z, xbc, dt_pre, s_conv_w, s_conv_b, dt_bias, A_log, D_skip, s_norm_w, S0, sconv0)
    mix = jnp.concatenate([h_m, h_s], axis=-1).astype(x.dtype) @ w_out
    x = layer_norm(ALPHA * x + mix, ln1_g, ln1_b)
    f, fconv = conv_ffn(x, w_up, f_conv_w, f_conv_b, w_down, fconv0)
    x = layer_norm(ALPHA * x + f, ln2_g, ln2_b)
    return x, (C, n, m, S, sconv, fconv)


def trunk(x, states, params):
    new = [[] for _ in states]
    for l in range(DEPTH):
        st_l = tuple(s[l] for s in states)
        p_l = tuple(w[l] for w in params)
        x, ns = layer(x, st_l, p_l)
        for j in range(len(ns)):
            new[j].append(ns[j])
    return x, tuple(jnp.stack(a, axis=0) for a in new)


def setup_inputs(seed: int = 0) -> dict:
    key = jax.random.key(seed)
    ks = iter(jax.random.split(key, 40))
    f32 = jnp.float32

    def nrm(shape, scale):
        return scale * jax.random.normal(next(ks), shape, f32)

    x_prompt = nrm((BATCH, SEQ, D_MODEL), 1.0)
    x_sample = nrm((DEC_BATCH, DEC_SEQ, D_MODEL), 1.0)
    state_mlstm_C = nrm((DEPTH, DEC_BATCH, M_HEADS, M_DK, M_DV), 0.5)
    state_mlstm_n = nrm((DEPTH, DEC_BATCH, M_HEADS, M_DK), 0.5)
    state_mlstm_m = nrm((DEPTH, DEC_BATCH, M_HEADS), 1.0)
    state_ssm = nrm((DEPTH, DEC_BATCH, S_HEADS, S_HEADDIM, S_STATE), 0.5)
    state_ssm_conv = nrm((DEPTH, DEC_BATCH, S_CONV - 1, S_CONV_DIM), 1.0)
    state_ffn_conv = nrm((DEPTH, DEC_BATCH, FFN_CONV - 1, 2 * D_FF), 1.0)
    w_in = nrm((DEPTH, D_MODEL, IN_DIM), D_MODEL ** -0.5)
    mlstm_b_i = nrm((DEPTH, M_HEADS), 0.1)
    mlstm_b_f = jnp.linspace(3.0, 6.0, M_HEADS, dtype=f32)[None, :] + nrm((DEPTH, M_HEADS), 0.1)
    mlstm_norm_w = 1.0 + nrm((DEPTH, M_WIDTH), 0.02)
    ssm_conv_w = nrm((DEPTH, S_CONV, S_CONV_DIM), S_CONV ** -0.5)
    ssm_conv_b = nrm((DEPTH, S_CONV_DIM), 0.02)
    dt0 = jnp.exp(jax.random.uniform(next(ks), (DEPTH, S_HEADS), f32, math.log(1e-3), math.log(1e-1)))
    ssm_dt_bias = dt0 + jnp.log(-jnp.expm1(-dt0))
    ssm_A_log = jnp.log(jax.random.uniform(next(ks), (DEPTH, S_HEADS), f32, 1.0, 16.0))
    ssm_D = 1.0 + nrm((DEPTH, S_HEADS), 0.1)
    ssm_norm_w = 1.0 + nrm((DEPTH, S_WIDTH), 0.02)
    w_out = nrm((DEPTH, MIX_WIDTH, D_MODEL), BETA * MIX_WIDTH ** -0.5)
    ln1_g = 1.0 + nrm((DEPTH, D_MODEL), 0.02)
    ln1_b = nrm((DEPTH, D_MODEL), 0.02)
    ffn_w_up = nrm((DEPTH, D_MODEL, 2 * D_FF), D_MODEL ** -0.5)
    ffn_conv_w = nrm((DEPTH, FFN_CONV, 2 * D_FF), FFN_CONV ** -0.5)
    ffn_conv_b = nrm((DEPTH, 2 * D_FF), 0.02)
    ffn_w_down = nrm((DEPTH, D_FF, D_MODEL), BETA * D_FF ** -0.5)
    ln2_g = 1.0 + nrm((DEPTH, D_MODEL), 0.02)
    ln2_b = nrm((DEPTH, D_MODEL), 0.02)
    return {'x_prompt': x_prompt, 'x_sample': x_sample,
            'state_mlstm_C': state_mlstm_C, 'state_mlstm_n': state_mlstm_n, 'state_mlstm_m': state_mlstm_m,
            'state_ssm': state_ssm, 'state_ssm_conv': state_ssm_conv, 'state_ffn_conv': state_ffn_conv,
            'w_in': w_in, 'mlstm_b_i': mlstm_b_i, 'mlstm_b_f': mlstm_b_f, 'mlstm_norm_w': mlstm_norm_w,
            'ssm_conv_w': ssm_conv_w, 'ssm_conv_b': ssm_conv_b, 'ssm_dt_bias': ssm_dt_bias,
            'ssm_A_log': ssm_A_log, 'ssm_D': ssm_D, 'ssm_norm_w': ssm_norm_w, 'w_out': w_out,
            'ln1_g': ln1_g, 'ln1_b': ln1_b, 'ffn_w_up': ffn_w_up, 'ffn_conv_w': ffn_conv_w,
            'ffn_conv_b': ffn_conv_b, 'ffn_w_down': ffn_w_down, 'ln2_g': ln2_g, 'ln2_b': ln2_b}


def reference(x_prompt, x_sample, state_mlstm_C, state_mlstm_n, state_mlstm_m, state_ssm, state_ssm_conv,
              state_ffn_conv, w_in, mlstm_b_i, mlstm_b_f, mlstm_norm_w, ssm_conv_w, ssm_conv_b, ssm_dt_bias,
              ssm_A_log, ssm_D, ssm_norm_w, w_out, ln1_g, ln1_b, ffn_w_up, ffn_conv_w, ffn_conv_b,
              ffn_w_down, ln2_g, ln2_b):
    params = (w_in, mlstm_b_i, mlstm_b_f, mlstm_norm_w, ssm_conv_w, ssm_conv_b, ssm_dt_bias, ssm_A_log,
              ssm_D, ssm_norm_w, w_out, ln1_g, ln1_b, ffn_w_up, ffn_conv_w, ffn_conv_b, ffn_w_down,
              ln2_g, ln2_b)
    f32 = jnp.float32
    B = x_prompt.shape[0]
    zero_states = (jnp.zeros((DEPTH, B, M_HEADS, M_DK, M_DV), f32),
                   jnp.zeros((DEPTH, B, M_HEADS, M_DK), f32),
                   jnp.zeros((DEPTH, B, M_HEADS), f32),
                   jnp.zeros((DEPTH, B, S_HEADS, S_HEADDIM, S_STATE), f32),
                   jnp.zeros((DEPTH, B, S_CONV - 1, S_CONV_DIM), x_prompt.dtype),
                   jnp.zeros((DEPTH, B, FFN_CONV - 1, 2 * D_FF), x_prompt.dtype))
    y_prompt, (p_C, p_n, p_m, p_ssm, p_sconv, p_fconv) = trunk(x_prompt, zero_states, params)
    past = (state_mlstm_C, state_mlstm_n, state_mlstm_m, state_ssm, state_ssm_conv, state_ffn_conv)
    y_sample, (s_C, s_n, s_m, s_ssm, s_sconv, s_fconv) = trunk(x_sample, past, params)
    return (y_prompt, y_sample, p_C, p_n, p_m, p_ssm, p_sconv, p_fconv,
            s_C, s_n, s_m, s_ssm, s_sconv, s_fconv)
```

```python
import functools

import jax
import jax.numpy as jnp
from jax import lax
from jax.experimental import pallas as pl
from jax.experimental.pallas import tpu as pltpu

F32 = jnp.float32
BF16 = jnp.bfloat16
HIGHEST = lax.Precision.HIGHEST

M_HEADS = 4
S_HEADDIM = 64
S_GROUPS = 2
S_STATE = 128
S_CONV = 4
FFN_CONV = 3
LN_EPS = 1e-5
GN_EPS = 1e-6
NEG_BIG = -1e30

LANES = 128
SUBLANES = 8
MXU_DIM = 256
VMEM_LIMIT_BYTES = 52 * 1024 * 1024

GATE_I = 0
GATE_F = 4
GATE_DT = 8

PROMPT_CHUNK = 128


def _largest_divisor(n, target, mult):
    best = None
    for d in range(mult, min(n, target) + 1, mult):
        if n % d == 0:
            best = d
    if best is None:
        raise ValueError(f"no tile for {n=} {target=} {mult=}")
    return best


def _params(sem):
    return pltpu.CompilerParams(dimension_semantics=sem, vmem_limit_bytes=VMEM_LIMIT_BYTES)


def _dot(a, b):
    return jnp.dot(a, b, preferred_element_type=F32)


def _dot_nt(a, b, precision=None):
    return lax.dot_general(a, b, (((1,), (1,)), ((), ())), precision=precision, preferred_element_type=F32)


def _dot_tn(a, b):
    return lax.dot_general(a, b, (((0,), (0,)), ((), ())), preferred_element_type=F32)


def _sigmoid(x):
    return 1.0 / (1.0 + jnp.exp(-x))


def _softplus(x):
    return jnp.maximum(x, 0.0) + jnp.log1p(jnp.exp(-jnp.abs(x)))


def _layer_norm(r, g, b):
    mu = jnp.mean(r, axis=-1, keepdims=True)
    d = r - mu
    var = jnp.mean(d * d, axis=-1, keepdims=True)
    return d * lax.rsqrt(var + LN_EPS) * g + b


def _mm_kernel(x_ref, w_ref, o_ref):
    o_ref[...] = _dot(x_ref[...], w_ref[...]).astype(o_ref.dtype)


def _matmul(x, w, out_dtype, tm_target=512, tn_target=512):
    m, k = x.shape
    n = w.shape[1]
    tm = _largest_divisor(m, tm_target, 16)
    tn = _largest_divisor(n, tn_target, LANES)
    return pl.pallas_call(
        _mm_kernel,
        grid=(m // tm, n // tn),
        in_specs=[pl.BlockSpec((tm, k), lambda i, j: (i, 0)),
                  pl.BlockSpec((k, tn), lambda i, j: (0, j))],
        out_specs=pl.BlockSpec((tm, tn), lambda i, j: (i, j)),
        out_shape=jax.ShapeDtypeStruct((m, n), out_dtype),
        compiler_params=_params(("parallel", "parallel")),
        name="in_proj",
    )(x, w)


def _mlstm_kernel(q_ref, k_ref, v_ref, o_ref, g_ref, gb_ref, nw_ref, c0_ref, n0_ref, m0_ref,
                  h_ref, c_out, n_out, m_out, c_s, n_s, m_s, *, L, dk):
    c = pl.program_id(1)
    nc = pl.num_programs(1)

    @pl.when(c == 0)
    def _():
        c_s[...] = c0_ref[0]
        n_s[...] = n0_ref[0]
        m_s[...] = m0_ref[0]

    g = g_ref[...] + gb_ref[...]
    lf = jnp.minimum(g, 0.0) - jnp.log1p(jnp.exp(-jnp.abs(g)))
    row = lax.broadcasted_iota(jnp.int32, (L, L), 0)
    col = lax.broadcasted_iota(jnp.int32, (L, L), 1)
    causal = col <= row
    tri = jnp.where(causal, 1.0, 0.0).astype(F32)
    bcum = jnp.dot(tri, lf, precision=HIGHEST, preferred_element_type=F32)
    lane = lax.broadcasted_iota(jnp.int32, (L, LANES), 1)
    packed = jnp.where(lane < GATE_F, g, bcum)
    sel = jnp.where(lax.broadcasted_iota(jnp.int32, (SUBLANES, LANES), 0)
                    == lax.broadcasted_iota(jnp.int32, (SUBLANES, LANES), 1), 1.0, 0.0).astype(F32)
    rows = _dot_nt(sel, packed, precision=HIGHEST)
    m_vec = m_s[...]
    m_lane = lax.broadcasted_iota(jnp.int32, (1, LANES), 1)
    m_new = m_vec

    for h in range(M_HEADS):
        sl = slice(h * dk, (h + 1) * dk)
        bc = bcum[:, GATE_F + h:GATE_F + h + 1]
        igc = g[:, GATE_I + h:GATE_I + h + 1]
        igr = rows[GATE_I + h:GATE_I + h + 1, :]
        br = rows[GATE_F + h:GATE_F + h + 1, :]
        m_h = m_vec[:, GATE_F + h:GATE_F + h + 1]
        q = q_ref[:, sl]
        k = k_ref[:, sl]
        v = v_ref[:, sl]
        C = c_s[h]
        n = n_s[h:h + 1, :]

        log_w = jnp.where(causal, bc - br + igr, NEG_BIG)
        log_inter = bc + m_h
        m_t = jnp.maximum(log_inter, jnp.max(log_w, axis=1, keepdims=True))
        w_intra = jnp.exp(log_w - m_t)
        w_inter = jnp.exp(log_inter - m_t)
        s = _dot_nt(q, k) * w_intra
        num = _dot(s.astype(BF16), v) + w_inter * _dot(q, C.astype(BF16))
        qn = jnp.sum(q.astype(F32) * n, axis=1, keepdims=True)
        den = jnp.sum(s, axis=1, keepdims=True) + w_inter * qn
        hh = num / jnp.maximum(jnp.abs(den), jnp.exp(-m_t))

        m_end = m_t[L - 1:L, :]
        b_end = bc[L - 1:L, :]
        w_end = jnp.exp(b_end - bc + igc - m_end)
        decay = jnp.exp(b_end + m_h - m_end)
        kw = k.astype(F32) * w_end
        c_s[h] = decay * C + _dot_tn(kw.astype(BF16), v)
        n_s[h:h + 1, :] = decay * n + jnp.sum(kw, axis=0, keepdims=True)
        m_new = jnp.where(m_lane == GATE_F + h, m_end, m_new)

        mu = jnp.mean(hh, axis=1, keepdims=True)
        d = hh - mu
        var = jnp.mean(d * d, axis=1, keepdims=True)
        hn = d * lax.rsqrt(var + GN_EPS) * nw_ref[:, sl]
        h_ref[:, sl] = (hn * _sigmoid(o_ref[:, sl])).astype(h_ref.dtype)

    m_s[...] = m_new

    @pl.when(c == nc - 1)
    def _():
        c_out[0] = c_s[...]
        n_out[0] = n_s[...]
        m_out[0] = m_s[...]


def _mlstm(qkv, oz, gates, gate_bias, norm_w, c0, n0, m0, *, nseq, T, L):
    W = qkv.shape[1] // 3
    dk = W // M_HEADS
    nc = T // L
    rb = lambda b, c: b * nc + c
    kern = functools.partial(_mlstm_kernel, L=L, dk=dk)
    return pl.pallas_call(
        kern,
        grid=(nseq, nc),
        in_specs=[pl.BlockSpec((L, W), lambda b, c: (rb(b, c), 0)),
                  pl.BlockSpec((L, W), lambda b, c: (rb(b, c), 1)),
                  pl.BlockSpec((L, W), lambda b, c: (rb(b, c), 2)),
                  pl.BlockSpec((L, W), lambda b, c: (rb(b, c), 0)),
                  pl.BlockSpec((L, LANES), lambda b, c: (rb(b, c), 0)),
                  pl.BlockSpec((1, LANES), lambda b, c: (0, 0)),
                  pl.BlockSpec((1, W), lambda b, c: (0, 0)),
                  pl.BlockSpec((1, M_HEADS, dk, dk), lambda b, c: (b, 0, 0, 0)),
                  pl.BlockSpec((1, M_HEADS, dk), lambda b, c: (b, 0, 0)),
                  pl.BlockSpec((1, 1, LANES), lambda b, c: (b, 0, 0))],
        out_specs=[pl.BlockSpec((L, W), lambda b, c: (rb(b, c), 0)),
                   pl.BlockSpec((1, M_HEADS, dk, dk), lambda b, c: (b, 0, 0, 0)),
                   pl.BlockSpec((1, M_HEADS, dk), lambda b, c: (b, 0, 0)),
                   pl.BlockSpec((1, 1, LANES), lambda b, c: (b, 0, 0))],
        out_shape=[jax.ShapeDtypeStruct((nseq * T, W), BF16),
                   jax.ShapeDtypeStruct((nseq, M_HEADS, dk, dk), F32),
                   jax.ShapeDtypeStruct((nseq, M_HEADS, dk), F32),
                   jax.ShapeDtypeStruct((nseq, 1, LANES), F32)],
        scratch_shapes=[pltpu.VMEM((M_HEADS, dk, dk), F32),
                        pltpu.VMEM((M_HEADS, dk), F32),
                        pltpu.VMEM((1, LANES), F32)],
        compiler_params=_params(("parallel", "arbitrary")),
        name="mlstm",
    )(qkv, qkv, qkv, oz, gates, gate_bias, norm_w, c0, n0, m0)


def _ssd_kernel(z_ref, x_ref, g_ref, cw_ref, cb_ref, hp_ref, nw_ref, s0_ref, cv0_ref,
                y_ref, s_out, cv_out, s_s, xp_s, y_s, *, L, nheads):
    c = pl.program_id(1)
    nc = pl.num_programs(1)
    P = S_HEADDIM
    N = S_STATE
    hg = nheads // S_GROUPS
    W = nheads * P
    halo = SUBLANES
    kc = S_CONV

    @pl.when(c == 0)
    def _():
        s_s[...] = s0_ref[0]
        xp_s[0:halo, :] = jnp.zeros((halo, xp_s.shape[1]), F32)
        xp_s[halo - (kc - 1):halo, :] = cv0_ref[0]

    xp_s[halo:halo + L, :] = x_ref[...]
    conv = cb_ref[...] + xp_s[halo:halo + L, :] * cw_ref[kc - 1:kc, :]
    for j in range(kc - 1):
        off = halo - (kc - 1) + j
        conv = conv + xp_s[off:off + L, :] * cw_ref[j:j + 1, :]
    tail = xp_s[L:L + halo, :]
    xp_s[0:halo, :] = tail
    cv_out[0] = tail
    xbc = conv * _sigmoid(conv)
    xs = xbc[:, 0:W]
    Bm = xbc[:, W:W + S_GROUPS * N]
    Cm = xbc[:, W + S_GROUPS * N:W + 2 * S_GROUPS * N]

    hp = hp_ref[...]
    dt = _softplus(g_ref[...] + hp[0:1, :])
    a = dt * (-jnp.exp(hp[1:2, :]))
    row = lax.broadcasted_iota(jnp.int32, (L, L), 0)
    col = lax.broadcasted_iota(jnp.int32, (L, L), 1)
    causal = col <= row
    tri = jnp.where(causal, 1.0, 0.0).astype(F32)
    bcum = jnp.dot(tri, a, precision=HIGHEST, preferred_element_type=F32)
    sel = jnp.where(lax.broadcasted_iota(jnp.int32, (nheads, LANES), 0) + GATE_DT
                    == lax.broadcasted_iota(jnp.int32, (nheads, LANES), 1), 1.0, 0.0).astype(F32)
    b_rows = _dot_nt(sel, bcum, precision=HIGHEST)
    dt_rows = _dot_nt(sel, dt, precision=HIGHEST)
    eb = jnp.exp(bcum)
    b_last = bcum[L - 1:L, :]
    w_end_all = jnp.exp(b_last - bcum) * dt
    dec_end = jnp.exp(b_last)

    for gi in range(S_GROUPS):
        Bg = Bm[:, gi * N:(gi + 1) * N].astype(BF16)
        Cg = Cm[:, gi * N:(gi + 1) * N].astype(BF16)
        cb = _dot_nt(Cg, Bg)
        s_g = s_s[gi * hg * P:(gi + 1) * hg * P, :]
        y_inter = _dot_nt(Cg, s_g.astype(BF16))
        xw_parts = []
        for hh in range(hg):
            idx = gi * hg + hh
            ln = GATE_DT + idx
            bc = bcum[:, ln:ln + 1]
            br = b_rows[idx:idx + 1, :]
            dtr = dt_rows[idx:idx + 1, :]
            decay = jnp.exp(jnp.where(causal, bc - br, NEG_BIG))
            mw = cb * decay * dtr
            xs_h = xs[:, idx * P:(idx + 1) * P]
            y_h = (_dot(mw.astype(BF16), xs_h.astype(BF16))
                   + eb[:, ln:ln + 1] * y_inter[:, hh * P:(hh + 1) * P]
                   + hp[2:3, ln:ln + 1] * xs_h)
            y_s[:, idx * P:(idx + 1) * P] = y_h
            xw_parts.append((xs_h * w_end_all[:, ln:ln + 1]).astype(BF16))
        xw = jnp.concatenate(xw_parts, axis=1)
        upd = _dot_tn(xw, Bg)
        for hh in range(hg):
            idx = gi * hg + hh
            ln = GATE_DT + idx
            r0 = idx * P
            s_s[r0:r0 + P, :] = dec_end[:, ln:ln + 1] * s_s[r0:r0 + P, :] + upd[hh * P:(hh + 1) * P, :]

    gw = W // S_GROUPS
    for gi in range(S_GROUPS):
        sl = slice(gi * gw, (gi + 1) * gw)
        z = z_ref[:, sl]
        gg = y_s[:, sl] * (z * _sigmoid(z))
        gg = gg * lax.rsqrt(jnp.mean(gg * gg, axis=1, keepdims=True) + GN_EPS)
        y_ref[:, sl] = (gg * nw_ref[:, sl]).astype(y_ref.dtype)

    @pl.when(c == nc - 1)
    def _():
        s_out[0] = s_s[...]


def _ssd(oz, xbc, gates, conv_w, conv_b, head_params, norm_w, s0, cv0, *, nseq, T, L):
    W = oz.shape[1] // 2
    nheads = W // S_HEADDIM
    CD = xbc.shape[1]
    nc = T // L
    rb = lambda b, c: b * nc + c
    kern = functools.partial(_ssd_kernel, L=L, nheads=nheads)
    return pl.pallas_call(
        kern,
        grid=(nseq, nc),
        in_specs=[pl.BlockSpec((L, W), lambda b, c: (rb(b, c), 1)),
                  pl.BlockSpec((L, CD), lambda b, c: (rb(b, c), 0)),
                  pl.BlockSpec((L, LANES), lambda b, c: (rb(b, c), 0)),
                  pl.BlockSpec((S_CONV, CD), lambda b, c: (0, 0)),
                  pl.BlockSpec((1, CD), lambda b, c: (0, 0)),
                  pl.BlockSpec((SUBLANES, LANES), lambda b, c: (0, 0)),
                  pl.BlockSpec((1, W), lambda b, c: (0, 0)),
                  pl.BlockSpec((1, nheads * S_HEADDIM, S_STATE), lambda b, c: (b, 0, 0)),
                  pl.BlockSpec((1, S_CONV - 1, CD), lambda b, c: (b, 0, 0))],
        out_specs=[pl.BlockSpec((L, W), lambda b, c: (rb(b, c), 0)),
                   pl.BlockSpec((1, nheads * S_HEADDIM, S_STATE), lambda b, c: (b, 0, 0)),
                   pl.BlockSpec((1, SUBLANES, CD), lambda b, c: (b, 0, 0))],
        out_shape=[jax.ShapeDtypeStruct((nseq * T, W), BF16),
                   jax.ShapeDtypeStruct((nseq, nheads * S_HEADDIM, S_STATE), F32),
                   jax.ShapeDtypeStruct((nseq, SUBLANES, CD), F32)],
        scratch_shapes=[pltpu.VMEM((nheads * S_HEADDIM, S_STATE), F32),
                        pltpu.VMEM((SUBLANES + L, CD), F32),
                        pltpu.VMEM((L, W), F32)],
        compiler_params=_params(("parallel", "arbitrary")),
        name="ssd",
    )(oz, xbc, gates, conv_w, conv_b, head_params, norm_w, s0, cv0)


def _outproj_kernel(hm_ref, hs_ref, wt_ref, wb_ref, x_ref, g_ref, b_ref, y_ref, yb_ref, *, alpha):
    mix = _dot(hm_ref[...], wt_ref[...]) + _dot(hs_ref[...], wb_ref[...])
    y = _layer_norm(alpha * x_ref[...] + mix, g_ref[...], b_ref[...])
    y_ref[...] = y
    yb_ref[...] = y.astype(BF16)


def _outproj(hm, hs, w_top, w_bot, x, ln_g, ln_b, *, alpha):
    m, d = x.shape
    w = hm.shape[1]
    tm = _largest_divisor(m, 512, 16)
    kern = functools.partial(_outproj_kernel, alpha=alpha)
    return pl.pallas_call(
        kern,
        grid=(m // tm,),
        in_specs=[pl.BlockSpec((tm, w), lambda i: (i, 0)),
                  pl.BlockSpec((tm, w), lambda i: (i, 0)),
                  pl.BlockSpec((w, d), lambda i: (0, 0)),
                  pl.BlockSpec((w, d), lambda i: (0, 0)),
                  pl.BlockSpec((tm, d), lambda i: (i, 0)),
                  pl.BlockSpec((1, d), lambda i: (0, 0)),
                  pl.BlockSpec((1, d), lambda i: (0, 0))],
        out_specs=[pl.BlockSpec((tm, d), lambda i: (i, 0)),
                   pl.BlockSpec((tm, d), lambda i: (i, 0))],
        out_shape=[jax.ShapeDtypeStruct((m, d), F32),
                   jax.ShapeDtypeStruct((m, d), BF16)],
        compiler_params=_params(("parallel",)),
        name="out_proj_ln",
    )(hm, hs, w_top, w_bot, x, ln_g, ln_b)


def _ffn_kernel(xb_ref, xf_ref, wg_ref, wv_ref, cwg_ref, cwv_ref, cbg_ref, cbv_ref, wd_ref,
                hg0_ref, hv0_ref, lg_ref, lb_ref,
                y_ref, yb_ref, og_ref, ov_ref,
                acc, extg, extv, *, sh, hr, tps, alpha):
    i = pl.program_id(0)
    j = pl.program_id(1)
    nj = pl.num_programs(1)
    tm = xb_ref.shape[0]
    first = (i % tps) == 0

    def conv(up, ext, h0_ref, cw_ref, cb_ref, o_ref):
        @pl.when(first)
        def _():
            ext[0:hr, :] = h0_ref[0]

        @pl.when(jnp.logical_not(first))
        def _():
            ext[0:hr, :] = o_ref[0, j]

        ext[hr:hr + tm, :] = up
        o_ref[0, j] = up[tm - hr:tm, :]
        return (ext[hr - 2 * sh:hr - 2 * sh + tm, :] * cw_ref[0:1, :]
                + ext[hr - sh:hr - sh + tm, :] * cw_ref[1:2, :]
                + up * cw_ref[2:3, :] + cb_ref[...])

    x = xb_ref[...]
    cg = conv(_dot(x, wg_ref[...]), extg, hg0_ref, cwg_ref, cbg_ref, og_ref)
    cv = conv(_dot(x, wv_ref[...]), extv, hv0_ref, cwv_ref, cbv_ref, ov_ref)
    act = (cg * _sigmoid(cg) * cv).astype(BF16)
    part = _dot(act, wd_ref[...])

    @pl.when(j == 0)
    def _():
        acc[...] = part

    @pl.when(j > 0)
    def _():
        acc[...] += part

    @pl.when(j == nj - 1)
    def _():
        y = _layer_norm(alpha * xf_ref[...] + acc[...], lg_ref[...], lb_ref[...])
        y_ref[...] = y
        yb_ref[...] = y.astype(BF16)


def _ffn(xb, xf, w_g, w_v, cw_g, cw_v, cb_g, cb_v, w_d, h0_g, h0_v, ln_g, ln_b, *,
         nseq, tm, tf, sh, hr, alpha):
    m, d = xf.shape
    fp = w_g.shape[1]
    nj = fp // tf
    ni = m // tm
    tps = ni // nseq
    kern = functools.partial(_ffn_kernel, sh=sh, hr=hr, tps=tps, alpha=alpha)
    y_f, y_b, og, ov = pl.pallas_call(
        kern,
        grid=(ni, nj),
        in_specs=[pl.BlockSpec((tm, d), lambda i, j: (i, 0)),
                  pl.BlockSpec((tm, d), lambda i, j: (i, 0)),
                  pl.BlockSpec((d, tf), lambda i, j: (0, j)),
                  pl.BlockSpec((d, tf), lambda i, j: (0, j)),
                  pl.BlockSpec((FFN_CONV, tf), lambda i, j: (0, j)),
                  pl.BlockSpec((FFN_CONV, tf), lambda i, j: (0, j)),
                  pl.BlockSpec((1, tf), lambda i, j: (0, j)),
                  pl.BlockSpec((1, tf), lambda i, j: (0, j)),
                  pl.BlockSpec((tf, d), lambda i, j: (j, 0)),
                  pl.BlockSpec((1, hr, tf), lambda i, j: (i // tps, 0, j)),
                  pl.BlockSpec((1, hr, tf), lambda i, j: (i // tps, 0, j)),
                  pl.BlockSpec((1, d), lambda i, j: (0, 0)),
                  pl.BlockSpec((1, d), lambda i, j: (0, 0))],
        out_specs=[pl.BlockSpec((tm, d), lambda i, j: (i, 0)),
                   pl.BlockSpec((tm, d), lambda i, j: (i, 0)),
                   pl.BlockSpec((1, nj, hr, tf), lambda i, j: (i // tps, 0, 0, 0)),
                   pl.BlockSpec((1, nj, hr, tf), lambda i, j: (i // tps, 0, 0, 0))],
        out_shape=[jax.ShapeDtypeStruct((m, d), F32),
                   jax.ShapeDtypeStruct((m, d), BF16),
                   jax.ShapeDtypeStruct((nseq, nj, hr, tf), F32),
                   jax.ShapeDtypeStruct((nseq, nj, hr, tf), F32)],
        scratch_shapes=[pltpu.VMEM((tm, d), F32),
                        pltpu.VMEM((hr + tm, tf), F32),
                        pltpu.VMEM((hr + tm, tf), F32)],
        compiler_params=_params(("arbitrary", "arbitrary")),
        name="conv_ffn_ln",
    )(xb, xf, w_g, w_v, cw_g, cw_v, cb_g, cb_v, w_d, h0_g, h0_v, ln_g, ln_b)
    unt = lambda a: a.transpose(0, 2, 1, 3).reshape(nseq, hr, fp)
    return y_f, y_b, unt(og), unt(ov)


def _prep_layer(l, w_in, b_i, b_f, m_norm_w, s_conv_w, s_conv_b, dt_bias, A_log, D_skip, s_norm_w, w_out,
                ln1_g, ln1_b, w_up, f_conv_w, f_conv_b, w_down, ln2_g, ln2_b):
    d = w_in.shape[1]
    mw = m_norm_w.shape[1]
    sw = s_norm_w.shape[1]
    cd = s_conv_w.shape[2]
    nh = dt_bias.shape[1]
    dff = w_down.shape[1]
    fp = -(-dff // MXU_DIM) * MXU_DIM
    o = 0
    cols = {}
    for name, size in (("q", mw), ("k", mw), ("v", mw), ("o", mw), ("i", M_HEADS), ("f", M_HEADS),
                       ("z", sw), ("xbc", cd), ("dt", nh)):
        cols[name] = (o, o + size)
        o += size
    wl = w_in[l]
    cut = lambda name: wl[:, cols[name][0]:cols[name][1]]
    dk = mw // M_HEADS
    p = {}
    p["w_qkv"] = jnp.concatenate([cut("q"), cut("k") * (dk ** -0.5), cut("v")], axis=1).astype(BF16)
    p["w_oz"] = jnp.concatenate([cut("o"), cut("z")], axis=1).astype(BF16)
    p["w_xbc"] = cut("xbc").astype(BF16)
    wg = jnp.concatenate([cut("i"), cut("f"), cut("dt")], axis=1)
    p["w_gate"] = jnp.pad(wg, ((0, 0), (0, LANES - wg.shape[1]))).astype(BF16)
    gb = jnp.concatenate([b_i[l], b_f[l]])
    p["gate_bias"] = jnp.pad(gb, (0, LANES - gb.shape[0]))[None, :]
    p["m_norm_w"] = m_norm_w[l][None, :]
    p["s_conv_w"] = s_conv_w[l]
    p["s_conv_b"] = s_conv_b[l][None, :]
    hp = jnp.stack([dt_bias[l], A_log[l], D_skip[l]])
    p["head_params"] = jnp.pad(hp, ((0, SUBLANES - 3), (GATE_DT, LANES - GATE_DT - nh)))
    p["s_norm_w"] = s_norm_w[l][None, :]
    p["w_out_top"] = w_out[l][:mw].astype(BF16)
    p["w_out_bot"] = w_out[l][mw:].astype(BF16)
    p["ln1_g"] = ln1_g[l][None, :]
    p["ln1_b"] = ln1_b[l][None, :]
    padc = lambda a: jnp.pad(a, ((0, 0), (0, fp - dff)))
    p["w_up_g"] = padc(w_up[l][:, :dff]).astype(BF16)
    p["w_up_v"] = padc(w_up[l][:, dff:]).astype(BF16)
    p["cw_g"] = padc(f_conv_w[l][:, :dff])
    p["cw_v"] = padc(f_conv_w[l][:, dff:])
    p["cb_g"] = padc(f_conv_b[l][None, :dff])
    p["cb_v"] = padc(f_conv_b[l][None, dff:])
    p["w_down"] = jnp.pad(w_down[l], ((0, fp - dff), (0, 0))).astype(BF16)
    p["ln2_g"] = ln2_g[l][None, :]
    p["ln2_b"] = ln2_b[l][None, :]
    p["dff"] = dff
    p["fp"] = fp
    return p


def _layer(xf, xb, p, st, *, nseq, T, L, ffn, alpha):
    C0, n0, m0, S0, sc0, hg0, hv0 = st
    qkv = _matmul(xb, p["w_qkv"], BF16)
    oz = _matmul(xb, p["w_oz"], F32)
    xbc = _matmul(xb, p["w_xbc"], F32)
    gates = _matmul(xb, p["w_gate"], F32)
    hm, C, n, m = _mlstm(qkv, oz, gates, p["gate_bias"], p["m_norm_w"], C0, n0, m0, nseq=nseq, T=T, L=L)
    hs, S, sconv = _ssd(oz, xbc, gates, p["s_conv_w"], p["s_conv_b"], p["head_params"], p["s_norm_w"],
                        S0, sc0, nseq=nseq, T=T, L=L)
    x1f, x1b = _outproj(hm, hs, p["w_out_top"], p["w_out_bot"], xf, p["ln1_g"], p["ln1_b"], alpha=alpha)
    d = xf.shape[1]
    tf = MXU_DIM
    if ffn["time_major"]:
        tr = lambda a: a.reshape(nseq, T, d).transpose(1, 0, 2).reshape(nseq * T, d)
        x1f_t, x1b_t = tr(x1f), tr(x1b)
        tm = _largest_divisor(nseq * T, 256, (FFN_CONV - 1) * nseq)
        y_f, y_b, og, ov = _ffn(x1b_t, x1f_t, p["w_up_g"], p["w_up_v"], p["cw_g"], p["cw_v"], p["cb_g"], p["cb_v"],
                                p["w_down"], hg0, hv0, p["ln2_g"], p["ln2_b"],
                                nseq=1, tm=tm, tf=tf, sh=nseq, hr=(FFN_CONV - 1) * nseq, alpha=alpha)
        tb = lambda a: a.reshape(T, nseq, d).transpose(1, 0, 2).reshape(nseq * T, d)
        y_f, y_b = tb(y_f), tb(y_b)
        dff = p["dff"]
        fconv = jnp.concatenate([og[0, :, :dff], ov[0, :, :dff]], axis=1)
        fconv = fconv.reshape(FFN_CONV - 1, nseq, 2 * dff).transpose(1, 0, 2)
    else:
        tm = _largest_divisor(T, 512, 16)
        y_f, y_b, og, ov = _ffn(x1b, x1f, p["w_up_g"], p["w_up_v"], p["cw_g"], p["cw_v"], p["cb_g"], p["cb_v"],
                                p["w_down"], hg0, hv0, p["ln2_g"], p["ln2_b"],
                                nseq=nseq, tm=tm, tf=tf, sh=1, hr=SUBLANES, alpha=alpha)
        dff = p["dff"]
        k = FFN_CONV - 1
        fconv = jnp.concatenate([og[:, SUBLANES - k:, :dff], ov[:, SUBLANES - k:, :dff]], axis=2)
    new_state = (C, n, m[:, 0, GATE_F:GATE_F + M_HEADS], S, sconv[:, SUBLANES - (S_CONV - 1):, :], fconv)
    return y_f, y_b, new_state


def kernel(x_prompt, x_sample, state_mlstm_C, state_mlstm_n, state_mlstm_m, state_ssm, state_ssm_conv,
           state_ffn_conv, w_in, mlstm_b_i, mlstm_b_f, mlstm_norm_w, ssm_conv_w, ssm_conv_b, ssm_dt_bias,
           ssm_A_log, ssm_D, ssm_norm_w, w_out, ln1_g, ln1_b, ffn_w_up, ffn_conv_w, ffn_conv_b,
           ffn_w_down, ln2_g, ln2_b):
    weights = (w_in, mlstm_b_i, mlstm_b_f, mlstm_norm_w, ssm_conv_w, ssm_conv_b, ssm_dt_bias, ssm_A_log,
               ssm_D, ssm_norm_w, w_out, ln1_g, ln1_b, ffn_w_up, ffn_conv_w, ffn_conv_b, ffn_w_down,
               ln2_g, ln2_b)
    depth = w_in.shape[0]
    alpha = float((2 * depth) ** 0.25)
    B, T, D = x_prompt.shape
    Bs, Ts, _ = x_sample.shape
    mw = mlstm_norm_w.shape[1]
    dk = mw // M_HEADS
    nh = ssm_dt_bias.shape[1]
    cd = ssm_conv_w.shape[2]
    Lp = PROMPT_CHUNK if T % PROMPT_CHUNK == 0 else T
    Ls = PROMPT_CHUNK if Ts % PROMPT_CHUNK == 0 else Ts

    xpf = x_prompt.reshape(B * T, D)
    xpb = xpf.astype(BF16)
    xsf = x_sample.reshape(Bs * Ts, D)
    xsb = xsf.astype(BF16)
    p_states, s_states = [], []
    for l in range(depth):
        p = _prep_layer(l, *weights)
        fp, dff = p["fp"], p["dff"]
        zst = (jnp.zeros((B, M_HEADS, dk, dk), F32), jnp.zeros((B, M_HEADS, dk), F32),
               jnp.zeros((B, 1, LANES), F32), jnp.zeros((B, nh * S_HEADDIM, S_STATE), F32),
               jnp.zeros((B, S_CONV - 1, cd), F32),
               jnp.zeros((B, SUBLANES, fp), F32), jnp.zeros((B, SUBLANES, fp), F32))
        xpf, xpb, ns = _layer(xpf, xpb, p, zst, nseq=B, T=T, L=Lp, ffn=dict(time_major=False), alpha=alpha)
        p_states.append(ns)
        m0 = jnp.pad(state_mlstm_m[l], ((0, 0), (GATE_F, LANES - GATE_F - M_HEADS)))[:, None, :]
        fc = state_ffn_conv[l].transpose(1, 0, 2).reshape(1, (FFN_CONV - 1) * Bs, 2 * dff)
        padf = lambda a: jnp.pad(a, ((0, 0), (0, 0), (0, fp - dff)))
        sst = (state_mlstm_C[l], state_mlstm_n[l], m0,
               state_ssm[l].reshape(Bs, nh * S_HEADDIM, S_STATE), state_ssm_conv[l],
               padf(fc[:, :, :dff]), padf(fc[:, :, dff:]))
        xsf, xsb, ns = _layer(xsf, xsb, p, sst, nseq=Bs, T=Ts, L=Ls, ffn=dict(time_major=True), alpha=alpha)
        s_states.append(ns)

    def stack(states, nseq):
        C, n, m, S, sc, fc = (jnp.stack([s[j] for s in states], axis=0) for j in range(6))
        return C, n, m, S.reshape(depth, nseq, nh, S_HEADDIM, S_STATE), sc, fc

    pC, pn, pm, pS, psc, pfc = stack(p_states, B)
    sC, sn, sm, sS, ssc, sfc = stack(s_states, Bs)
    return (xpf.reshape(B, T, D), xsf.reshape(Bs, Ts, D), pC, pn, pm, pS, psc, pfc,
            sC, sn, sm, sS, ssc, sfc)
```

```python
import functools

import jax
import jax.numpy as jnp
from jax import lax
from jax.experimental import pallas as pl
from jax.experimental.pallas import tpu as pltpu

F32 = jnp.float32
BF16 = jnp.bfloat16
HIGHEST = lax.Precision.HIGHEST

M_HEADS = 4
S_HEADDIM = 64
S_GROUPS = 2
S_STATE = 128
S_CONV = 4
FFN_CONV = 3
LN_EPS = 1e-5
GN_EPS = 1e-6
NEG_BIG = -1e30

LANES = 128
SUBLANES = 8
MXU_DIM = 256
VMEM_LIMIT_BYTES = 52 * 1024 * 1024

GATE_I = 0
GATE_F = 4
GATE_DT = 8

PROMPT_CHUNK = 128
FFN_TILE = 2 * MXU_DIM
FFN_ROW_CHUNK = 32


def _largest_divisor(n, target, mult):
    best = None
    for d in range(mult, min(n, target) + 1, mult):
        if n % d == 0:
            best = d
    if best is None:
        raise ValueError(f"no tile for {n=} {target=} {mult=}")
    return best


def _params(sem):
    return pltpu.CompilerParams(dimension_semantics=sem, vmem_limit_bytes=VMEM_LIMIT_BYTES)


def _dot(a, b):
    return jnp.dot(a, b, preferred_element_type=F32)


def _dot_nt(a, b, precision=None):
    return lax.dot_general(a, b, (((1,), (1,)), ((), ())), precision=precision, preferred_element_type=F32)


def _dot_tn(a, b):
    return lax.dot_general(a, b, (((0,), (0,)), ((), ())), preferred_element_type=F32)


def _sigmoid(x):
    return 1.0 / (1.0 + jnp.exp(-x))


def _softplus(x):
    return jnp.maximum(x, 0.0) + jnp.log1p(jnp.exp(-jnp.abs(x)))


def _layer_norm(r, g, b):
    mu = jnp.mean(r, axis=-1, keepdims=True)
    d = r - mu
    var = jnp.mean(d * d, axis=-1, keepdims=True)
    return d * lax.rsqrt(var + LN_EPS) * g + b


def _mm_kernel(x_ref, w_ref, o_ref):
    o_ref[...] = _dot(x_ref[...], w_ref[...]).astype(o_ref.dtype)


def _matmul(x, w, out_dtype, tm_target=512, tn_target=512):
    m, k = x.shape
    n = w.shape[1]
    tm = _largest_divisor(m, tm_target, 16)
    tn = _largest_divisor(n, tn_target, LANES)
    return pl.pallas_call(
        _mm_kernel,
        grid=(m // tm, n // tn),
        in_specs=[pl.BlockSpec((tm, k), lambda i, j: (i, 0)),
                  pl.BlockSpec((k, tn), lambda i, j: (0, j))],
        out_specs=pl.BlockSpec((tm, tn), lambda i, j: (i, j)),
        out_shape=jax.ShapeDtypeStruct((m, n), out_dtype),
        compiler_params=_params(("parallel", "parallel")),
        name="in_proj",
    )(x, w)


def _mlstm_kernel(q_ref, k_ref, v_ref, o_ref, g_ref, gb_ref, nw_ref, c0_ref, n0_ref, m0_ref,
                  h_ref, c_out, n_out, m_out, c_s, n_s, m_s, *, L, dk):
    c = pl.program_id(1)
    nc = pl.num_programs(1)

    @pl.when(c == 0)
    def _():
        c_s[...] = c0_ref[0]
        n_s[...] = n0_ref[0]
        m_s[...] = m0_ref[0]

    g = g_ref[...] + gb_ref[...]
    lf = jnp.minimum(g, 0.0) - jnp.log1p(jnp.exp(-jnp.abs(g)))
    row = lax.broadcasted_iota(jnp.int32, (L, L), 0)
    col = lax.broadcasted_iota(jnp.int32, (L, L), 1)
    causal = col <= row
    tri = jnp.where(causal, 1.0, 0.0).astype(F32)
    bcum = jnp.dot(tri, lf, precision=HIGHEST, preferred_element_type=F32)
    lane = lax.broadcasted_iota(jnp.int32, (L, LANES), 1)
    packed = jnp.where(lane < GATE_F, g, bcum)
    sel = jnp.where(lax.broadcasted_iota(jnp.int32, (SUBLANES, LANES), 0)
                    == lax.broadcasted_iota(jnp.int32, (SUBLANES, LANES), 1), 1.0, 0.0).astype(F32)
    rows = _dot_nt(sel, packed, precision=HIGHEST)
    m_vec = m_s[...]
    m_lane = lax.broadcasted_iota(jnp.int32, (1, LANES), 1)
    m_new = m_vec

    for h in range(M_HEADS):
        sl = slice(h * dk, (h + 1) * dk)
        bc = bcum[:, GATE_F + h:GATE_F + h + 1]
        igc = g[:, GATE_I + h:GATE_I + h + 1]
        igr = rows[GATE_I + h:GATE_I + h + 1, :]
        br = rows[GATE_F + h:GATE_F + h + 1, :]
        m_h = m_vec[:, GATE_F + h:GATE_F + h + 1]
        q = q_ref[:, sl]
        k = k_ref[:, sl]
        v = v_ref[:, sl]
        C = c_s[h]
        n = n_s[h:h + 1, :]

        log_w = jnp.where(causal, bc - br + igr, NEG_BIG)
        log_inter = bc + m_h
        m_t = jnp.maximum(log_inter, jnp.max(log_w, axis=1, keepdims=True))
        w_intra = jnp.exp(log_w - m_t)
        w_inter = jnp.exp(log_inter - m_t)
        s = _dot_nt(q, k) * w_intra
        num = _dot(s.astype(BF16), v) + w_inter * _dot(q, C.astype(BF16))
        qn = jnp.sum(q.astype(F32) * n, axis=1, keepdims=True)
        den = jnp.sum(s, axis=1, keepdims=True) + w_inter * qn
        hh = num / jnp.maximum(jnp.abs(den), jnp.exp(-m_t))

        m_end = m_t[L - 1:L, :]
        b_end = bc[L - 1:L, :]
        w_end = jnp.exp(b_end - bc + igc - m_end)
        decay = jnp.exp(b_end + m_h - m_end)
        kw = k.astype(F32) * w_end
        c_s[h] = decay * C + _dot_tn(kw.astype(BF16), v)
        n_s[h:h + 1, :] = decay * n + jnp.sum(kw, axis=0, keepdims=True)
        m_new = jnp.where(m_lane == GATE_F + h, m_end, m_new)

        mu = jnp.mean(hh, axis=1, keepdims=True)
        d = hh - mu
        var = jnp.mean(d * d, axis=1, keepdims=True)
        hn = d * lax.rsqrt(var + GN_EPS) * nw_ref[:, sl]
        h_ref[:, sl] = (hn * _sigmoid(o_ref[:, sl])).astype(h_ref.dtype)

    m_s[...] = m_new

    @pl.when(c == nc - 1)
    def _():
        c_out[0] = c_s[...]
        n_out[0] = n_s[...]
        m_out[0] = m_s[...]


def _mlstm(qkv, oz, gates, gate_bias, norm_w, c0, n0, m0, *, nseq, T, L):
    W = qkv.shape[1] // 3
    dk = W // M_HEADS
    nc = T // L
    rb = lambda b, c: b * nc + c
    kern = functools.partial(_mlstm_kernel, L=L, dk=dk)
    return pl.pallas_call(
        kern,
        grid=(nseq, nc),
        in_specs=[pl.BlockSpec((L, W), lambda b, c: (rb(b, c), 0)),
                  pl.BlockSpec((L, W), lambda b, c: (rb(b, c), 1)),
                  pl.BlockSpec((L, W), lambda b, c: (rb(b, c), 2)),
                  pl.BlockSpec((L, W), lambda b, c: (rb(b, c), 0)),
                  pl.BlockSpec((L, LANES), lambda b, c: (rb(b, c), 0)),
                  pl.BlockSpec((1, LANES), lambda b, c: (0, 0)),
                  pl.BlockSpec((1, W), lambda b, c: (0, 0)),
                  pl.BlockSpec((1, M_HEADS, dk, dk), lambda b, c: (b, 0, 0, 0)),
                  pl.BlockSpec((1, M_HEADS, dk), lambda b, c: (b, 0, 0)),
                  pl.BlockSpec((1, 1, LANES), lambda b, c: (b, 0, 0))],
        out_specs=[pl.BlockSpec((L, W), lambda b, c: (rb(b, c), 0)),
                   pl.BlockSpec((1, M_HEADS, dk, dk), lambda b, c: (b, 0, 0, 0)),
                   pl.BlockSpec((1, M_HEADS, dk), lambda b, c: (b, 0, 0)),
                   pl.BlockSpec((1, 1, LANES), lambda b, c: (b, 0, 0))],
        out_shape=[jax.ShapeDtypeStruct((nseq * T, W), BF16),
                   jax.ShapeDtypeStruct((nseq, M_HEADS, dk, dk), F32),
                   jax.ShapeDtypeStruct((nseq, M_HEADS, dk), F32),
                   jax.ShapeDtypeStruct((nseq, 1, LANES), F32)],
        scratch_shapes=[pltpu.VMEM((M_HEADS, dk, dk), F32),
                        pltpu.VMEM((M_HEADS, dk), F32),
                        pltpu.VMEM((1, LANES), F32)],
        compiler_params=_params(("parallel", "arbitrary")),
        name="mlstm",
    )(qkv, qkv, qkv, oz, gates, gate_bias, norm_w, c0, n0, m0)


def _ssd_kernel(z_ref, x_ref, g_ref, cw_ref, cb_ref, hp_ref, nw_ref, s0_ref, cv0_ref,
                y_ref, s_out, cv_out, s_s, xp_s, y_s, *, L, nheads):
    c = pl.program_id(1)
    nc = pl.num_programs(1)
    P = S_HEADDIM
    N = S_STATE
    hg = nheads // S_GROUPS
    W = nheads * P
    halo = SUBLANES
    kc = S_CONV

    @pl.when(c == 0)
    def _():
        s_s[...] = s0_ref[0]
        xp_s[0:halo, :] = jnp.zeros((halo, xp_s.shape[1]), F32)
        xp_s[halo - (kc - 1):halo, :] = cv0_ref[0]

    xp_s[halo:halo + L, :] = x_ref[...]
    conv = cb_ref[...] + xp_s[halo:halo + L, :] * cw_ref[kc - 1:kc, :]
    for j in range(kc - 1):
        off = halo - (kc - 1) + j
        conv = conv + xp_s[off:off + L, :] * cw_ref[j:j + 1, :]
    tail = xp_s[L:L + halo, :]
    xp_s[0:halo, :] = tail
    cv_out[0] = tail
    xbc = conv * _sigmoid(conv)
    xs = xbc[:, 0:W]
    Bm = xbc[:, W:W + S_GROUPS * N]
    Cm = xbc[:, W + S_GROUPS * N:W + 2 * S_GROUPS * N]

    hp = hp_ref[...]
    dt = _softplus(g_ref[...] + hp[0:1, :])
    a = dt * (-jnp.exp(hp[1:2, :]))
    row = lax.broadcasted_iota(jnp.int32, (L, L), 0)
    col = lax.broadcasted_iota(jnp.int32, (L, L), 1)
    causal = col <= row
    tri = jnp.where(causal, 1.0, 0.0).astype(F32)
    bcum = jnp.dot(tri, a, precision=HIGHEST, preferred_element_type=F32)
    sel = jnp.where(lax.broadcasted_iota(jnp.int32, (nheads, LANES), 0) + GATE_DT
                    == lax.broadcasted_iota(jnp.int32, (nheads, LANES), 1), 1.0, 0.0).astype(F32)
    b_rows = _dot_nt(sel, bcum, precision=HIGHEST)
    dt_rows = _dot_nt(sel, dt, precision=HIGHEST)
    eb = jnp.exp(bcum)
    b_last = bcum[L - 1:L, :]
    w_end_all = jnp.exp(b_last - bcum) * dt
    dec_end = jnp.exp(b_last)

    for gi in range(S_GROUPS):
        Bg = Bm[:, gi * N:(gi + 1) * N].astype(BF16)
        Cg = Cm[:, gi * N:(gi + 1) * N].astype(BF16)
        cb = _dot_nt(Cg, Bg)
        s_g = s_s[gi * hg * P:(gi + 1) * hg * P, :]
        y_inter = _dot_nt(Cg, s_g.astype(BF16))
        xw_parts = []
        for hh in range(hg):
            idx = gi * hg + hh
            ln = GATE_DT + idx
            bc = bcum[:, ln:ln + 1]
            br = b_rows[idx:idx + 1, :]
            dtr = dt_rows[idx:idx + 1, :]
            decay = jnp.exp(jnp.where(causal, bc - br, NEG_BIG))
            mw = cb * decay * dtr
            xs_h = xs[:, idx * P:(idx + 1) * P]
            y_h = (_dot(mw.astype(BF16), xs_h.astype(BF16))
                   + eb[:, ln:ln + 1] * y_inter[:, hh * P:(hh + 1) * P]
                   + hp[2:3, ln:ln + 1] * xs_h)
            y_s[:, idx * P:(idx + 1) * P] = y_h
            xw_parts.append((xs_h * w_end_all[:, ln:ln + 1]).astype(BF16))
        xw = jnp.concatenate(xw_parts, axis=1)
        upd = _dot_tn(xw, Bg)
        for hh in range(hg):
            idx = gi * hg + hh
            ln = GATE_DT + idx
            r0 = idx * P
            s_s[r0:r0 + P, :] = dec_end[:, ln:ln + 1] * s_s[r0:r0 + P, :] + upd[hh * P:(hh + 1) * P, :]

    gw = W // S_GROUPS
    for gi in range(S_GROUPS):
        sl = slice(gi * gw, (gi + 1) * gw)
        z = z_ref[:, sl]
        gg = y_s[:, sl] * (z * _sigmoid(z))
        gg = gg * lax.rsqrt(jnp.mean(gg * gg, axis=1, keepdims=True) + GN_EPS)
        y_ref[:, sl] = (gg * nw_ref[:, sl]).astype(y_ref.dtype)

    @pl.when(c == nc - 1)
    def _():
        s_out[0] = s_s[...]


def _ssd(oz, xbc, gates, conv_w, conv_b, head_params, norm_w, s0, cv0, *, nseq, T, L):
    W = oz.shape[1] // 2
    nheads = W // S_HEADDIM
    CD = xbc.shape[1]
    nc = T // L
    rb = lambda b, c: b * nc + c
    kern = functools.partial(_ssd_kernel, L=L, nheads=nheads)
    return pl.pallas_call(
        kern,
        grid=(nseq, nc),
        in_specs=[pl.BlockSpec((L, W), lambda b, c: (rb(b, c), 1)),
                  pl.BlockSpec((L, CD), lambda b, c: (rb(b, c), 0)),
                  pl.BlockSpec((L, LANES), lambda b, c: (rb(b, c), 0)),
                  pl.BlockSpec((S_CONV, CD), lambda b, c: (0, 0)),
                  pl.BlockSpec((1, CD), lambda b, c: (0, 0)),
                  pl.BlockSpec((SUBLANES, LANES), lambda b, c: (0, 0)),
                  pl.BlockSpec((1, W), lambda b, c: (0, 0)),
                  pl.BlockSpec((1, nheads * S_HEADDIM, S_STATE), lambda b, c: (b, 0, 0)),
                  pl.BlockSpec((1, S_CONV - 1, CD), lambda b, c: (b, 0, 0))],
        out_specs=[pl.BlockSpec((L, W), lambda b, c: (rb(b, c), 0)),
                   pl.BlockSpec((1, nheads * S_HEADDIM, S_STATE), lambda b, c: (b, 0, 0)),
                   pl.BlockSpec((1, SUBLANES, CD), lambda b, c: (b, 0, 0))],
        out_shape=[jax.ShapeDtypeStruct((nseq * T, W), BF16),
                   jax.ShapeDtypeStruct((nseq, nheads * S_HEADDIM, S_STATE), F32),
                   jax.ShapeDtypeStruct((nseq, SUBLANES, CD), F32)],
        scratch_shapes=[pltpu.VMEM((nheads * S_HEADDIM, S_STATE), F32),
                        pltpu.VMEM((SUBLANES + L, CD), F32),
                        pltpu.VMEM((L, W), F32)],
        compiler_params=_params(("parallel", "arbitrary")),
        name="ssd",
    )(oz, xbc, gates, conv_w, conv_b, head_params, norm_w, s0, cv0)


def _outproj_kernel(hm_ref, hs_ref, wt_ref, wb_ref, x_ref, g_ref, b_ref, y_ref, yb_ref, *, alpha):
    mix = _dot(hm_ref[...], wt_ref[...]) + _dot(hs_ref[...], wb_ref[...])
    y = _layer_norm(alpha * x_ref[...] + mix, g_ref[...], b_ref[...])
    y_ref[...] = y
    yb_ref[...] = y.astype(BF16)


def _outproj(hm, hs, w_top, w_bot, x, ln_g, ln_b, *, alpha):
    m, d = x.shape
    w = hm.shape[1]
    tm = _largest_divisor(m, 512, 16)
    kern = functools.partial(_outproj_kernel, alpha=alpha)
    return pl.pallas_call(
        kern,
        grid=(m // tm,),
        in_specs=[pl.BlockSpec((tm, w), lambda i: (i, 0)),
                  pl.BlockSpec((tm, w), lambda i: (i, 0)),
                  pl.BlockSpec((w, d), lambda i: (0, 0)),
                  pl.BlockSpec((w, d), lambda i: (0, 0)),
                  pl.BlockSpec((tm, d), lambda i: (i, 0)),
                  pl.BlockSpec((1, d), lambda i: (0, 0)),
                  pl.BlockSpec((1, d), lambda i: (0, 0))],
        out_specs=[pl.BlockSpec((tm, d), lambda i: (i, 0)),
                   pl.BlockSpec((tm, d), lambda i: (i, 0))],
        out_shape=[jax.ShapeDtypeStruct((m, d), F32),
                   jax.ShapeDtypeStruct((m, d), BF16)],
        compiler_params=_params(("parallel",)),
        name="out_proj_ln",
    )(hm, hs, w_top, w_bot, x, ln_g, ln_b)


def _ffn_kernel(xb_ref, xf_ref, wu_ref, cw_ref, cb_ref, wd_ref, h0_ref, lg_ref, lb_ref,
                y_ref, yb_ref, oh_ref, acc, ext, act, hal, *, sh, hr, tps, rc, alpha):
    i = pl.program_id(0)
    j = pl.program_id(1)
    nj = pl.num_programs(1)
    tm = xb_ref.shape[0]
    tf = wd_ref.shape[1]
    first = (i % tps) == 0

    @pl.when(first)
    def _():
        ext[0:hr, :] = h0_ref[0, 0]

    @pl.when(jnp.logical_not(first))
    def _():
        ext[0:hr, :] = hal[j]

    ext[hr:hr + tm, :] = _dot(xb_ref[...], wu_ref[0])
    last = ext[tm:tm + hr, :]
    hal[j] = last
    oh_ref[0, 0] = last

    for r0 in range(0, tm, rc):
        c = (ext[hr - 2 * sh + r0:hr - 2 * sh + r0 + rc, :] * cw_ref[0, 0:1, :]
             + ext[hr - sh + r0:hr - sh + r0 + rc, :] * cw_ref[0, 1:2, :]
             + ext[hr + r0:hr + r0 + rc, :] * cw_ref[0, 2:3, :] + cb_ref[0])
        cg = c[:, :tf]
        act[r0:r0 + rc, :] = (cg * _sigmoid(cg) * c[:, tf:]).astype(BF16)

    @pl.when(j == 0)
    def _():
        acc[...] = jnp.zeros_like(acc)

    acc[...] += _dot(act[...], wd_ref[0])

    @pl.when(j == nj - 1)
    def _():
        y = _layer_norm(alpha * xf_ref[...] + acc[...], lg_ref[...], lb_ref[...])
        y_ref[...] = y
        yb_ref[...] = y.astype(BF16)


def _ffn(xb, xf, w_u, cw, cb, w_d, h0, ln_g, ln_b, *, nseq, tm, sh, hr, alpha):
    m, d = xf.shape
    nj, tf = w_d.shape[0], w_d.shape[1]
    ni = m // tm
    tps = ni // nseq
    kern = functools.partial(_ffn_kernel, sh=sh, hr=hr, tps=tps, rc=FFN_ROW_CHUNK, alpha=alpha)
    return pl.pallas_call(
        kern,
        grid=(ni, nj),
        in_specs=[pl.BlockSpec((tm, d), lambda i, j: (i, 0)),
                  pl.BlockSpec((tm, d), lambda i, j: (i, 0)),
                  pl.BlockSpec((1, d, 2 * tf), lambda i, j: (j, 0, 0)),
                  pl.BlockSpec((1, FFN_CONV, 2 * tf), lambda i, j: (j, 0, 0)),
                  pl.BlockSpec((1, 1, 2 * tf), lambda i, j: (j, 0, 0)),
                  pl.BlockSpec((1, tf, d), lambda i, j: (j, 0, 0)),
                  pl.BlockSpec((1, 1, hr, 2 * tf), lambda i, j: (i // tps, j, 0, 0)),
                  pl.BlockSpec((1, d), lambda i, j: (0, 0)),
                  pl.BlockSpec((1, d), lambda i, j: (0, 0))],
        out_specs=[pl.BlockSpec((tm, d), lambda i, j: (i, 0)),
                   pl.BlockSpec((tm, d), lambda i, j: (i, 0)),
                   pl.BlockSpec((1, 1, hr, 2 * tf),
                                lambda i, j: (i // tps, jnp.where(i % tps == tps - 1, j, 0), 0, 0))],
        out_shape=[jax.ShapeDtypeStruct((m, d), F32),
                   jax.ShapeDtypeStruct((m, d), BF16),
                   jax.ShapeDtypeStruct((nseq, nj, hr, 2 * tf), F32)],
        scratch_shapes=[pltpu.VMEM((tm, d), F32),
                        pltpu.VMEM((hr + tm, 2 * tf), F32),
                        pltpu.VMEM((tm, tf), BF16),
                        pltpu.VMEM((nj, hr, 2 * tf), F32)],
        compiler_params=_params(("arbitrary", "arbitrary")),
        name="conv_ffn_ln",
    )(xb, xf, w_u, cw, cb, w_d, h0, ln_g, ln_b)


def _ffn_tiles(g, v, tf):
    nj = g.shape[-1] // tf
    lead = g.shape[:-1]
    t = jnp.stack([g.reshape(lead + (nj, tf)), v.reshape(lead + (nj, tf))], axis=-2)
    return jnp.moveaxis(t.reshape(lead + (nj, 2 * tf)), -2, 0)


def _ffn_untile(t, tf):
    nj = t.shape[-3]
    t = jnp.moveaxis(t, -3, -2)
    lead = t.shape[:-2]
    return (t[..., :tf].reshape(lead + (nj * tf,)), t[..., tf:].reshape(lead + (nj * tf,)))


def _prep_layer(l, w_in, b_i, b_f, m_norm_w, s_conv_w, s_conv_b, dt_bias, A_log, D_skip, s_norm_w, w_out,
                ln1_g, ln1_b, w_up, f_conv_w, f_conv_b, w_down, ln2_g, ln2_b):
    d = w_in.shape[1]
    mw = m_norm_w.shape[1]
    sw = s_norm_w.shape[1]
    cd = s_conv_w.shape[2]
    nh = dt_bias.shape[1]
    dff = w_down.shape[1]
    tf = FFN_TILE
    fp = -(-dff // tf) * tf
    o = 0
    cols = {}
    for name, size in (("q", mw), ("k", mw), ("v", mw), ("o", mw), ("i", M_HEADS), ("f", M_HEADS),
                       ("z", sw), ("xbc", cd), ("dt", nh)):
        cols[name] = (o, o + size)
        o += size
    wl = w_in[l]
    cut = lambda name: wl[:, cols[name][0]:cols[name][1]]
    dk = mw // M_HEADS
    p = {}
    p["w_qkv"] = jnp.concatenate([cut("q"), cut("k") * (dk ** -0.5), cut("v")], axis=1).astype(BF16)
    p["w_oz"] = jnp.concatenate([cut("o"), cut("z")], axis=1).astype(BF16)
    p["w_xbc"] = cut("xbc").astype(BF16)
    wg = jnp.concatenate([cut("i"), cut("f"), cut("dt")], axis=1)
    p["w_gate"] = jnp.pad(wg, ((0, 0), (0, LANES - wg.shape[1]))).astype(BF16)
    gb = jnp.concatenate([b_i[l], b_f[l]])
    p["gate_bias"] = jnp.pad(gb, (0, LANES - gb.shape[0]))[None, :]
    p["m_norm_w"] = m_norm_w[l][None, :]
    p["s_conv_w"] = s_conv_w[l]
    p["s_conv_b"] = s_conv_b[l][None, :]
    hp = jnp.stack([dt_bias[l], A_log[l], D_skip[l]])
    p["head_params"] = jnp.pad(hp, ((0, SUBLANES - 3), (GATE_DT, LANES - GATE_DT - nh)))
    p["s_norm_w"] = s_norm_w[l][None, :]
    p["w_out_top"] = w_out[l][:mw].astype(BF16)
    p["w_out_bot"] = w_out[l][mw:].astype(BF16)
    p["ln1_g"] = ln1_g[l][None, :]
    p["ln1_b"] = ln1_b[l][None, :]
    padc = lambda a: jnp.pad(a, ((0, 0), (0, fp - dff)))
    p["w_up"] = _ffn_tiles(padc(w_up[l][:, :dff]).astype(BF16), padc(w_up[l][:, dff:]).astype(BF16), tf)
    p["f_conv_w"] = _ffn_tiles(padc(f_conv_w[l][:, :dff]), padc(f_conv_w[l][:, dff:]), tf)
    p["f_conv_b"] = _ffn_tiles(padc(f_conv_b[l][None, :dff]), padc(f_conv_b[l][None, dff:]), tf)
    p["w_down"] = jnp.pad(w_down[l], ((0, fp - dff), (0, 0))).astype(BF16).reshape(fp // tf, tf, d)
    p["ln2_g"] = ln2_g[l][None, :]
    p["ln2_b"] = ln2_b[l][None, :]
    p["dff"] = dff
    p["fp"] = fp
    return p


def _layer(xf, xb, p, st, *, nseq, T, L, ffn, alpha):
    C0, n0, m0, S0, sc0, h0 = st
    qkv = _matmul(xb, p["w_qkv"], BF16)
    oz = _matmul(xb, p["w_oz"], F32)
    xbc = _matmul(xb, p["w_xbc"], F32)
    gates = _matmul(xb, p["w_gate"], F32)
    hm, C, n, m = _mlstm(qkv, oz, gates, p["gate_bias"], p["m_norm_w"], C0, n0, m0, nseq=nseq, T=T, L=L)
    hs, S, sconv = _ssd(oz, xbc, gates, p["s_conv_w"], p["s_conv_b"], p["head_params"], p["s_norm_w"],
                        S0, sc0, nseq=nseq, T=T, L=L)
    x1f, x1b = _outproj(hm, hs, p["w_out_top"], p["w_out_bot"], xf, p["ln1_g"], p["ln1_b"], alpha=alpha)
    d = xf.shape[1]
    dff = p["dff"]
    ffn_w = (p["w_up"], p["f_conv_w"], p["f_conv_b"], p["w_down"])
    if ffn["time_major"]:
        tr = lambda a: a.reshape(nseq, T, d).transpose(1, 0, 2).reshape(nseq * T, d)
        x1f_t, x1b_t = tr(x1f), tr(x1b)
        tm = _largest_divisor(nseq * T, 256, (FFN_CONV - 1) * nseq)
        y_f, y_b, oh = _ffn(x1b_t, x1f_t, *ffn_w, h0, p["ln2_g"], p["ln2_b"],
                            nseq=1, tm=tm, sh=nseq, hr=(FFN_CONV - 1) * nseq, alpha=alpha)
        tb = lambda a: a.reshape(T, nseq, d).transpose(1, 0, 2).reshape(nseq * T, d)
        y_f, y_b = tb(y_f), tb(y_b)
        og, ov = _ffn_untile(oh, FFN_TILE)
        fconv = jnp.concatenate([og[0, :, :dff], ov[0, :, :dff]], axis=1)
        fconv = fconv.reshape(FFN_CONV - 1, nseq, 2 * dff).transpose(1, 0, 2)
    else:
        tm = _largest_divisor(T, 512, FFN_ROW_CHUNK)
        y_f, y_b, oh = _ffn(x1b, x1f, *ffn_w, h0, p["ln2_g"], p["ln2_b"],
                            nseq=nseq, tm=tm, sh=1, hr=SUBLANES, alpha=alpha)
        og, ov = _ffn_untile(oh, FFN_TILE)
        k = FFN_CONV - 1
        fconv = jnp.concatenate([og[:, SUBLANES - k:, :dff], ov[:, SUBLANES - k:, :dff]], axis=2)
    new_state = (C, n, m[:, 0, GATE_F:GATE_F + M_HEADS], S, sconv[:, SUBLANES - (S_CONV - 1):, :], fconv)
    return y_f, y_b, new_state


def kernel(x_prompt, x_sample, state_mlstm_C, state_mlstm_n, state_mlstm_m, state_ssm, state_ssm_conv,
           state_ffn_conv, w_in, mlstm_b_i, mlstm_b_f, mlstm_norm_w, ssm_conv_w, ssm_conv_b, ssm_dt_bias,
           ssm_A_log, ssm_D, ssm_norm_w, w_out, ln1_g, ln1_b, ffn_w_up, ffn_conv_w, ffn_conv_b,
           ffn_w_down, ln2_g, ln2_b):
    weights = (w_in, mlstm_b_i, mlstm_b_f, mlstm_norm_w, ssm_conv_w, ssm_conv_b, ssm_dt_bias, ssm_A_log,
               ssm_D, ssm_norm_w, w_out, ln1_g, ln1_b, ffn_w_up, ffn_conv_w, ffn_conv_b, ffn_w_down,
               ln2_g, ln2_b)
    depth = w_in.shape[0]
    alpha = float((2 * depth) ** 0.25)
    B, T, D = x_prompt.shape
    Bs, Ts, _ = x_sample.shape
    mw = mlstm_norm_w.shape[1]
    dk = mw // M_HEADS
    nh = ssm_dt_bias.shape[1]
    cd = ssm_conv_w.shape[2]
    Lp = PROMPT_CHUNK if T % PROMPT_CHUNK == 0 else T
    Ls = PROMPT_CHUNK if Ts % PROMPT_CHUNK == 0 else Ts

    xpf = x_prompt.reshape(B * T, D)
    xpb = xpf.astype(BF16)
    xsf = x_sample.reshape(Bs * Ts, D)
    xsb = xsf.astype(BF16)
    p_states, s_states = [], []
    for l in range(depth):
        p = _prep_layer(l, *weights)
        fp, dff = p["fp"], p["dff"]
        zst = (jnp.zeros((B, M_HEADS, dk, dk), F32), jnp.zeros((B, M_HEADS, dk), F32),
               jnp.zeros((B, 1, LANES), F32), jnp.zeros((B, nh * S_HEADDIM, S_STATE), F32),
               jnp.zeros((B, S_CONV - 1, cd), F32),
               jnp.zeros((B, fp // FFN_TILE, SUBLANES, 2 * FFN_TILE), F32))
        xpf, xpb, ns = _layer(xpf, xpb, p, zst, nseq=B, T=T, L=Lp, ffn=dict(time_major=False), alpha=alpha)
        p_states.append(ns)
        m0 = jnp.pad(state_mlstm_m[l], ((0, 0), (GATE_F, LANES - GATE_F - M_HEADS)))[:, None, :]
        fc = state_ffn_conv[l].transpose(1, 0, 2).reshape((FFN_CONV - 1) * Bs, 2 * dff)
        padf = lambda a: jnp.pad(a, ((0, 0), (0, fp - dff)))
        sst = (state_mlstm_C[l], state_mlstm_n[l], m0,
               state_ssm[l].reshape(Bs, nh * S_HEADDIM, S_STATE), state_ssm_conv[l],
               _ffn_tiles(padf(fc[:, :dff]), padf(fc[:, dff:]), FFN_TILE)[None])
        xsf, xsb, ns = _layer(xsf, xsb, p, sst, nseq=Bs, T=Ts, L=Ls, ffn=dict(time_major=True), alpha=alpha)
        s_states.append(ns)

    def stack(states, nseq):
        C, n, m, S, sc, fc = (jnp.stack([s[j] for s in states], axis=0) for j in range(6))
        return C, n, m, S.reshape(depth, nseq, nh, S_HEADDIM, S_STATE), sc, fc

    pC, pn, pm, pS, psc, pfc = stack(p_states, B)
    sC, sn, sm, sS, ssc, sfc = stack(s_states, Bs)
    return (xpf.reshape(B, T, D), xsf.reshape(Bs, Ts, D), pC, pn, pm, pS, psc, pfc,
            sC, sn, sm, sS, ssc, sfc)
```

```python
import functools

import jax
import jax.numpy as jnp
from jax import lax
from jax.experimental import pallas as pl
from jax.experimental.pallas import tpu as pltpu

F32 = jnp.float32
BF16 = jnp.bfloat16
HIGHEST = lax.Precision.HIGHEST

M_HEADS = 4
S_HEADDIM = 64
S_GROUPS = 2
S_STATE = 128
S_CONV = 4
FFN_CONV = 3
LN_EPS = 1e-5
GN_EPS = 1e-6
NEG_BIG = -1e30

LANES = 128
SUBLANES = 8
MXU_DIM = 256
VMEM_LIMIT_BYTES = 52 * 1024 * 1024

GATE_I = 0
GATE_F = 4
GATE_DT = 8

PROMPT_CHUNK = 128
IN_PROJ_TILE = 2 * MXU_DIM
FFN_TILE = 2 * MXU_DIM
FFN_ROW_CHUNK = 32


def _largest_divisor(n, target, mult):
    best = None
    for d in range(mult, min(n, target) + 1, mult):
        if n % d == 0:
            best = d
    if best is None:
        raise ValueError(f"no tile for {n=} {target=} {mult=}")
    return best


def _params(sem):
    return pltpu.CompilerParams(dimension_semantics=sem, vmem_limit_bytes=VMEM_LIMIT_BYTES)


def _dot(a, b):
    return jnp.dot(a, b, preferred_element_type=F32)


def _dot_nt(a, b, precision=None):
    return lax.dot_general(a, b, (((1,), (1,)), ((), ())), precision=precision, preferred_element_type=F32)


def _dot_tn(a, b):
    return lax.dot_general(a, b, (((0,), (0,)), ((), ())), preferred_element_type=F32)


def _sigmoid(x):
    return 1.0 / (1.0 + jnp.exp(-x))


def _softplus(x):
    return jnp.maximum(x, 0.0) + jnp.log1p(jnp.exp(-jnp.abs(x)))


def _layer_norm(r, g, b):
    mu = jnp.mean(r, axis=-1, keepdims=True)
    d = r - mu
    var = jnp.mean(d * d, axis=-1, keepdims=True)
    return d * lax.rsqrt(var + LN_EPS) * g + b


def _in_proj_kernel(*refs, groups):
    x_ref = refs[0]
    ng = len(groups)
    w_refs = refs[1:1 + ng]
    o_refs = refs[1 + ng:1 + 2 * ng]
    j = pl.program_id(1)
    for (start, count), w_ref, o_ref in zip(groups, w_refs, o_refs):
        @pl.when(jnp.logical_and(j >= start, j < start + count))
        def _(w_ref=w_ref, o_ref=o_ref):
            o_ref[...] = _dot(x_ref[...], w_ref[...]).astype(o_ref.dtype)


def _in_proj(x, weights, out_dtypes, tm_target):
    m, k = x.shape
    tm = _largest_divisor(m, tm_target, 16)
    tiles, groups, start = [], [], 0
    for w in weights:
        n = w.shape[1]
        tn = _largest_divisor(n, IN_PROJ_TILE, LANES)
        tiles.append(tn)
        groups.append((start, n // tn))
        start += n // tn

    def clamp(s, c):
        return lambda i, j: (0, jnp.clip(j - s, 0, c - 1))

    def clamp_out(s, c):
        return lambda i, j: (i, jnp.clip(j - s, 0, c - 1))

    in_specs = [pl.BlockSpec((tm, k), lambda i, j: (i, 0))]
    in_specs += [pl.BlockSpec((k, tn), clamp(s, c)) for tn, (s, c) in zip(tiles, groups)]
    out_specs = [pl.BlockSpec((tm, tn), clamp_out(s, c)) for tn, (s, c) in zip(tiles, groups)]
    out_shape = [jax.ShapeDtypeStruct((m, w.shape[1]), dt) for w, dt in zip(weights, out_dtypes)]
    return pl.pallas_call(
        functools.partial(_in_proj_kernel, groups=tuple(groups)),
        grid=(m // tm, start),
        in_specs=in_specs,
        out_specs=out_specs,
        out_shape=out_shape,
        compiler_params=_params(("parallel", "arbitrary")),
        name="in_proj",
    )(x, *weights)


def _mlstm_kernel(*refs, L, dk, has_init, has_prev):
    q_ref, k_ref, v_ref, o_ref, g_ref, gb_ref, nw_ref = refs[:7]
    pos = 7
    if has_init:
        c0_ref, n0_ref, m0_ref = refs[pos:pos + 3]
        pos += 3
    if has_prev:
        pos += 3
    h_ref, c_out, n_out, m_out, c_s, n_s, m_s = refs[pos:pos + 7]
    c = pl.program_id(1)
    nc = pl.num_programs(1)

    @pl.when(c == 0)
    def _():
        if has_init:
            c_s[...] = c0_ref[0, 0]
            n_s[...] = n0_ref[0, 0]
            m_s[...] = m0_ref[0, 0]
        else:
            c_s[...] = jnp.zeros_like(c_s)
            n_s[...] = jnp.zeros_like(n_s)
            m_s[...] = jnp.zeros_like(m_s)

    g = g_ref[...] + gb_ref[...]
    lf = jnp.minimum(g, 0.0) - jnp.log1p(jnp.exp(-jnp.abs(g)))
    row = lax.broadcasted_iota(jnp.int32, (L, L), 0)
    col = lax.broadcasted_iota(jnp.int32, (L, L), 1)
    causal = col <= row
    tri = jnp.where(causal, 1.0, 0.0).astype(F32)
    bcum = jnp.dot(tri, lf, precision=HIGHEST, preferred_element_type=F32)
    lane = lax.broadcasted_iota(jnp.int32, (L, LANES), 1)
    packed = jnp.where(lane < GATE_F, g, bcum)
    sel = jnp.where(lax.broadcasted_iota(jnp.int32, (SUBLANES, LANES), 0)
                    == lax.broadcasted_iota(jnp.int32, (SUBLANES, LANES), 1), 1.0, 0.0).astype(F32)
    rows = _dot_nt(sel, packed, precision=HIGHEST)
    m_vec = m_s[...]
    m_lane = lax.broadcasted_iota(jnp.int32, (1, LANES), 1)
    m_new = m_vec
    k_scale = dk ** -0.5

    for h in range(M_HEADS):
        sl = slice(h * dk, (h + 1) * dk)
        bc = bcum[:, GATE_F + h:GATE_F + h + 1]
        igc = g[:, GATE_I + h:GATE_I + h + 1]
        igr = rows[GATE_I + h:GATE_I + h + 1, :]
        br = rows[GATE_F + h:GATE_F + h + 1, :]
        m_h = m_vec[:, GATE_F + h:GATE_F + h + 1]
        q = q_ref[:, sl]
        kf = k_ref[:, sl].astype(F32) * k_scale
        k = kf.astype(BF16)
        v = v_ref[:, sl]
        C = c_s[h]
        n = n_s[h:h + 1, :]

        log_w = jnp.where(causal, bc - br + igr, NEG_BIG)
        log_inter = bc + m_h
        m_t = jnp.maximum(log_inter, jnp.max(log_w, axis=1, keepdims=True))
        w_intra = jnp.exp(log_w - m_t)
        w_inter = jnp.exp(log_inter - m_t)
        s = _dot_nt(q, k) * w_intra
        num = _dot(s.astype(BF16), v) + w_inter * _dot(q, C.astype(BF16))
        qn = jnp.sum(q.astype(F32) * n, axis=1, keepdims=True)
        den = jnp.sum(s, axis=1, keepdims=True) + w_inter * qn
        hh = num / jnp.maximum(jnp.abs(den), jnp.exp(-m_t))

        m_end = m_t[L - 1:L, :]
        b_end = bc[L - 1:L, :]
        w_end = jnp.exp(b_end - bc + igc - m_end)
        decay = jnp.exp(b_end + m_h - m_end)
        kw = kf * w_end
        c_s[h] = decay * C + _dot_tn(kw.astype(BF16), v)
        n_s[h:h + 1, :] = decay * n + jnp.sum(kw, axis=0, keepdims=True)
        m_new = jnp.where(m_lane == GATE_F + h, m_end, m_new)

        mu = jnp.mean(hh, axis=1, keepdims=True)
        d = hh - mu
        var = jnp.mean(d * d, axis=1, keepdims=True)
        hn = d * lax.rsqrt(var + GN_EPS) * nw_ref[:, sl]
        h_ref[:, sl] = (hn * _sigmoid(o_ref[:, sl])).astype(h_ref.dtype)

    m_s[...] = m_new

    @pl.when(c == nc - 1)
    def _():
        c_out[0, 0] = c_s[...]
        n_out[0, 0] = n_s[...]
        m_out[0, 0] = m_s[...]


def _mlstm(qkv, o_pre, gates, gate_bias, norm_w, init, prev, *, layer, depth, nseq, T, L):
    W = qkv.shape[1] // 3
    dk = W // M_HEADS
    nc = T // L
    rb = lambda b, c: b * nc + c
    kern = functools.partial(_mlstm_kernel, L=L, dk=dk, has_init=init is not None, has_prev=prev is not None)
    st_specs = [pl.BlockSpec((1, 1, M_HEADS, dk, dk), lambda b, c: (layer, b, 0, 0, 0)),
                pl.BlockSpec((1, 1, M_HEADS, dk), lambda b, c: (layer, b, 0, 0)),
                pl.BlockSpec((1, 1, 1, LANES), lambda b, c: (layer, b, 0, 0))]
    in_specs = [pl.BlockSpec((L, W), lambda b, c: (rb(b, c), 0)),
                pl.BlockSpec((L, W), lambda b, c: (rb(b, c), 1)),
                pl.BlockSpec((L, W), lambda b, c: (rb(b, c), 2)),
                pl.BlockSpec((L, W), lambda b, c: (rb(b, c), 0)),
                pl.BlockSpec((L, LANES), lambda b, c: (rb(b, c), 0)),
                pl.BlockSpec((1, LANES), lambda b, c: (0, 0)),
                pl.BlockSpec((1, W), lambda b, c: (0, 0))]
    args = [qkv, qkv, qkv, o_pre, gates, gate_bias, norm_w]
    if init is not None:
        in_specs += st_specs
        args += list(init)
    aliases = {}
    if prev is not None:
        aliases = {len(args) + t: 1 + t for t in range(3)}
        in_specs += [pl.BlockSpec(memory_space=pl.ANY)] * 3
        args += list(prev)
    return pl.pallas_call(
        kern,
        grid=(nseq, nc),
        in_specs=in_specs,
        out_specs=[pl.BlockSpec((L, W), lambda b, c: (rb(b, c), 0))] + st_specs,
        out_shape=[jax.ShapeDtypeStruct((nseq * T, W), BF16),
                   jax.ShapeDtypeStruct((depth, nseq, M_HEADS, dk, dk), F32),
                   jax.ShapeDtypeStruct((depth, nseq, M_HEADS, dk), F32),
                   jax.ShapeDtypeStruct((depth, nseq, 1, LANES), F32)],
        scratch_shapes=[pltpu.VMEM((M_HEADS, dk, dk), F32),
                        pltpu.VMEM((M_HEADS, dk), F32),
                        pltpu.VMEM((1, LANES), F32)],
        input_output_aliases=aliases,
        compiler_params=_params(("parallel", "arbitrary")),
        name="mlstm",
    )(*args)


def _ssd_kernel(*refs, L, nheads, has_init, has_prev):
    z_ref, x_ref, g_ref, cw_ref, cb_ref, hp_ref, nw_ref = refs[:7]
    pos = 7
    if has_init:
        s0_ref, cv0_ref = refs[pos:pos + 2]
        pos += 2
    if has_prev:
        pos += 2
    y_ref, s_out, cv_out, s_s, xp_s, y_s = refs[pos:pos + 6]
    c = pl.program_id(1)
    nc = pl.num_programs(1)
    P = S_HEADDIM
    N = S_STATE
    hg = nheads // S_GROUPS
    W = nheads * P
    halo = SUBLANES
    kc = S_CONV

    @pl.when(c == 0)
    def _():
        xp_s[0:halo, :] = jnp.zeros((halo, xp_s.shape[1]), F32)
        if has_init:
            s_s[...] = s0_ref[0, 0]
            xp_s[halo - (kc - 1):halo, :] = cv0_ref[0, 0]
        else:
            s_s[...] = jnp.zeros_like(s_s)

    xp_s[halo:halo + L, :] = x_ref[...]
    conv = cb_ref[...] + xp_s[halo:halo + L, :] * cw_ref[kc - 1:kc, :]
    for j in range(kc - 1):
        off = halo - (kc - 1) + j
        conv = conv + xp_s[off:off + L, :] * cw_ref[j:j + 1, :]
    tail = xp_s[L:L + halo, :]
    xp_s[0:halo, :] = tail
    cv_out[0, 0] = tail
    xbc = conv * _sigmoid(conv)
    xs = xbc[:, 0:W]
    Bm = xbc[:, W:W + S_GROUPS * N]
    Cm = xbc[:, W + S_GROUPS * N:W + 2 * S_GROUPS * N]

    hp = hp_ref[...]
    dt = _softplus(g_ref[...] + hp[0:1, :])
    a = dt * (-jnp.exp(hp[1:2, :]))
    row = lax.broadcasted_iota(jnp.int32, (L, L), 0)
    col = lax.broadcasted_iota(jnp.int32, (L, L), 1)
    causal = col <= row
    tri = jnp.where(causal, 1.0, 0.0).astype(F32)
    bcum = jnp.dot(tri, a, precision=HIGHEST, preferred_element_type=F32)
    sel = jnp.where(lax.broadcasted_iota(jnp.int32, (nheads, LANES), 0) + GATE_DT
                    == lax.broadcasted_iota(jnp.int32, (nheads, LANES), 1), 1.0, 0.0).astype(F32)
    b_rows = _dot_nt(sel, bcum, precision=HIGHEST)
    dt_rows = _dot_nt(sel, dt, precision=HIGHEST)
    eb = jnp.exp(bcum)
    b_last = bcum[L - 1:L, :]
    w_end_all = jnp.exp(b_last - bcum) * dt
    dec_end = jnp.exp(b_last)

    for gi in range(S_GROUPS):
        Bg = Bm[:, gi * N:(gi + 1) * N].astype(BF16)
        Cg = Cm[:, gi * N:(gi + 1) * N].astype(BF16)
        cb = _dot_nt(Cg, Bg)
        s_g = s_s[gi * hg * P:(gi + 1) * hg * P, :]
        y_inter = _dot_nt(Cg, s_g.astype(BF16))
        xw_parts = []
        for hh in range(hg):
            idx = gi * hg + hh
            ln = GATE_DT + idx
            bc = bcum[:, ln:ln + 1]
            br = b_rows[idx:idx + 1, :]
            dtr = dt_rows[idx:idx + 1, :]
            decay = jnp.exp(jnp.where(causal, bc - br, NEG_BIG))
            mw = cb * decay * dtr
            xs_h = xs[:, idx * P:(idx + 1) * P]
            y_h = (_dot(mw.astype(BF16), xs_h.astype(BF16))
                   + eb[:, ln:ln + 1] * y_inter[:, hh * P:(hh + 1) * P]
                   + hp[2:3, ln:ln + 1] * xs_h)
            y_s[:, idx * P:(idx + 1) * P] = y_h
            xw_parts.append((xs_h * w_end_all[:, ln:ln + 1]).astype(BF16))
        xw = jnp.concatenate(xw_parts, axis=1)
        upd = _dot_tn(xw, Bg)
        for hh in range(hg):
            idx = gi * hg + hh
            ln = GATE_DT + idx
            r0 = idx * P
            s_s[r0:r0 + P, :] = dec_end[:, ln:ln + 1] * s_s[r0:r0 + P, :] + upd[hh * P:(hh + 1) * P, :]

    gw = W // S_GROUPS
    for gi in range(S_GROUPS):
        sl = slice(gi * gw, (gi + 1) * gw)
        z = z_ref[:, sl]
        gg = y_s[:, sl] * (z * _sigmoid(z))
        gg = gg * lax.rsqrt(jnp.mean(gg * gg, axis=1, keepdims=True) + GN_EPS)
        y_ref[:, sl] = (gg * nw_ref[:, sl]).astype(y_ref.dtype)

    @pl.when(c == nc - 1)
    def _():
        s_out[0, 0] = s_s[...]


def _ssd(z, xbc, gates, conv_w, conv_b, head_params, norm_w, init, prev, *, layer, depth, nseq, T, L):
    W = z.shape[1]
    nheads = W // S_HEADDIM
    CD = xbc.shape[1]
    nc = T // L
    rb = lambda b, c: b * nc + c
    kern = functools.partial(_ssd_kernel, L=L, nheads=nheads, has_init=init is not None, has_prev=prev is not None)
    s_spec = pl.BlockSpec((1, 1, nheads * S_HEADDIM, S_STATE), lambda b, c: (layer, b, 0, 0))
    in_specs = [pl.BlockSpec((L, W), lambda b, c: (rb(b, c), 0)),
                pl.BlockSpec((L, CD), lambda b, c: (rb(b, c), 0)),
                pl.BlockSpec((L, LANES), lambda b, c: (rb(b, c), 0)),
                pl.BlockSpec((S_CONV, CD), lambda b, c: (0, 0)),
                pl.BlockSpec((1, CD), lambda b, c: (0, 0)),
                pl.BlockSpec((SUBLANES, LANES), lambda b, c: (0, 0)),
                pl.BlockSpec((1, W), lambda b, c: (0, 0))]
    args = [z, xbc, gates, conv_w, conv_b, head_params, norm_w]
    if init is not None:
        in_specs += [s_spec, pl.BlockSpec((1, 1, S_CONV - 1, CD), lambda b, c: (layer, b, 0, 0))]
        args += list(init)
    aliases = {}
    if prev is not None:
        aliases = {len(args) + t: 1 + t for t in range(2)}
        in_specs += [pl.BlockSpec(memory_space=pl.ANY)] * 2
        args += list(prev)
    return pl.pallas_call(
        kern,
        grid=(nseq, nc),
        in_specs=in_specs,
        out_specs=[pl.BlockSpec((L, W), lambda b, c: (rb(b, c), 0)),
                   s_spec,
                   pl.BlockSpec((1, 1, SUBLANES, CD), lambda b, c: (layer, b, 0, 0))],
        out_shape=[jax.ShapeDtypeStruct((nseq * T, W), BF16),
                   jax.ShapeDtypeStruct((depth, nseq, nheads * S_HEADDIM, S_STATE), F32),
                   jax.ShapeDtypeStruct((depth, nseq, SUBLANES, CD), F32)],
        scratch_shapes=[pltpu.VMEM((nheads * S_HEADDIM, S_STATE), F32),
                        pltpu.VMEM((SUBLANES + L, CD), F32),
                        pltpu.VMEM((L, W), F32)],
        input_output_aliases=aliases,
        compiler_params=_params(("parallel", "arbitrary")),
        name="ssd",
    )(*args)


def _outproj_kernel(hm_ref, hs_ref, wt_ref, wb_ref, x_ref, g_ref, b_ref, y_ref, yb_ref, *, alpha):
    mix = _dot(hm_ref[...], wt_ref[0]) + _dot(hs_ref[...], wb_ref[0])
    y = _layer_norm(alpha * x_ref[...] + mix, g_ref[...], b_ref[...])
    y_ref[...] = y
    yb_ref[...] = y.astype(BF16)


def _outproj(hm, hs, w_out, x, ln_g, ln_b, *, alpha):
    m, d = x.shape
    w = hm.shape[1]
    tm = _largest_divisor(m, 512, 16)
    kern = functools.partial(_outproj_kernel, alpha=alpha)
    return pl.pallas_call(
        kern,
        grid=(m // tm,),
        in_specs=[pl.BlockSpec((tm, w), lambda i: (i, 0)),
                  pl.BlockSpec((tm, w), lambda i: (i, 0)),
                  pl.BlockSpec((1, w, d), lambda i: (0, 0, 0)),
                  pl.BlockSpec((1, w, d), lambda i: (1, 0, 0)),
                  pl.BlockSpec((tm, d), lambda i: (i, 0)),
                  pl.BlockSpec((1, d), lambda i: (0, 0)),
                  pl.BlockSpec((1, d), lambda i: (0, 0))],
        out_specs=[pl.BlockSpec((tm, d), lambda i: (i, 0)),
                   pl.BlockSpec((tm, d), lambda i: (i, 0))],
        out_shape=[jax.ShapeDtypeStruct((m, d), F32),
                   jax.ShapeDtypeStruct((m, d), BF16)],
        compiler_params=_params(("parallel",)),
        name="out_proj_ln",
    )(hm, hs, w_out, w_out, x, ln_g, ln_b)


def _ffn_kernel(*refs, sh, hr, tps, rc, alpha, has_init):
    xb_ref, xf_ref, wu_ref, cw_ref, cb_ref, wd_ref = refs[:6]
    pos = 6
    if has_init:
        h0_ref = refs[pos]
        pos += 1
    lg_ref, lb_ref, y_ref, yb_ref, oh_ref, acc, ext, act, hal = refs[pos:pos + 9]
    i = pl.program_id(0)
    j = pl.program_id(1)
    nj = pl.num_programs(1)
    tm = xb_ref.shape[0]
    tf = wd_ref.shape[1]
    first = (i % tps) == 0

    @pl.when(first)
    def _():
        if has_init:
            ext[0:hr, :] = h0_ref[0, 0]
        else:
            ext[0:hr, :] = jnp.zeros((hr, ext.shape[1]), F32)

    @pl.when(jnp.logical_not(first))
    def _():
        ext[0:hr, :] = hal[j]

    ext[hr:hr + tm, :] = _dot(xb_ref[...], wu_ref[0])
    last = ext[tm:tm + hr, :]
    hal[j] = last
    oh_ref[0, 0] = last

    for r0 in range(0, tm, rc):
        c = (ext[hr - 2 * sh + r0:hr - 2 * sh + r0 + rc, :] * cw_ref[0, 0:1, :]
             + ext[hr - sh + r0:hr - sh + r0 + rc, :] * cw_ref[0, 1:2, :]
             + ext[hr + r0:hr + r0 + rc, :] * cw_ref[0, 2:3, :] + cb_ref[0])
        cg = c[:, :tf]
        act[r0:r0 + rc, :] = (cg * _sigmoid(cg) * c[:, tf:]).astype(BF16)

    @pl.when(j == 0)
    def _():
        acc[...] = jnp.zeros_like(acc)

    acc[...] += _dot(act[...], wd_ref[0])

    @pl.when(j == nj - 1)
    def _():
        y = _layer_norm(alpha * xf_ref[...] + acc[...], lg_ref[...], lb_ref[...])
        y_ref[...] = y
        yb_ref[...] = y.astype(BF16)


def _ffn(xb, xf, w_u, cw, cb, w_d, h0, ln_g, ln_b, *, nseq, tm, sh, hr, alpha):
    m, d = xf.shape
    nj, tf = w_d.shape[0], w_d.shape[1]
    ni = m // tm
    tps = ni // nseq
    kern = functools.partial(_ffn_kernel, sh=sh, hr=hr, tps=tps, rc=FFN_ROW_CHUNK, alpha=alpha,
                             has_init=h0 is not None)
    in_specs = [pl.BlockSpec((tm, d), lambda i, j: (i, 0)),
                pl.BlockSpec((tm, d), lambda i, j: (i, 0)),
                pl.BlockSpec((1, d, 2 * tf), lambda i, j: (j, 0, 0)),
                pl.BlockSpec((1, FFN_CONV, 2 * tf), lambda i, j: (j, 0, 0)),
                pl.BlockSpec((1, 1, 2 * tf), lambda i, j: (j, 0, 0)),
                pl.BlockSpec((1, tf, d), lambda i, j: (j, 0, 0))]
    args = [xb, xf, w_u, cw, cb, w_d]
    if h0 is not None:
        in_specs.append(pl.BlockSpec((1, 1, hr, 2 * tf), lambda i, j: (i // tps, j, 0, 0)))
        args.append(h0)
    in_specs += [pl.BlockSpec((1, d), lambda i, j: (0, 0))] * 2
    args += [ln_g, ln_b]
    return pl.pallas_call(
        kern,
        grid=(ni, nj),
        in_specs=in_specs,
        out_specs=[pl.BlockSpec((tm, d), lambda i, j: (i, 0)),
                   pl.BlockSpec((tm, d), lambda i, j: (i, 0)),
                   pl.BlockSpec((1, 1, hr, 2 * tf),
                                lambda i, j: (i // tps, jnp.where(i % tps == tps - 1, j, 0), 0, 0))],
        out_shape=[jax.ShapeDtypeStruct((m, d), F32),
                   jax.ShapeDtypeStruct((m, d), BF16),
                   jax.ShapeDtypeStruct((nseq, nj, hr, 2 * tf), F32)],
        scratch_shapes=[pltpu.VMEM((tm, d), F32),
                        pltpu.VMEM((hr + tm, 2 * tf), F32),
                        pltpu.VMEM((tm, tf), BF16),
                        pltpu.VMEM((nj, hr, 2 * tf), F32)],
        compiler_params=_params(("arbitrary", "arbitrary")),
        name="conv_ffn_ln",
    )(*args)


def _ffn_tiles(g, v, tf):
    nj = g.shape[-1] // tf
    lead = g.shape[:-1]
    t = jnp.stack([g.reshape(lead + (nj, tf)), v.reshape(lead + (nj, tf))], axis=-2)
    return jnp.moveaxis(t.reshape(lead + (nj, 2 * tf)), -2, 0)


def _ffn_untile(t, tf):
    nj = t.shape[-3]
    t = jnp.moveaxis(t, -3, -2)
    lead = t.shape[:-2]
    return (t[..., :tf].reshape(lead + (nj * tf,)), t[..., tf:].reshape(lead + (nj * tf,)))


def _tile_up_kernel(w_ref, o_ref, *, dff, tf):
    nj = o_ref.shape[0]
    for j in range(nj):
        lo = j * tf
        width = min(tf, dff - lo)
        for half, base in ((0, 0), (1, dff)):
            o_ref[j, :, half * tf:half * tf + width] = w_ref[0, :, base + lo:base + lo + width].astype(BF16)
            if width < tf:
                o_ref[j, :, half * tf + width:(half + 1) * tf] = jnp.zeros((o_ref.shape[1], tf - width), BF16)


def _tile_up(w_up, layer, tf):
    _, d, two_dff = w_up.shape
    dff = two_dff // 2
    assert dff % LANES == 0
    nj = -(-dff // tf)
    tr = _largest_divisor(d, 128, 16)
    return pl.pallas_call(
        functools.partial(_tile_up_kernel, dff=dff, tf=tf),
        grid=(d // tr,),
        in_specs=[pl.BlockSpec((1, tr, two_dff), lambda r: (layer, r, 0))],
        out_specs=pl.BlockSpec((nj, tr, 2 * tf), lambda r: (0, r, 0)),
        out_shape=jax.ShapeDtypeStruct((nj, d, 2 * tf), BF16),
        compiler_params=_params(("parallel",)),
        name="tile_w_up",
    )(w_up)


def _tile_down_kernel(w_ref, o_ref, *, dff):
    j = pl.program_id(0)
    tf = o_ref.shape[1]
    rows = lax.broadcasted_iota(jnp.int32, o_ref.shape[1:], 0) + j * tf
    o_ref[0] = jnp.where(rows < dff, w_ref[0], 0.0).astype(BF16)


def _tile_down(w_down, layer, tf):
    _, dff, d = w_down.shape
    nj = -(-dff // tf)
    return pl.pallas_call(
        functools.partial(_tile_down_kernel, dff=dff),
        grid=(nj,),
        in_specs=[pl.BlockSpec((1, tf, d), lambda j: (layer, j, 0))],
        out_specs=pl.BlockSpec((1, tf, d), lambda j: (j, 0, 0)),
        out_shape=jax.ShapeDtypeStruct((nj, tf, d), BF16),
        compiler_params=_params(("parallel",)),
        name="tile_w_down",
    )(w_down)


def _prep_layer(l, w_in, b_i, b_f, m_norm_w, s_conv_w, s_conv_b, dt_bias, A_log, D_skip, s_norm_w, w_out,
                ln1_g, ln1_b, w_up, f_conv_w, f_conv_b, w_down, ln2_g, ln2_b):
    mw = m_norm_w.shape[1]
    sw = s_norm_w.shape[1]
    cd = s_conv_w.shape[2]
    nh = dt_bias.shape[1]
    dff = w_down.shape[1]
    tf = FFN_TILE
    fp = -(-dff // tf) * tf
    o = 0
    cols = {}
    for name, size in (("qkv", 3 * mw), ("o", mw), ("i", M_HEADS), ("f", M_HEADS),
                       ("z", sw), ("xbc", cd), ("dt", nh)):
        cols[name] = (o, o + size)
        o += size
    wl = w_in[l]
    cut = lambda name: wl[:, cols[name][0]:cols[name][1]]
    p = {}
    wg = jnp.concatenate([cut("i"), cut("f"), cut("dt")], axis=1)
    p["w_in"] = [cut("qkv").astype(BF16), cut("o").astype(BF16), cut("z").astype(BF16), cut("xbc").astype(BF16),
                 jnp.pad(wg, ((0, 0), (0, LANES - wg.shape[1]))).astype(BF16)]
    gb = jnp.concatenate([b_i[l], b_f[l]])
    p["gate_bias"] = jnp.pad(gb, (0, LANES - gb.shape[0]))[None, :]
    p["m_norm_w"] = m_norm_w[l][None, :]
    p["s_conv_w"] = s_conv_w[l]
    p["s_conv_b"] = s_conv_b[l][None, :]
    hp = jnp.stack([dt_bias[l], A_log[l], D_skip[l]])
    p["head_params"] = jnp.pad(hp, ((0, SUBLANES - 3), (GATE_DT, LANES - GATE_DT - nh)))
    p["s_norm_w"] = s_norm_w[l][None, :]
    p["w_out"] = w_out[l].astype(BF16).reshape(2, mw, w_out.shape[2])
    p["ln1_g"] = ln1_g[l][None, :]
    p["ln1_b"] = ln1_b[l][None, :]
    padc = lambda a: jnp.pad(a, ((0, 0), (0, fp - dff)))
    p["w_up"] = _tile_up(w_up, l, tf)
    p["f_conv_w"] = _ffn_tiles(padc(f_conv_w[l][:, :dff]), padc(f_conv_w[l][:, dff:]), tf)
    p["f_conv_b"] = _ffn_tiles(padc(f_conv_b[l][None, :dff]), padc(f_conv_b[l][None, dff:]), tf)
    p["w_down"] = _tile_down(w_down, l, tf)
    p["ln2_g"] = ln2_g[l][None, :]
    p["ln2_b"] = ln2_b[l][None, :]
    p["dff"] = dff
    p["fp"] = fp
    return p


def _layer(xf, xb, p, init, prev, *, layer, depth, nseq, T, L, time_major_ffn, alpha):
    qkv, o_pre, z, xbc, gates = _in_proj(xb, p["w_in"], [BF16, F32, F32, F32, F32], tm_target=1024)
    hm, C, n, m = _mlstm(qkv, o_pre, gates, p["gate_bias"], p["m_norm_w"],
                         None if init is None else init["mlstm"], None if prev is None else prev["mlstm"],
                         layer=layer, depth=depth, nseq=nseq, T=T, L=L)
    hs, S, sconv = _ssd(z, xbc, gates, p["s_conv_w"], p["s_conv_b"], p["head_params"], p["s_norm_w"],
                        None if init is None else init["ssd"], None if prev is None else prev["ssd"],
                        layer=layer, depth=depth, nseq=nseq, T=T, L=L)
    x1f, x1b = _outproj(hm, hs, p["w_out"], xf, p["ln1_g"], p["ln1_b"], alpha=alpha)
    d = xf.shape[1]
    dff = p["dff"]
    ffn_w = (p["w_up"], p["f_conv_w"], p["f_conv_b"], p["w_down"])
    h0 = None if init is None else init["ffn"]
    if time_major_ffn:
        tr = lambda a: a.reshape(nseq, T, d).transpose(1, 0, 2).reshape(nseq * T, d)
        x1f_t, x1b_t = tr(x1f), tr(x1b)
        tm = _largest_divisor(nseq * T, 256, (FFN_CONV - 1) * nseq)
        y_f, y_b, oh = _ffn(x1b_t, x1f_t, *ffn_w, h0, p["ln2_g"], p["ln2_b"],
                            nseq=1, tm=tm, sh=nseq, hr=(FFN_CONV - 1) * nseq, alpha=alpha)
        tb = lambda a: a.reshape(T, nseq, d).transpose(1, 0, 2).reshape(nseq * T, d)
        y_f, y_b = tb(y_f), tb(y_b)
        og, ov = _ffn_untile(oh, FFN_TILE)
        fconv = jnp.concatenate([og[0, :, :dff], ov[0, :, :dff]], axis=1)
        fconv = fconv.reshape(FFN_CONV - 1, nseq, 2 * dff).transpose(1, 0, 2)
    else:
        tm = _largest_divisor(T, 512, FFN_ROW_CHUNK)
        y_f, y_b, oh = _ffn(x1b, x1f, *ffn_w, h0, p["ln2_g"], p["ln2_b"],
                            nseq=nseq, tm=tm, sh=1, hr=SUBLANES, alpha=alpha)
        og, ov = _ffn_untile(oh, FFN_TILE)
        k = FFN_CONV - 1
        fconv = jnp.concatenate([og[:, SUBLANES - k:, :dff], ov[:, SUBLANES - k:, :dff]], axis=2)
    return y_f, y_b, dict(mlstm=(C, n, m), ssd=(S, sconv)), fconv


def kernel(x_prompt, x_sample, state_mlstm_C, state_mlstm_n, state_mlstm_m, state_ssm, state_ssm_conv,
           state_ffn_conv, w_in, mlstm_b_i, mlstm_b_f, mlstm_norm_w, ssm_conv_w, ssm_conv_b, ssm_dt_bias,
           ssm_A_log, ssm_D, ssm_norm_w, w_out, ln1_g, ln1_b, ffn_w_up, ffn_conv_w, ffn_conv_b,
           ffn_w_down, ln2_g, ln2_b):
    weights = (w_in, mlstm_b_i, mlstm_b_f, mlstm_norm_w, ssm_conv_w, ssm_conv_b, ssm_dt_bias, ssm_A_log,
               ssm_D, ssm_norm_w, w_out, ln1_g, ln1_b, ffn_w_up, ffn_conv_w, ffn_conv_b, ffn_w_down,
               ln2_g, ln2_b)
    depth = w_in.shape[0]
    alpha = float((2 * depth) ** 0.25)
    B, T, D = x_prompt.shape
    Bs, Ts, _ = x_sample.shape
    nh = ssm_dt_bias.shape[1]
    Lp = PROMPT_CHUNK if T % PROMPT_CHUNK == 0 else T
    Ls = PROMPT_CHUNK if Ts % PROMPT_CHUNK == 0 else Ts

    xpf = x_prompt.reshape(B * T, D)
    xpb = xpf.astype(BF16)
    xsf = x_sample.reshape(Bs * Ts, D)
    xsb = xsf.astype(BF16)
    m_in = jnp.pad(state_mlstm_m, ((0, 0), (0, 0), (GATE_F, LANES - GATE_F - M_HEADS)))[:, :, None, :]
    s_in = state_ssm.reshape(depth, Bs, nh * S_HEADDIM, S_STATE)
    p_prev, s_prev, p_fconv, s_fconv = None, None, [], []
    for l in range(depth):
        p = _prep_layer(l, *weights)
        fp, dff = p["fp"], p["dff"]
        xpf, xpb, p_prev, fc = _layer(xpf, xpb, p, None, p_prev, layer=l, depth=depth, nseq=B, T=T, L=Lp,
                                      time_major_ffn=False, alpha=alpha)
        p_fconv.append(fc)
        fc0 = state_ffn_conv[l].transpose(1, 0, 2).reshape((FFN_CONV - 1) * Bs, 2 * dff)
        padf = lambda a: jnp.pad(a, ((0, 0), (0, fp - dff)))
        init = dict(mlstm=(state_mlstm_C, state_mlstm_n, m_in), ssd=(s_in, state_ssm_conv),
                    ffn=_ffn_tiles(padf(fc0[:, :dff]), padf(fc0[:, dff:]), FFN_TILE)[None])
        xsf, xsb, s_prev, fc = _layer(xsf, xsb, p, init, s_prev, layer=l, depth=depth, nseq=Bs, T=Ts, L=Ls,
                                      time_major_ffn=True, alpha=alpha)
        s_fconv.append(fc)

    def unpack(st, nseq):
        C, n, m = st["mlstm"]
        S, sconv = st["ssd"]
        return (C, n, m[:, :, 0, GATE_F:GATE_F + M_HEADS], S.reshape(depth, nseq, nh, S_HEADDIM, S_STATE),
                sconv[:, :, SUBLANES - (S_CONV - 1):, :])

    pC, pn, pm, pS, psc = unpack(p_prev, B)
    sC, sn, sm, sS, ssc = unpack(s_prev, Bs)
    return (xpf.reshape(B, T, D), xsf.reshape(Bs, Ts, D), pC, pn, pm, pS, psc, jnp.stack(p_fconv),
            sC, sn, sm, sS, ssc, jnp.stack(s_fconv))
```

```python
import functools

import jax
import jax.numpy as jnp
from jax import lax
from jax.experimental import pallas as pl
from jax.experimental.pallas import tpu as pltpu

F32 = jnp.float32
BF16 = jnp.bfloat16
HIGHEST = lax.Precision.HIGHEST

M_HEADS = 4
S_HEADDIM = 64
S_GROUPS = 2
S_STATE = 128
S_CONV = 4
FFN_CONV = 3
LN_EPS = 1e-5
GN_EPS = 1e-6
NEG_BIG = -1e30

LANES = 128
SUBLANES = 8
MXU_DIM = 256
VMEM_LIMIT_BYTES = 52 * 1024 * 1024

GATE_I = 0
GATE_F = 4
GATE_DT = 8

PROMPT_CHUNK = 128
SEQS_PER_STEP = 4
IN_PROJ_TILE = 2 * MXU_DIM
FFN_TILE = 2 * MXU_DIM
FFN_ROW_CHUNK = 32


def _largest_divisor(n, target, mult):
    best = None
    for d in range(mult, min(n, target) + 1, mult):
        if n % d == 0:
            best = d
    if best is None:
        raise ValueError(f"no tile for {n=} {target=} {mult=}")
    return best


def _params(sem, flags=None):
    return pltpu.CompilerParams(dimension_semantics=sem, vmem_limit_bytes=VMEM_LIMIT_BYTES, flags=flags)


def _dot(a, b):
    return jnp.dot(a, b, preferred_element_type=F32)


def _dot_nt(a, b, precision=None):
    return lax.dot_general(a, b, (((1,), (1,)), ((), ())), precision=precision, preferred_element_type=F32)


def _dot_tn(a, b):
    return lax.dot_general(a, b, (((0,), (0,)), ((), ())), preferred_element_type=F32)


def _sigmoid(x):
    return 1.0 / (1.0 + jnp.exp(-x))


def _softplus(x):
    return jnp.maximum(x, 0.0) + jnp.log1p(jnp.exp(-jnp.abs(x)))


def _layer_norm(r, g, b):
    mu = jnp.mean(r, axis=-1, keepdims=True)
    d = r - mu
    var = jnp.mean(d * d, axis=-1, keepdims=True)
    return d * lax.rsqrt(var + LN_EPS) * g + b


def _in_proj_kernel(*refs, groups):
    x_ref = refs[0]
    ng = len(groups)
    w_refs = refs[1:1 + ng]
    o_refs = refs[1 + ng:1 + 2 * ng]
    j = pl.program_id(1)
    for (start, count), w_ref, o_ref in zip(groups, w_refs, o_refs):
        @pl.when(jnp.logical_and(j >= start, j < start + count))
        def _(w_ref=w_ref, o_ref=o_ref):
            o_ref[...] = _dot(x_ref[...], w_ref[...]).astype(o_ref.dtype)


def _in_proj(x, weights, out_dtypes, tm_target):
    m, k = x.shape
    tm = _largest_divisor(m, tm_target, 16)
    tiles, groups, start = [], [], 0
    for w in weights:
        n = w.shape[1]
        tn = _largest_divisor(n, IN_PROJ_TILE, LANES)
        tiles.append(tn)
        groups.append((start, n // tn))
        start += n // tn

    def clamp(s, c):
        return lambda i, j: (0, jnp.clip(j - s, 0, c - 1))

    def clamp_out(s, c):
        return lambda i, j: (i, jnp.clip(j - s, 0, c - 1))

    in_specs = [pl.BlockSpec((tm, k), lambda i, j: (i, 0))]
    in_specs += [pl.BlockSpec((k, tn), clamp(s, c)) for tn, (s, c) in zip(tiles, groups)]
    out_specs = [pl.BlockSpec((tm, tn), clamp_out(s, c)) for tn, (s, c) in zip(tiles, groups)]
    out_shape = [jax.ShapeDtypeStruct((m, w.shape[1]), dt) for w, dt in zip(weights, out_dtypes)]
    return pl.pallas_call(
        functools.partial(_in_proj_kernel, groups=tuple(groups)),
        grid=(m // tm, start),
        in_specs=in_specs,
        out_specs=out_specs,
        out_shape=out_shape,
        compiler_params=_params(("parallel", "arbitrary")),
        name="in_proj",
    )(x, *weights)


def _mlstm_kernel(*refs, L, dk, has_init, has_prev):
    q_ref, k_ref, v_ref, o_ref, g_ref, gb_ref, nw_ref = refs[:7]
    pos = 7
    if has_init:
        c0_ref, n0_ref, m0_ref = refs[pos:pos + 3]
        pos += 3
    if has_prev:
        pos += 3
    h_ref, c_out, n_out, m_out, c_s, n_s, m_s = refs[pos:pos + 7]
    c = pl.program_id(1)
    nc = pl.num_programs(1)

    @pl.when(c == 0)
    def _():
        if has_init:
            c_s[...] = c0_ref[0]
            n_s[...] = n0_ref[0]
            m_s[...] = m0_ref[0]
        else:
            c_s[...] = jnp.zeros_like(c_s)
            n_s[...] = jnp.zeros_like(n_s)
            m_s[...] = jnp.zeros_like(m_s)

    row = lax.broadcasted_iota(jnp.int32, (L, L), 0)
    col = lax.broadcasted_iota(jnp.int32, (L, L), 1)
    causal = col <= row
    tri = jnp.where(causal, 1.0, 0.0).astype(F32)
    lane = lax.broadcasted_iota(jnp.int32, (L, LANES), 1)
    sel = jnp.where(lax.broadcasted_iota(jnp.int32, (SUBLANES, LANES), 0)
                    == lax.broadcasted_iota(jnp.int32, (SUBLANES, LANES), 1), 1.0, 0.0).astype(F32)
    m_lane = lax.broadcasted_iota(jnp.int32, (1, LANES), 1)
    k_scale = dk ** -0.5

    for s_i, h in [(s_i, h) for s_i in range(q_ref.shape[0]) for h in range(M_HEADS)]:
        if h == 0:
            g = g_ref[s_i] + gb_ref[...]
            lf = jnp.minimum(g, 0.0) - jnp.log1p(jnp.exp(-jnp.abs(g)))
            bcum = jnp.dot(tri, lf, precision=HIGHEST, preferred_element_type=F32)
            packed = jnp.where(lane < GATE_F, g, bcum)
            rows = _dot_nt(sel, packed, precision=HIGHEST)
            m_vec = m_s[s_i]
            m_new = m_vec
        sl = slice(h * dk, (h + 1) * dk)
        bc = bcum[:, GATE_F + h:GATE_F + h + 1]
        igc = g[:, GATE_I + h:GATE_I + h + 1]
        igr = rows[GATE_I + h:GATE_I + h + 1, :]
        br = rows[GATE_F + h:GATE_F + h + 1, :]
        m_h = m_vec[:, GATE_F + h:GATE_F + h + 1]
        q = q_ref[s_i, :, sl]
        kf = k_ref[s_i, :, sl].astype(F32) * k_scale
        k = kf.astype(BF16)
        v = v_ref[s_i, :, sl]
        C = c_s[s_i, h]
        n = n_s[s_i, h:h + 1, :]

        log_w = jnp.where(causal, bc - br + igr, NEG_BIG)
        log_inter = bc + m_h
        m_t = jnp.maximum(log_inter, jnp.max(log_w, axis=1, keepdims=True))
        w_intra = jnp.exp(log_w - m_t)
        w_inter = jnp.exp(log_inter - m_t)
        s = _dot_nt(q, k) * w_intra
        num = _dot(s.astype(BF16), v) + w_inter * _dot(q, C.astype(BF16))
        qn = jnp.sum(q.astype(F32) * n, axis=1, keepdims=True)
        den = jnp.sum(s, axis=1, keepdims=True) + w_inter * qn
        hh = num / jnp.maximum(jnp.abs(den), jnp.exp(-m_t))

        m_end = m_t[L - 1:L, :]
        b_end = bc[L - 1:L, :]
        w_end = jnp.exp(b_end - bc + igc - m_end)
        decay = jnp.exp(b_end + m_h - m_end)
        kw = kf * w_end
        c_s[s_i, h] = decay * C + _dot_tn(kw.astype(BF16), v)
        n_s[s_i, h:h + 1, :] = decay * n + jnp.sum(kw, axis=0, keepdims=True)
        m_new = jnp.where(m_lane == GATE_F + h, m_end, m_new)

        mu = jnp.mean(hh, axis=1, keepdims=True)
        d = hh - mu
        var = jnp.mean(d * d, axis=1, keepdims=True)
        hn = d * lax.rsqrt(var + GN_EPS) * nw_ref[:, sl]
        h_ref[s_i, :, sl] = (hn * _sigmoid(o_ref[s_i, :, sl])).astype(h_ref.dtype)
        if h == M_HEADS - 1:
            m_s[s_i] = m_new

    @pl.when(c == nc - 1)
    def _():
        c_out[0] = c_s[...]
        n_out[0] = n_s[...]
        m_out[0] = m_s[...]


def _mlstm(qkv, o_pre, gates, gate_bias, norm_w, init, prev, *, layer, depth, nseq, T, L):
    W = qkv.shape[1] // 3
    dk = W // M_HEADS
    nc = T // L
    G = _largest_divisor(nseq, SEQS_PER_STEP, 1)
    qkv, o_pre, gates = (a.reshape(nseq, T, a.shape[1]) for a in (qkv, o_pre, gates))
    kern = functools.partial(_mlstm_kernel, L=L, dk=dk, has_init=init is not None, has_prev=prev is not None)
    st_specs = [pl.BlockSpec((1, G, M_HEADS, dk, dk), lambda b, c: (layer, b, 0, 0, 0)),
                pl.BlockSpec((1, G, M_HEADS, dk), lambda b, c: (layer, b, 0, 0)),
                pl.BlockSpec((1, G, 1, LANES), lambda b, c: (layer, b, 0, 0))]
    in_specs = [pl.BlockSpec((G, L, W), lambda b, c: (b, c, 0)),
                pl.BlockSpec((G, L, W), lambda b, c: (b, c, 1)),
                pl.BlockSpec((G, L, W), lambda b, c: (b, c, 2)),
                pl.BlockSpec((G, L, W), lambda b, c: (b, c, 0)),
                pl.BlockSpec((G, L, LANES), lambda b, c: (b, c, 0)),
                pl.BlockSpec((1, LANES), lambda b, c: (0, 0)),
                pl.BlockSpec((1, W), lambda b, c: (0, 0))]
    args = [qkv, qkv, qkv, o_pre, gates, gate_bias, norm_w]
    if init is not None:
        in_specs += st_specs
        args += list(init)
    aliases = {}
    if prev is not None:
        aliases = {len(args) + t: 1 + t for t in range(3)}
        in_specs += [pl.BlockSpec(memory_space=pl.ANY)] * 3
        args += list(prev)
    h, C, n, m = pl.pallas_call(
        kern,
        grid=(nseq // G, nc),
        in_specs=in_specs,
        out_specs=[pl.BlockSpec((G, L, W), lambda b, c: (b, c, 0))] + st_specs,
        out_shape=[jax.ShapeDtypeStruct((nseq, T, W), BF16),
                   jax.ShapeDtypeStruct((depth, nseq, M_HEADS, dk, dk), F32),
                   jax.ShapeDtypeStruct((depth, nseq, M_HEADS, dk), F32),
                   jax.ShapeDtypeStruct((depth, nseq, 1, LANES), F32)],
        scratch_shapes=[pltpu.VMEM((G, M_HEADS, dk, dk), F32),
                        pltpu.VMEM((G, M_HEADS, dk), F32),
                        pltpu.VMEM((G, 1, LANES), F32)],
        input_output_aliases=aliases,
        compiler_params=_params(("parallel", "arbitrary")),
        name="mlstm",
    )(*args)
    return h.reshape(nseq * T, W), C, n, m


def _ssd_kernel(*refs, L, nheads, has_init, has_prev):
    z_ref, x_ref, g_ref, cw_ref, cb_ref, hp_ref, nw_ref = refs[:7]
    pos = 7
    if has_init:
        s0_ref, cv0_ref = refs[pos:pos + 2]
        pos += 2
    if has_prev:
        pos += 2
    y_ref, s_out, cv_out, s_s, xp_s, y_s = refs[pos:pos + 6]
    c = pl.program_id(1)
    nc = pl.num_programs(1)
    P = S_HEADDIM
    N = S_STATE
    hg = nheads // S_GROUPS
    W = nheads * P
    halo = SUBLANES
    kc = S_CONV

    G = z_ref.shape[0]

    @pl.when(c == 0)
    def _():
        xp_s[:, 0:halo, :] = jnp.zeros((G, halo, xp_s.shape[2]), F32)
        if has_init:
            s_s[...] = s0_ref[0]
            xp_s[:, halo - (kc - 1):halo, :] = cv0_ref[0]
        else:
            s_s[...] = jnp.zeros_like(s_s)

    hp = hp_ref[...]
    row = lax.broadcasted_iota(jnp.int32, (L, L), 0)
    col = lax.broadcasted_iota(jnp.int32, (L, L), 1)
    causal = col <= row
    tri = jnp.where(causal, 1.0, 0.0).astype(F32)
    sel = jnp.where(lax.broadcasted_iota(jnp.int32, (nheads, LANES), 0) + GATE_DT
                    == lax.broadcasted_iota(jnp.int32, (nheads, LANES), 1), 1.0, 0.0).astype(F32)
    gw = W // S_GROUPS

    for s_i in range(G):
        xp_s[s_i, halo:halo + L, :] = x_ref[s_i]
        conv = cb_ref[...] + xp_s[s_i, halo:halo + L, :] * cw_ref[kc - 1:kc, :]
        for j in range(kc - 1):
            off = halo - (kc - 1) + j
            conv = conv + xp_s[s_i, off:off + L, :] * cw_ref[j:j + 1, :]
        tail = xp_s[s_i, L:L + halo, :]
        xp_s[s_i, 0:halo, :] = tail
        cv_out[0, s_i] = tail
        xbc = conv * _sigmoid(conv)
        xs = xbc[:, 0:W]
        Bm = xbc[:, W:W + S_GROUPS * N]
        Cm = xbc[:, W + S_GROUPS * N:W + 2 * S_GROUPS * N]

        dt = _softplus(g_ref[s_i] + hp[0:1, :])
        a = dt * (-jnp.exp(hp[1:2, :]))
        bcum = jnp.dot(tri, a, precision=HIGHEST, preferred_element_type=F32)
        b_rows = _dot_nt(sel, bcum, precision=HIGHEST)
        dt_rows = _dot_nt(sel, dt, precision=HIGHEST)
        eb = jnp.exp(bcum)
        b_last = bcum[L - 1:L, :]
        w_end_all = jnp.exp(b_last - bcum) * dt
        dec_end = jnp.exp(b_last)

        for gi in range(S_GROUPS):
            Bg = Bm[:, gi * N:(gi + 1) * N].astype(BF16)
            Cg = Cm[:, gi * N:(gi + 1) * N].astype(BF16)
            cb = _dot_nt(Cg, Bg)
            s_g = s_s[s_i, gi * hg * P:(gi + 1) * hg * P, :]
            y_inter = _dot_nt(Cg, s_g.astype(BF16))
            xw_parts = []
            for hh in range(hg):
                idx = gi * hg + hh
                ln = GATE_DT + idx
                bc = bcum[:, ln:ln + 1]
                br = b_rows[idx:idx + 1, :]
                dtr = dt_rows[idx:idx + 1, :]
                decay = jnp.exp(jnp.where(causal, bc - br, NEG_BIG))
                mw = cb * decay * dtr
                xs_h = xs[:, idx * P:(idx + 1) * P]
                y_h = (_dot(mw.astype(BF16), xs_h.astype(BF16))
                       + eb[:, ln:ln + 1] * y_inter[:, hh * P:(hh + 1) * P]
                       + hp[2:3, ln:ln + 1] * xs_h)
                y_s[s_i, :, idx * P:(idx + 1) * P] = y_h
                xw_parts.append((xs_h * w_end_all[:, ln:ln + 1]).astype(BF16))
            xw = jnp.concatenate(xw_parts, axis=1)
            upd = _dot_tn(xw, Bg)
            for hh in range(hg):
                idx = gi * hg + hh
                ln = GATE_DT + idx
                r0 = idx * P
                s_s[s_i, r0:r0 + P, :] = (dec_end[:, ln:ln + 1] * s_s[s_i, r0:r0 + P, :]
                                          + upd[hh * P:(hh + 1) * P, :])

        for gi in range(S_GROUPS):
            sl = slice(gi * gw, (gi + 1) * gw)
            z = z_ref[s_i, :, sl]
            gg = y_s[s_i, :, sl] * (z * _sigmoid(z))
            gg = gg * lax.rsqrt(jnp.mean(gg * gg, axis=1, keepdims=True) + GN_EPS)
            y_ref[s_i, :, sl] = (gg * nw_ref[:, sl]).astype(y_ref.dtype)

    @pl.when(c == nc - 1)
    def _():
        s_out[0] = s_s[...]


def _ssd(z, xbc, gates, conv_w, conv_b, head_params, norm_w, init, prev, *, layer, depth, nseq, T, L):
    W = z.shape[1]
    nheads = W // S_HEADDIM
    CD = xbc.shape[1]
    nc = T // L
    G = _largest_divisor(nseq, SEQS_PER_STEP, 1)
    z, xbc, gates = (a.reshape(nseq, T, a.shape[1]) for a in (z, xbc, gates))
    kern = functools.partial(_ssd_kernel, L=L, nheads=nheads, has_init=init is not None, has_prev=prev is not None)
    s_spec = pl.BlockSpec((1, G, nheads * S_HEADDIM, S_STATE), lambda b, c: (layer, b, 0, 0))
    in_specs = [pl.BlockSpec((G, L, W), lambda b, c: (b, c, 0)),
                pl.BlockSpec((G, L, CD), lambda b, c: (b, c, 0)),
                pl.BlockSpec((G, L, LANES), lambda b, c: (b, c, 0)),
                pl.BlockSpec((S_CONV, CD), lambda b, c: (0, 0)),
                pl.BlockSpec((1, CD), lambda b, c: (0, 0)),
                pl.BlockSpec((SUBLANES, LANES), lambda b, c: (0, 0)),
                pl.BlockSpec((1, W), lambda b, c: (0, 0))]
    args = [z, xbc, gates, conv_w, conv_b, head_params, norm_w]
    if init is not None:
        in_specs += [s_spec, pl.BlockSpec((1, G, S_CONV - 1, CD), lambda b, c: (layer, b, 0, 0))]
        args += list(init)
    aliases = {}
    if prev is not None:
        aliases = {len(args) + t: 1 + t for t in range(2)}
        in_specs += [pl.BlockSpec(memory_space=pl.ANY)] * 2
        args += list(prev)
    y, S, cv = pl.pallas_call(
        kern,
        grid=(nseq // G, nc),
        in_specs=in_specs,
        out_specs=[pl.BlockSpec((G, L, W), lambda b, c: (b, c, 0)),
                   s_spec,
                   pl.BlockSpec((1, G, SUBLANES, CD), lambda b, c: (layer, b, 0, 0))],
        out_shape=[jax.ShapeDtypeStruct((nseq, T, W), BF16),
                   jax.ShapeDtypeStruct((depth, nseq, nheads * S_HEADDIM, S_STATE), F32),
                   jax.ShapeDtypeStruct((depth, nseq, SUBLANES, CD), F32)],
        scratch_shapes=[pltpu.VMEM((G, nheads * S_HEADDIM, S_STATE), F32),
                        pltpu.VMEM((G, SUBLANES + L, CD), F32),
                        pltpu.VMEM((G, L, W), F32)],
        input_output_aliases=aliases,
        compiler_params=_params(("parallel", "arbitrary")),
        name="ssd",
    )(*args)
    return y.reshape(nseq * T, W), S, cv


def _outproj_kernel(hm_ref, hs_ref, wt_ref, wb_ref, x_ref, g_ref, b_ref, y_ref, yb_ref, *, alpha):
    mix = _dot(hm_ref[...], wt_ref[0]) + _dot(hs_ref[...], wb_ref[0])
    y = _layer_norm(alpha * x_ref[...] + mix, g_ref[...], b_ref[...])
    y_ref[...] = y
    yb_ref[...] = y.astype(BF16)


def _outproj(hm, hs, w_out, x, ln_g, ln_b, *, alpha):
    m, d = x.shape
    w = hm.shape[1]
    tm = _largest_divisor(m, 512, 16)
    kern = functools.partial(_outproj_kernel, alpha=alpha)
    return pl.pallas_call(
        kern,
        grid=(m // tm,),
        in_specs=[pl.BlockSpec((tm, w), lambda i: (i, 0)),
                  pl.BlockSpec((tm, w), lambda i: (i, 0)),
                  pl.BlockSpec((1, w, d), lambda i: (0, 0, 0)),
                  pl.BlockSpec((1, w, d), lambda i: (1, 0, 0)),
                  pl.BlockSpec((tm, d), lambda i: (i, 0)),
                  pl.BlockSpec((1, d), lambda i: (0, 0)),
                  pl.BlockSpec((1, d), lambda i: (0, 0))],
        out_specs=[pl.BlockSpec((tm, d), lambda i: (i, 0)),
                   pl.BlockSpec((tm, d), lambda i: (i, 0))],
        out_shape=[jax.ShapeDtypeStruct((m, d), F32),
                   jax.ShapeDtypeStruct((m, d), BF16)],
        compiler_params=_params(("parallel",)),
        name="out_proj_ln",
    )(hm, hs, w_out, w_out, x, ln_g, ln_b)


def _ffn_kernel(*refs, sh, hr, tps, rc, nj, nsteps, alpha, has_init):
    xb_ref, xf_ref, wu_ref, cw_ref, cb_ref, wd_ref = refs[:6]
    pos = 6
    if has_init:
        h0_ref = refs[pos]
        pos += 1
    lg_ref, lb_ref, y_ref, yb_ref, oh_ref, acc, ext0, ext1, act, hal = refs[pos:pos + 10]
    s = pl.program_id(0)
    sa = jnp.minimum(s, nsteps - 2)
    ia, ja = sa // nj, sa % nj
    jb = jnp.maximum(s - 1, 0) % nj
    tm = xb_ref.shape[0]
    tf = wd_ref.shape[1]
    first = (ia % tps) == 0

    @pl.when(s == 0)
    def _():
        acc[...] = jnp.zeros_like(acc)
        ext1[...] = jnp.zeros_like(ext1)
        hal[...] = jnp.zeros_like(hal)

    def step(ext_a, ext_b):
        for r0 in range(0, tm, rc):
            c = (ext_b[hr - 2 * sh + r0:hr - 2 * sh + r0 + rc, :] * cw_ref[0, 0:1, :]
                 + ext_b[hr - sh + r0:hr - sh + r0 + rc, :] * cw_ref[0, 1:2, :]
                 + ext_b[hr + r0:hr + r0 + rc, :] * cw_ref[0, 2:3, :] + cb_ref[0])
            cg = c[:, :tf]
            act[r0:r0 + rc, :] = (cg * _sigmoid(cg) * c[:, tf:]).astype(BF16)
        init_rows = h0_ref[0, 0] if has_init else jnp.zeros((hr, 2 * tf), F32)
        ext_a[0:hr, :] = jnp.where(first, init_rows, hal[ja])
        ext_a[hr:hr + tm, :] = _dot(xb_ref[...], wu_ref[0])
        acc[...] = jnp.where(jb == 0, 0.0, acc[...]) + _dot(act[...], wd_ref[0])
        last = ext_a[tm:tm + hr, :]
        hal[ja] = last
        oh_ref[0, 0] = last

    @pl.when(s % 2 == 0)
    def _():
        step(ext0, ext1)

    @pl.when(s % 2 == 1)
    def _():
        step(ext1, ext0)

    @pl.when(jnp.logical_and(s >= 1, jb == nj - 1))
    def _():
        y = _layer_norm(alpha * xf_ref[...] + acc[...], lg_ref[...], lb_ref[...])
        y_ref[...] = y
        yb_ref[...] = y.astype(BF16)


def _ffn(xb, xf, w_u, cw, cb, w_d, h0, ln_g, ln_b, *, nseq, tm, sh, hr, alpha):
    m, d = xf.shape
    nj, tf = w_d.shape[0], w_d.shape[1]
    ni = m // tm
    tps = ni // nseq
    nsteps = ni * nj + 1
    kern = functools.partial(_ffn_kernel, sh=sh, hr=hr, tps=tps, rc=FFN_ROW_CHUNK, nj=nj, nsteps=nsteps,
                             alpha=alpha, has_init=h0 is not None)
    ia = lambda s: jnp.minimum(s, nsteps - 2) // nj
    ja = lambda s: jnp.minimum(s, nsteps - 2) % nj
    ib = lambda s: jnp.maximum(s - 1, 0) // nj
    jb = lambda s: jnp.maximum(s - 1, 0) % nj
    in_specs = [pl.BlockSpec((tm, d), lambda s: (ia(s), 0)),
                pl.BlockSpec((tm, d), lambda s: (ib(s), 0)),
                pl.BlockSpec((1, d, 2 * tf), lambda s: (ja(s), 0, 0)),
                pl.BlockSpec((1, FFN_CONV, 2 * tf), lambda s: (jb(s), 0, 0)),
                pl.BlockSpec((1, 1, 2 * tf), lambda s: (jb(s), 0, 0)),
                pl.BlockSpec((1, tf, d), lambda s: (jb(s), 0, 0))]
    args = [xb, xf, w_u, cw, cb, w_d]
    if h0 is not None:
        in_specs.append(pl.BlockSpec((1, 1, hr, 2 * tf), lambda s: (ia(s) // tps, ja(s), 0, 0)))
        args.append(h0)
    in_specs += [pl.BlockSpec((1, d), lambda s: (0, 0))] * 2
    args += [ln_g, ln_b]
    return pl.pallas_call(
        kern,
        grid=(nsteps,),
        in_specs=in_specs,
        out_specs=[pl.BlockSpec((tm, d), lambda s: (ib(s), 0)),
                   pl.BlockSpec((tm, d), lambda s: (ib(s), 0)),
                   pl.BlockSpec((1, 1, hr, 2 * tf),
                                lambda s: (ia(s) // tps, jnp.where(ia(s) % tps == tps - 1, ja(s), 0), 0, 0))],
        out_shape=[jax.ShapeDtypeStruct((m, d), F32),
                   jax.ShapeDtypeStruct((m, d), BF16),
                   jax.ShapeDtypeStruct((nseq, nj, hr, 2 * tf), F32)],
        scratch_shapes=[pltpu.VMEM((tm, d), F32),
                        pltpu.VMEM((hr + tm, 2 * tf), F32),
                        pltpu.VMEM((hr + tm, 2 * tf), F32),
                        pltpu.VMEM((tm, tf), BF16),
                        pltpu.VMEM((nj, hr, 2 * tf), F32)],
        compiler_params=_params(("arbitrary",)),
        name="conv_ffn_ln",
    )(*args)


def _ffn_tiles(g, v, tf):
    nj = g.shape[-1] // tf
    lead = g.shape[:-1]
    t = jnp.stack([g.reshape(lead + (nj, tf)), v.reshape(lead + (nj, tf))], axis=-2)
    return jnp.moveaxis(t.reshape(lead + (nj, 2 * tf)), -2, 0)


def _ffn_untile(t, tf):
    nj = t.shape[-3]
    t = jnp.moveaxis(t, -3, -2)
    lead = t.shape[:-2]
    return (t[..., :tf].reshape(lead + (nj * tf,)), t[..., tf:].reshape(lead + (nj * tf,)))


def _tile_up_kernel(w_ref, o_ref, *, dff, tf):
    nj = o_ref.shape[0]
    for j in range(nj):
        lo = j * tf
        width = min(tf, dff - lo)
        for half, base in ((0, 0), (1, dff)):
            o_ref[j, :, half * tf:half * tf + width] = w_ref[0, :, base + lo:base + lo + width].astype(BF16)
            if width < tf:
                o_ref[j, :, half * tf + width:(half + 1) * tf] = jnp.zeros((o_ref.shape[1], tf - width), BF16)


def _tile_up(w_up, layer, tf):
    _, d, two_dff = w_up.shape
    dff = two_dff // 2
    assert dff % LANES == 0
    nj = -(-dff // tf)
    tr = _largest_divisor(d, 128, 16)
    return pl.pallas_call(
        functools.partial(_tile_up_kernel, dff=dff, tf=tf),
        grid=(d // tr,),
        in_specs=[pl.BlockSpec((1, tr, two_dff), lambda r: (layer, r, 0))],
        out_specs=pl.BlockSpec((nj, tr, 2 * tf), lambda r: (0, r, 0)),
        out_shape=jax.ShapeDtypeStruct((nj, d, 2 * tf), BF16),
        compiler_params=_params(("parallel",)),
        name="tile_w_up",
    )(w_up)


def _tile_down_kernel(w_ref, o_ref, *, dff):
    j = pl.program_id(0)
    tf = o_ref.shape[1]
    rows = lax.broadcasted_iota(jnp.int32, o_ref.shape[1:], 0) + j * tf
    o_ref[0] = jnp.where(rows < dff, w_ref[0], 0.0).astype(BF16)


def _tile_down(w_down, layer, tf):
    _, dff, d = w_down.shape
    nj = -(-dff // tf)
    return pl.pallas_call(
        functools.partial(_tile_down_kernel, dff=dff),
        grid=(nj,),
        in_specs=[pl.BlockSpec((1, tf, d), lambda j: (layer, j, 0))],
        out_specs=pl.BlockSpec((1, tf, d), lambda j: (j, 0, 0)),
        out_shape=jax.ShapeDtypeStruct((nj, tf, d), BF16),
        compiler_params=_params(("parallel",)),
        name="tile_w_down",
    )(w_down)


def _prep_layer(l, w_in, b_i, b_f, m_norm_w, s_conv_w, s_conv_b, dt_bias, A_log, D_skip, s_norm_w, w_out,
                ln1_g, ln1_b, w_up, f_conv_w, f_conv_b, w_down, ln2_g, ln2_b):
    mw = m_norm_w.shape[1]
    sw = s_norm_w.shape[1]
    cd = s_conv_w.shape[2]
    nh = dt_bias.shape[1]
    dff = w_down.shape[1]
    tf = FFN_TILE
    fp = -(-dff // tf) * tf
    o = 0
    cols = {}
    for name, size in (("qkv", 3 * mw), ("o", mw), ("i", M_HEADS), ("f", M_HEADS),
                       ("z", sw), ("xbc", cd), ("dt", nh)):
        cols[name] = (o, o + size)
        o += size
    wl = w_in[l]
    cut = lambda name: wl[:, cols[name][0]:cols[name][1]]
    p = {}
    wg = jnp.concatenate([cut("i"), cut("f"), cut("dt")], axis=1)
    p["w_in"] = [cut("qkv").astype(BF16), cut("o").astype(BF16), cut("z").astype(BF16), cut("xbc").astype(BF16),
                 jnp.pad(wg, ((0, 0), (0, LANES - wg.shape[1]))).astype(BF16)]
    gb = jnp.concatenate([b_i[l], b_f[l]])
    p["gate_bias"] = jnp.pad(gb, (0, LANES - gb.shape[0]))[None, :]
    p["m_norm_w"] = m_norm_w[l][None, :]
    p["s_conv_w"] = s_conv_w[l]
    p["s_conv_b"] = s_conv_b[l][None, :]
    hp = jnp.stack([dt_bias[l], A_log[l], D_skip[l]])
    p["head_params"] = jnp.pad(hp, ((0, SUBLANES - 3), (GATE_DT, LANES - GATE_DT - nh)))
    p["s_norm_w"] = s_norm_w[l][None, :]
    p["w_out"] = w_out[l].astype(BF16).reshape(2, mw, w_out.shape[2])
    p["ln1_g"] = ln1_g[l][None, :]
    p["ln1_b"] = ln1_b[l][None, :]
    padc = lambda a: jnp.pad(a, ((0, 0), (0, fp - dff)))
    p["w_up"] = _tile_up(w_up, l, tf)
    p["f_conv_w"] = _ffn_tiles(padc(f_conv_w[l][:, :dff]), padc(f_conv_w[l][:, dff:]), tf)
    p["f_conv_b"] = _ffn_tiles(padc(f_conv_b[l][None, :dff]), padc(f_conv_b[l][None, dff:]), tf)
    p["w_down"] = _tile_down(w_down, l, tf)
    p["ln2_g"] = ln2_g[l][None, :]
    p["ln2_b"] = ln2_b[l][None, :]
    p["dff"] = dff
    p["fp"] = fp
    return p


def _layer(xf, xb, p, init, prev, *, layer, depth, nseq, T, L, time_major_ffn, alpha):
    qkv, o_pre, z, xbc, gates = _in_proj(xb, p["w_in"], [BF16, F32, F32, F32, F32], tm_target=1024)
    hm, C, n, m = _mlstm(qkv, o_pre, gates, p["gate_bias"], p["m_norm_w"],
                         None if init is None else init["mlstm"], None if prev is None else prev["mlstm"],
                         layer=layer, depth=depth, nseq=nseq, T=T, L=L)
    hs, S, sconv = _ssd(z, xbc, gates, p["s_conv_w"], p["s_conv_b"], p["head_params"], p["s_norm_w"],
                        None if init is None else init["ssd"], None if prev is None else prev["ssd"],
                        layer=layer, depth=depth, nseq=nseq, T=T, L=L)
    x1f, x1b = _outproj(hm, hs, p["w_out"], xf, p["ln1_g"], p["ln1_b"], alpha=alpha)
    d = xf.shape[1]
    dff = p["dff"]
    ffn_w = (p["w_up"], p["f_conv_w"], p["f_conv_b"], p["w_down"])
    h0 = None if init is None else init["ffn"]
    if time_major_ffn:
        tr = lambda a: a.reshape(nseq, T, d).transpose(1, 0, 2).reshape(nseq * T, d)
        x1f_t, x1b_t = tr(x1f), tr(x1b)
        tm = _largest_divisor(nseq * T, 256, (FFN_CONV - 1) * nseq)
        y_f, y_b, oh = _ffn(x1b_t, x1f_t, *ffn_w, h0, p["ln2_g"], p["ln2_b"],
                            nseq=1, tm=tm, sh=nseq, hr=(FFN_CONV - 1) * nseq, alpha=alpha)
        tb = lambda a: a.reshape(T, nseq, d).transpose(1, 0, 2).reshape(nseq * T, d)
        y_f, y_b = tb(y_f), tb(y_b)
        og, ov = _ffn_untile(oh, FFN_TILE)
        fconv = jnp.concatenate([og[0, :, :dff], ov[0, :, :dff]], axis=1)
        fconv = fconv.reshape(FFN_CONV - 1, nseq, 2 * dff).transpose(1, 0, 2)
    else:
        tm = _largest_divisor(T, 512, FFN_ROW_CHUNK)
        y_f, y_b, oh = _ffn(x1b, x1f, *ffn_w, h0, p["ln2_g"], p["ln2_b"],
                            nseq=nseq, tm=tm, sh=1, hr=SUBLANES, alpha=alpha)
        og, ov = _ffn_untile(oh, FFN_TILE)
        k = FFN_CONV - 1
        fconv = jnp.concatenate([og[:, SUBLANES - k:, :dff], ov[:, SUBLANES - k:, :dff]], axis=2)
    return y_f, y_b, dict(mlstm=(C, n, m), ssd=(S, sconv)), fconv


def kernel(x_prompt, x_sample, state_mlstm_C, state_mlstm_n, state_mlstm_m, state_ssm, state_ssm_conv,
           state_ffn_conv, w_in, mlstm_b_i, mlstm_b_f, mlstm_norm_w, ssm_conv_w, ssm_conv_b, ssm_dt_bias,
           ssm_A_log, ssm_D, ssm_norm_w, w_out, ln1_g, ln1_b, ffn_w_up, ffn_conv_w, ffn_conv_b,
           ffn_w_down, ln2_g, ln2_b):
    weights = (w_in, mlstm_b_i, mlstm_b_f, mlstm_norm_w, ssm_conv_w, ssm_conv_b, ssm_dt_bias, ssm_A_log,
               ssm_D, ssm_norm_w, w_out, ln1_g, ln1_b, ffn_w_up, ffn_conv_w, ffn_conv_b, ffn_w_down,
               ln2_g, ln2_b)
    depth = w_in.shape[0]
    alpha = float((2 * depth) ** 0.25)
    B, T, D = x_prompt.shape
    Bs, Ts, _ = x_sample.shape
    nh = ssm_dt_bias.shape[1]
    Lp = PROMPT_CHUNK if T % PROMPT_CHUNK == 0 else T
    Ls = PROMPT_CHUNK if Ts % PROMPT_CHUNK == 0 else Ts

    xpf = x_prompt.reshape(B * T, D)
    xpb = xpf.astype(BF16)
    xsf = x_sample.reshape(Bs * Ts, D)
    xsb = xsf.astype(BF16)
    m_in = jnp.pad(state_mlstm_m, ((0, 0), (0, 0), (GATE_F, LANES - GATE_F - M_HEADS)))[:, :, None, :]
    s_in = state_ssm.reshape(depth, Bs, nh * S_HEADDIM, S_STATE)
    p_prev, s_prev, p_fconv, s_fconv = None, None, [], []
    for l in range(depth):
        p = _prep_layer(l, *weights)
        fp, dff = p["fp"], p["dff"]
        xpf, xpb, p_prev, fc = _layer(xpf, xpb, p, None, p_prev, layer=l, depth=depth, nseq=B, T=T, L=Lp,
                                      time_major_ffn=False, alpha=alpha)
        p_fconv.append(fc)
        fc0 = state_ffn_conv[l].transpose(1, 0, 2).reshape((FFN_CONV - 1) * Bs, 2 * dff)
        padf = lambda a: jnp.pad(a, ((0, 0), (0, fp - dff)))
        init = dict(mlstm=(state_mlstm_C, state_mlstm_n, m_in), ssd=(s_in, state_ssm_conv),
                    ffn=_ffn_tiles(padf(fc0[:, :dff]), padf(fc0[:, dff:]), FFN_TILE)[None])
        xsf, xsb, s_prev, fc = _layer(xsf, xsb, p, init, s_prev, layer=l, depth=depth, nseq=Bs, T=Ts, L=Ls,
                                      time_major_ffn=True, alpha=alpha)
        s_fconv.append(fc)

    def unpack(st, nseq):
        C, n, m = st["mlstm"]
        S, sconv = st["ssd"]
        return (C, n, m[:, :, 0, GATE_F:GATE_F + M_HEADS], S.reshape(depth, nseq, nh, S_HEADDIM, S_STATE),
                sconv[:, :, SUBLANES - (S_CONV - 1):, :])

    pC, pn, pm, pS, psc = unpack(p_prev, B)
    sC, sn, sm, sS, ssc = unpack(s_prev, Bs)
    return (xpf.reshape(B, T, D), xsf.reshape(Bs, Ts, D), pC, pn, pm, pS, psc, jnp.stack(p_fconv),
            sC, sn, sm, sS, ssc, jnp.stack(s_fconv))
```

```python
import functools

import jax
import jax.numpy as jnp
from jax import lax
from jax.experimental import pallas as pl
from jax.experimental.pallas import tpu as pltpu

F32 = jnp.float32
BF16 = jnp.bfloat16
HIGHEST = lax.Precision.HIGHEST

M_HEADS = 4
S_HEADDIM = 64
S_GROUPS = 2
S_STATE = 128
S_CONV = 4
FFN_CONV = 3
LN_EPS = 1e-5
GN_EPS = 1e-6
NEG_BIG = -1e30

LANES = 128
SUBLANES = 8
MXU_DIM = 256
VMEM_LIMIT_BYTES = 52 * 1024 * 1024

GATE_I = 0
GATE_F = 4
GATE_DT = 8

PROMPT_CHUNK = 128
SEQS_PER_STEP = 4
IN_PROJ_TILE = 2 * MXU_DIM
FFN_TILE = 2 * MXU_DIM
FFN_ROW_CHUNK = 32


def _largest_divisor(n, target, mult):
    best = None
    for d in range(mult, min(n, target) + 1, mult):
        if n % d == 0:
            best = d
    if best is None:
        raise ValueError(f"no tile for {n=} {target=} {mult=}")
    return best


def _params(sem, flags=None):
    return pltpu.CompilerParams(dimension_semantics=sem, vmem_limit_bytes=VMEM_LIMIT_BYTES, flags=flags)


def _dot(a, b):
    return jnp.dot(a, b, preferred_element_type=F32)


def _dot_nt(a, b, precision=None):
    return lax.dot_general(a, b, (((1,), (1,)), ((), ())), precision=precision, preferred_element_type=F32)


def _dot_tn(a, b):
    return lax.dot_general(a, b, (((0,), (0,)), ((), ())), preferred_element_type=F32)


def _sigmoid(x):
    return 1.0 / (1.0 + jnp.exp(-x))


def _softplus(x):
    return jnp.maximum(x, 0.0) + jnp.log1p(jnp.exp(-jnp.abs(x)))


def _layer_norm(r, g, b):
    mu = jnp.mean(r, axis=-1, keepdims=True)
    d = r - mu
    var = jnp.mean(d * d, axis=-1, keepdims=True)
    return d * lax.rsqrt(var + LN_EPS) * g + b


def _in_proj_kernel(*refs, groups):
    x_ref = refs[0]
    ng = len(groups)
    w_refs = refs[1:1 + ng]
    o_refs = refs[1 + ng:1 + 2 * ng]
    j = pl.program_id(1)
    for (start, count), w_ref, o_ref in zip(groups, w_refs, o_refs):
        @pl.when(jnp.logical_and(j >= start, j < start + count))
        def _(w_ref=w_ref, o_ref=o_ref):
            o_ref[...] = _dot(x_ref[...], w_ref[...]).astype(o_ref.dtype)


def _in_proj(x, weights, out_dtypes, tm_target):
    m, k = x.shape
    tm = _largest_divisor(m, tm_target, 16)
    tiles, groups, start = [], [], 0
    for w in weights:
        n = w.shape[1]
        tn = _largest_divisor(n, IN_PROJ_TILE, LANES)
        tiles.append(tn)
        groups.append((start, n // tn))
        start += n // tn

    def clamp(s, c):
        return lambda i, j: (0, jnp.clip(j - s, 0, c - 1))

    def clamp_out(s, c):
        return lambda i, j: (i, jnp.clip(j - s, 0, c - 1))

    in_specs = [pl.BlockSpec((tm, k), lambda i, j: (i, 0))]
    in_specs += [pl.BlockSpec((k, tn), clamp(s, c)) for tn, (s, c) in zip(tiles, groups)]
    out_specs = [pl.BlockSpec((tm, tn), clamp_out(s, c)) for tn, (s, c) in zip(tiles, groups)]
    out_shape = [jax.ShapeDtypeStruct((m, w.shape[1]), dt) for w, dt in zip(weights, out_dtypes)]
    return pl.pallas_call(
        functools.partial(_in_proj_kernel, groups=tuple(groups)),
        grid=(m // tm, start),
        in_specs=in_specs,
        out_specs=out_specs,
        out_shape=out_shape,
        compiler_params=_params(("parallel", "arbitrary")),
        name="in_proj",
    )(x, *weights)


def _mlstm_kernel(*refs, L, dk, has_init, has_prev):
    q_ref, k_ref, v_ref, o_ref, g_ref, gb_ref, nw_ref = refs[:7]
    pos = 7
    if has_init:
        c0_ref, n0_ref, m0_ref = refs[pos:pos + 3]
        pos += 3
    if has_prev:
        pos += 3
    h_ref, c_out, n_out, m_out, c_s, n_s, m_s = refs[pos:pos + 7]
    c = pl.program_id(1)
    nc = pl.num_programs(1)

    @pl.when(c == 0)
    def _():
        if has_init:
            c_s[...] = c0_ref[0]
            n_s[...] = n0_ref[0]
            m_s[...] = m0_ref[0]
        else:
            c_s[...] = jnp.zeros_like(c_s)
            n_s[...] = jnp.zeros_like(n_s)
            m_s[...] = jnp.zeros_like(m_s)

    row = lax.broadcasted_iota(jnp.int32, (L, L), 0)
    col = lax.broadcasted_iota(jnp.int32, (L, L), 1)
    causal = col <= row
    tri = jnp.where(causal, 1.0, 0.0).astype(F32)
    lane = lax.broadcasted_iota(jnp.int32, (L, LANES), 1)
    sel = jnp.where(lax.broadcasted_iota(jnp.int32, (SUBLANES, LANES), 0)
                    == lax.broadcasted_iota(jnp.int32, (SUBLANES, LANES), 1), 1.0, 0.0).astype(F32)
    m_lane = lax.broadcasted_iota(jnp.int32, (1, LANES), 1)
    k_scale = dk ** -0.5

    nsq = q_ref.shape[0]
    pairs = [(s_i, h) for s_i in range(nsq) for h in range(M_HEADS)]
    hsl = lambda h: slice(h * dk, (h + 1) * dk)


    gs, bcums, rowss, m_vecs = [], [], [], []
    for s_i in range(nsq):
        g = g_ref[s_i] + gb_ref[...]
        lf = jnp.minimum(g, 0.0) - jnp.log1p(jnp.exp(-jnp.abs(g)))
        bcum = jnp.dot(tri, lf, precision=HIGHEST, preferred_element_type=F32)
        packed = jnp.where(lane < GATE_F, g, bcum)
        gs.append(g)
        bcums.append(bcum)
        rowss.append(_dot_nt(sel, packed, precision=HIGHEST))
        m_vecs.append(m_s[s_i])

    st = []
    for s_i, h in pairs:
        bc = bcums[s_i][:, GATE_F + h:GATE_F + h + 1]
        igr = rowss[s_i][GATE_I + h:GATE_I + h + 1, :]
        br = rowss[s_i][GATE_F + h:GATE_F + h + 1, :]
        m_h = m_vecs[s_i][:, GATE_F + h:GATE_F + h + 1]
        log_w = jnp.where(causal, bc - br + igr, NEG_BIG)
        log_inter = bc + m_h
        m_t = jnp.maximum(log_inter, jnp.max(log_w, axis=1, keepdims=True))
        st.append(dict(bc=bc, m_h=m_h, m_t=m_t, w_intra=jnp.exp(log_w - m_t), w_inter=jnp.exp(log_inter - m_t)))

    for (s_i, h), e in zip(pairs, st):
        q = q_ref[s_i, :, hsl(h)]
        kf = k_ref[s_i, :, hsl(h)].astype(F32) * k_scale
        v = v_ref[s_i, :, hsl(h)]
        s = _dot_nt(q, kf.astype(BF16)) * e["w_intra"]
        num = _dot(s.astype(BF16), v) + e["w_inter"] * _dot(q, c_s[s_i, h].astype(BF16))
        qn = jnp.sum(q.astype(F32) * n_s[s_i, h:h + 1, :], axis=1, keepdims=True)
        den = jnp.sum(s, axis=1, keepdims=True) + e["w_inter"] * qn
        e["hh"] = num / jnp.maximum(jnp.abs(den), jnp.exp(-e["m_t"]))

    m_news = list(m_vecs)
    for (s_i, h), e in zip(pairs, st):
        bc, m_h, m_t = e["bc"], e["m_h"], e["m_t"]
        igc = gs[s_i][:, GATE_I + h:GATE_I + h + 1]
        m_end = m_t[L - 1:L, :]
        b_end = bc[L - 1:L, :]
        w_end = jnp.exp(b_end - bc + igc - m_end)
        decay = jnp.exp(b_end + m_h - m_end)
        kw = k_ref[s_i, :, hsl(h)].astype(F32) * k_scale * w_end
        c_s[s_i, h] = decay * c_s[s_i, h] + _dot_tn(kw.astype(BF16), v_ref[s_i, :, hsl(h)])
        n_s[s_i, h:h + 1, :] = decay * n_s[s_i, h:h + 1, :] + jnp.sum(kw, axis=0, keepdims=True)
        m_news[s_i] = jnp.where(m_lane == GATE_F + h, m_end, m_news[s_i])
    for s_i in range(nsq):
        m_s[s_i] = m_news[s_i]

    for (s_i, h), e in zip(pairs, st):
        hh = e["hh"]
        mu = jnp.mean(hh, axis=1, keepdims=True)
        d = hh - mu
        var = jnp.mean(d * d, axis=1, keepdims=True)
        hn = d * lax.rsqrt(var + GN_EPS) * nw_ref[:, hsl(h)]
        h_ref[s_i, :, hsl(h)] = (hn * _sigmoid(o_ref[s_i, :, hsl(h)].astype(F32))).astype(h_ref.dtype)

    @pl.when(c == nc - 1)
    def _():
        c_out[0] = c_s[...]
        n_out[0] = n_s[...]
        m_out[0] = m_s[...]


def _mlstm(qkv, o_pre, gates, gate_bias, norm_w, init, prev, *, layer, depth, nseq, T, L):
    W = qkv.shape[1] // 3
    dk = W // M_HEADS
    nc = T // L
    G = _largest_divisor(nseq, SEQS_PER_STEP, 1)
    qkv, o_pre, gates = (a.reshape(nseq, T, a.shape[1]) for a in (qkv, o_pre, gates))
    kern = functools.partial(_mlstm_kernel, L=L, dk=dk, has_init=init is not None, has_prev=prev is not None)
    st_specs = [pl.BlockSpec((1, G, M_HEADS, dk, dk), lambda b, c: (layer, b, 0, 0, 0)),
                pl.BlockSpec((1, G, M_HEADS, dk), lambda b, c: (layer, b, 0, 0)),
                pl.BlockSpec((1, G, 1, LANES), lambda b, c: (layer, b, 0, 0))]
    in_specs = [pl.BlockSpec((G, L, W), lambda b, c: (b, c, 0)),
                pl.BlockSpec((G, L, W), lambda b, c: (b, c, 1)),
                pl.BlockSpec((G, L, W), lambda b, c: (b, c, 2)),
                pl.BlockSpec((G, L, W), lambda b, c: (b, c, 0)),
                pl.BlockSpec((G, L, LANES), lambda b, c: (b, c, 0)),
                pl.BlockSpec((1, LANES), lambda b, c: (0, 0)),
                pl.BlockSpec((1, W), lambda b, c: (0, 0))]
    args = [qkv, qkv, qkv, o_pre, gates, gate_bias, norm_w]
    if init is not None:
        in_specs += st_specs
        args += list(init)
    aliases = {}
    if prev is not None:
        aliases = {len(args) + t: 1 + t for t in range(3)}
        in_specs += [pl.BlockSpec(memory_space=pl.ANY)] * 3
        args += list(prev)
    h, C, n, m = pl.pallas_call(
        kern,
        grid=(nseq // G, nc),
        in_specs=in_specs,
        out_specs=[pl.BlockSpec((G, L, W), lambda b, c: (b, c, 0))] + st_specs,
        out_shape=[jax.ShapeDtypeStruct((nseq, T, W), BF16),
                   jax.ShapeDtypeStruct((depth, nseq, M_HEADS, dk, dk), F32),
                   jax.ShapeDtypeStruct((depth, nseq, M_HEADS, dk), F32),
                   jax.ShapeDtypeStruct((depth, nseq, 1, LANES), F32)],
        scratch_shapes=[pltpu.VMEM((G, M_HEADS, dk, dk), F32),
                        pltpu.VMEM((G, M_HEADS, dk), F32),
                        pltpu.VMEM((G, 1, LANES), F32)],
        input_output_aliases=aliases,
        compiler_params=_params(("parallel", "arbitrary")),
        name="mlstm",
    )(*args)
    return h.reshape(nseq * T, W), C, n, m


def _ssd_kernel(*refs, L, nheads, has_init, has_prev):
    z_ref, x_ref, g_ref, cw_ref, cb_ref, hp_ref, nw_ref = refs[:7]
    pos = 7
    if has_init:
        s0_ref, cv0_ref = refs[pos:pos + 2]
        pos += 2
    if has_prev:
        pos += 2
    y_ref, s_out, cv_out, s_s, xp_s, y_s = refs[pos:pos + 6]
    c = pl.program_id(1)
    nc = pl.num_programs(1)
    P = S_HEADDIM
    N = S_STATE
    hg = nheads // S_GROUPS
    W = nheads * P
    halo = SUBLANES
    kc = S_CONV

    G = z_ref.shape[0]

    @pl.when(c == 0)
    def _():
        xp_s[:, 0:halo, :] = jnp.zeros((G, halo, xp_s.shape[2]), F32)
        if has_init:
            s_s[...] = s0_ref[0]
            xp_s[:, halo - (kc - 1):halo, :] = cv0_ref[0]
        else:
            s_s[...] = jnp.zeros_like(s_s)

    hp = hp_ref[...]
    row = lax.broadcasted_iota(jnp.int32, (L, L), 0)
    col = lax.broadcasted_iota(jnp.int32, (L, L), 1)
    causal = col <= row
    tri = jnp.where(causal, 1.0, 0.0).astype(F32)
    sel = jnp.where(lax.broadcasted_iota(jnp.int32, (nheads, LANES), 0) + GATE_DT
                    == lax.broadcasted_iota(jnp.int32, (nheads, LANES), 1), 1.0, 0.0).astype(F32)
    gw = W // S_GROUPS

    for s_i in range(G):
        xp_s[s_i, halo:halo + L, :] = x_ref[s_i].astype(F32)
        conv = cb_ref[...] + xp_s[s_i, halo:halo + L, :] * cw_ref[kc - 1:kc, :]
        for j in range(kc - 1):
            off = halo - (kc - 1) + j
            conv = conv + xp_s[s_i, off:off + L, :] * cw_ref[j:j + 1, :]
        tail = xp_s[s_i, L:L + halo, :]
        xp_s[s_i, 0:halo, :] = tail
        cv_out[0, s_i] = tail
        xbc = conv * _sigmoid(conv)
        xs = xbc[:, 0:W]
        Bm = xbc[:, W:W + S_GROUPS * N]
        Cm = xbc[:, W + S_GROUPS * N:W + 2 * S_GROUPS * N]

        dt = _softplus(g_ref[s_i] + hp[0:1, :])
        a = dt * (-jnp.exp(hp[1:2, :]))
        bcum = jnp.dot(tri, a, precision=HIGHEST, preferred_element_type=F32)
        b_rows = _dot_nt(sel, bcum, precision=HIGHEST)
        dt_rows = _dot_nt(sel, dt, precision=HIGHEST)
        eb = jnp.exp(bcum)
        b_last = bcum[L - 1:L, :]
        w_end_all = jnp.exp(b_last - bcum) * dt
        dec_end = jnp.exp(b_last)

        for gi in range(S_GROUPS):
            Bg = Bm[:, gi * N:(gi + 1) * N].astype(BF16)
            Cg = Cm[:, gi * N:(gi + 1) * N].astype(BF16)
            cb = _dot_nt(Cg, Bg)
            s_g = s_s[s_i, gi * hg * P:(gi + 1) * hg * P, :]
            y_inter = _dot_nt(Cg, s_g.astype(BF16))
            xw_parts = []
            for hh in range(hg):
                idx = gi * hg + hh
                ln = GATE_DT + idx
                bc = bcum[:, ln:ln + 1]
                br = b_rows[idx:idx + 1, :]
                dtr = dt_rows[idx:idx + 1, :]
                decay = jnp.exp(jnp.where(causal, bc - br, NEG_BIG))
                mw = cb * decay * dtr
                xs_h = xs[:, idx * P:(idx + 1) * P]
                y_h = (_dot(mw.astype(BF16), xs_h.astype(BF16))
                       + eb[:, ln:ln + 1] * y_inter[:, hh * P:(hh + 1) * P]
                       + hp[2:3, ln:ln + 1] * xs_h)
                y_s[s_i, :, idx * P:(idx + 1) * P] = y_h
                xw_parts.append((xs_h * w_end_all[:, ln:ln + 1]).astype(BF16))
            xw = jnp.concatenate(xw_parts, axis=1)
            upd = _dot_tn(xw, Bg)
            for hh in range(hg):
                idx = gi * hg + hh
                ln = GATE_DT + idx
                r0 = idx * P
                s_s[s_i, r0:r0 + P, :] = (dec_end[:, ln:ln + 1] * s_s[s_i, r0:r0 + P, :]
                                          + upd[hh * P:(hh + 1) * P, :])

        for gi in range(S_GROUPS):
            sl = slice(gi * gw, (gi + 1) * gw)
            z = z_ref[s_i, :, sl].astype(F32)
            gg = y_s[s_i, :, sl] * (z * _sigmoid(z))
            gg = gg * lax.rsqrt(jnp.mean(gg * gg, axis=1, keepdims=True) + GN_EPS)
            y_ref[s_i, :, sl] = (gg * nw_ref[:, sl]).astype(y_ref.dtype)

    @pl.when(c == nc - 1)
    def _():
        s_out[0] = s_s[...]


def _ssd(z, xbc, gates, conv_w, conv_b, head_params, norm_w, init, prev, *, layer, depth, nseq, T, L):
    W = z.shape[1]
    nheads = W // S_HEADDIM
    CD = xbc.shape[1]
    nc = T // L
    G = _largest_divisor(nseq, SEQS_PER_STEP, 1) if L < PROMPT_CHUNK else 1
    z, xbc, gates = (a.reshape(nseq, T, a.shape[1]) for a in (z, xbc, gates))
    kern = functools.partial(_ssd_kernel, L=L, nheads=nheads, has_init=init is not None, has_prev=prev is not None)
    s_spec = pl.BlockSpec((1, G, nheads * S_HEADDIM, S_STATE), lambda b, c: (layer, b, 0, 0))
    in_specs = [pl.BlockSpec((G, L, W), lambda b, c: (b, c, 0)),
                pl.BlockSpec((G, L, CD), lambda b, c: (b, c, 0)),
                pl.BlockSpec((G, L, LANES), lambda b, c: (b, c, 0)),
                pl.BlockSpec((S_CONV, CD), lambda b, c: (0, 0)),
                pl.BlockSpec((1, CD), lambda b, c: (0, 0)),
                pl.BlockSpec((SUBLANES, LANES), lambda b, c: (0, 0)),
                pl.BlockSpec((1, W), lambda b, c: (0, 0))]
    args = [z, xbc, gates, conv_w, conv_b, head_params, norm_w]
    if init is not None:
        in_specs += [s_spec, pl.BlockSpec((1, G, S_CONV - 1, CD), lambda b, c: (layer, b, 0, 0))]
        args += list(init)
    aliases = {}
    if prev is not None:
        aliases = {len(args) + t: 1 + t for t in range(2)}
        in_specs += [pl.BlockSpec(memory_space=pl.ANY)] * 2
        args += list(prev)
    y, S, cv = pl.pallas_call(
        kern,
        grid=(nseq // G, nc),
        in_specs=in_specs,
        out_specs=[pl.BlockSpec((G, L, W), lambda b, c: (b, c, 0)),
                   s_spec,
                   pl.BlockSpec((1, G, SUBLANES, CD), lambda b, c: (layer, b, 0, 0))],
        out_shape=[jax.ShapeDtypeStruct((nseq, T, W), BF16),
                   jax.ShapeDtypeStruct((depth, nseq, nheads * S_HEADDIM, S_STATE), F32),
                   jax.ShapeDtypeStruct((depth, nseq, SUBLANES, CD), F32)],
        scratch_shapes=[pltpu.VMEM((G, nheads * S_HEADDIM, S_STATE), F32),
                        pltpu.VMEM((G, SUBLANES + L, CD), F32),
                        pltpu.VMEM((G, L, W), F32)],
        input_output_aliases=aliases,
        compiler_params=_params(("parallel", "arbitrary")),
        name="ssd",
    )(*args)
    return y.reshape(nseq * T, W), S, cv


def _outproj_kernel(hm_ref, hs_ref, wt_ref, wb_ref, x_ref, g_ref, b_ref, y_ref, yb_ref, *, alpha):
    mix = _dot(hm_ref[...], wt_ref[0]) + _dot(hs_ref[...], wb_ref[0])
    y = _layer_norm(alpha * x_ref[...] + mix, g_ref[...], b_ref[...])
    y_ref[...] = y
    yb_ref[...] = y.astype(BF16)


def _outproj(hm, hs, w_out, x, ln_g, ln_b, *, alpha):
    m, d = x.shape
    w = hm.shape[1]
    tm = _largest_divisor(m, 512, 16)
    kern = functools.partial(_outproj_kernel, alpha=alpha)
    return pl.pallas_call(
        kern,
        grid=(m // tm,),
        in_specs=[pl.BlockSpec((tm, w), lambda i: (i, 0)),
                  pl.BlockSpec((tm, w), lambda i: (i, 0)),
                  pl.BlockSpec((1, w, d), lambda i: (0, 0, 0)),
                  pl.BlockSpec((1, w, d), lambda i: (1, 0, 0)),
                  pl.BlockSpec((tm, d), lambda i: (i, 0)),
                  pl.BlockSpec((1, d), lambda i: (0, 0)),
                  pl.BlockSpec((1, d), lambda i: (0, 0))],
        out_specs=[pl.BlockSpec((tm, d), lambda i: (i, 0)),
                   pl.BlockSpec((tm, d), lambda i: (i, 0))],
        out_shape=[jax.ShapeDtypeStruct((m, d), F32),
                   jax.ShapeDtypeStruct((m, d), BF16)],
        compiler_params=_params(("parallel",)),
        name="out_proj_ln",
    )(hm, hs, w_out, w_out, x, ln_g, ln_b)


def _ffn_kernel(*refs, sh, hr, tps, rc, alpha, has_init):
    xb_ref, xf_ref, wu_ref, cw_ref, cb_ref, wd_ref = refs[:6]
    pos = 6
    if has_init:
        h0_ref = refs[pos]
        pos += 1
    lg_ref, lb_ref, y_ref, yb_ref, oh_ref, acc, ext, act, hal = refs[pos:pos + 9]
    i = pl.program_id(0)
    j = pl.program_id(1)
    nj = pl.num_programs(1)
    tm = xb_ref.shape[0]
    tf = wd_ref.shape[1]
    first = (i % tps) == 0

    @pl.when(first)
    def _():
        if has_init:
            ext[0:hr, :] = h0_ref[0, 0]
        else:
            ext[0:hr, :] = jnp.zeros((hr, ext.shape[1]), F32)

    @pl.when(jnp.logical_not(first))
    def _():
        ext[0:hr, :] = hal[j]

    ext[hr:hr + tm, :] = _dot(xb_ref[...], wu_ref[0])
    last = ext[tm:tm + hr, :]
    hal[j] = last
    oh_ref[0, 0] = last

    for r0 in range(0, tm, rc):
        c = (ext[hr - 2 * sh + r0:hr - 2 * sh + r0 + rc, :] * cw_ref[0, 0:1, :]
             + ext[hr - sh + r0:hr - sh + r0 + rc, :] * cw_ref[0, 1:2, :]
             + ext[hr + r0:hr + r0 + rc, :] * cw_ref[0, 2:3, :] + cb_ref[0])
        cg = c[:, :tf]
        act[r0:r0 + rc, :] = (cg * _sigmoid(cg) * c[:, tf:]).astype(BF16)

    @pl.when(j == 0)
    def _():
        acc[...] = jnp.zeros_like(acc)

    acc[...] += _dot(act[...], wd_ref[0])

    @pl.when(j == nj - 1)
    def _():
        y = _layer_norm(alpha * xf_ref[...] + acc[...], lg_ref[...], lb_ref[...])
        y_ref[...] = y
        yb_ref[...] = y.astype(BF16)


def _ffn(xb, xf, w_u, cw, cb, w_d, h0, ln_g, ln_b, *, nseq, tm, sh, hr, alpha):
    m, d = xf.shape
    nj, tf = w_d.shape[0], w_d.shape[1]
    ni = m // tm
    tps = ni // nseq
    kern = functools.partial(_ffn_kernel, sh=sh, hr=hr, tps=tps, rc=FFN_ROW_CHUNK, alpha=alpha,
                             has_init=h0 is not None)
    in_specs = [pl.BlockSpec((tm, d), lambda i, j: (i, 0)),
                pl.BlockSpec((tm, d), lambda i, j: (i, 0)),
                pl.BlockSpec((1, d, 2 * tf), lambda i, j: (j, 0, 0)),
                pl.BlockSpec((1, FFN_CONV, 2 * tf), lambda i, j: (j, 0, 0)),
                pl.BlockSpec((1, 1, 2 * tf), lambda i, j: (j, 0, 0)),
                pl.BlockSpec((1, tf, d), lambda i, j: (j, 0, 0))]
    args = [xb, xf, w_u, cw, cb, w_d]
    if h0 is not None:
        in_specs.append(pl.BlockSpec((1, 1, hr, 2 * tf), lambda i, j: (i // tps, j, 0, 0)))
        args.append(h0)
    in_specs += [pl.BlockSpec((1, d), lambda i, j: (0, 0))] * 2
    args += [ln_g, ln_b]
    return pl.pallas_call(
        kern,
        grid=(ni, nj),
        in_specs=in_specs,
        out_specs=[pl.BlockSpec((tm, d), lambda i, j: (i, 0)),
                   pl.BlockSpec((tm, d), lambda i, j: (i, 0)),
                   pl.BlockSpec((1, 1, hr, 2 * tf),
                                lambda i, j: (i // tps, jnp.where(i % tps == tps - 1, j, 0), 0, 0))],
        out_shape=[jax.ShapeDtypeStruct((m, d), F32),
                   jax.ShapeDtypeStruct((m, d), BF16),
                   jax.ShapeDtypeStruct((nseq, nj, hr, 2 * tf), F32)],
        scratch_shapes=[pltpu.VMEM((tm, d), F32),
                        pltpu.VMEM((hr + tm, 2 * tf), F32),
                        pltpu.VMEM((tm, tf), BF16),
                        pltpu.VMEM((nj, hr, 2 * tf), F32)],
        compiler_params=_params(("arbitrary", "arbitrary")),
        name="conv_ffn_ln",
    )(*args)


def _ffn_tiles(g, v, tf):
    nj = g.shape[-1] // tf
    lead = g.shape[:-1]
    t = jnp.stack([g.reshape(lead + (nj, tf)), v.reshape(lead + (nj, tf))], axis=-2)
    return jnp.moveaxis(t.reshape(lead + (nj, 2 * tf)), -2, 0)


def _ffn_untile(t, tf):
    nj = t.shape[-3]
    t = jnp.moveaxis(t, -3, -2)
    lead = t.shape[:-2]
    return (t[..., :tf].reshape(lead + (nj * tf,)), t[..., tf:].reshape(lead + (nj * tf,)))


def _tile_up_kernel(w_ref, o_ref, *, dff, tf):
    nj = o_ref.shape[0]
    for j in range(nj):
        lo = j * tf
        width = min(tf, dff - lo)
        for half, base in ((0, 0), (1, dff)):
            o_ref[j, :, half * tf:half * tf + width] = w_ref[0, :, base + lo:base + lo + width].astype(BF16)
            if width < tf:
                o_ref[j, :, half * tf + width:(half + 1) * tf] = jnp.zeros((o_ref.shape[1], tf - width), BF16)


def _tile_up(w_up, layer, tf):
    _, d, two_dff = w_up.shape
    dff = two_dff // 2
    assert dff % LANES == 0
    nj = -(-dff // tf)
    tr = _largest_divisor(d, 128, 16)
    return pl.pallas_call(
        functools.partial(_tile_up_kernel, dff=dff, tf=tf),
        grid=(d // tr,),
        in_specs=[pl.BlockSpec((1, tr, two_dff), lambda r: (layer, r, 0))],
        out_specs=pl.BlockSpec((nj, tr, 2 * tf), lambda r: (0, r, 0)),
        out_shape=jax.ShapeDtypeStruct((nj, d, 2 * tf), BF16),
        compiler_params=_params(("parallel",)),
        name="tile_w_up",
    )(w_up)


def _tile_down_kernel(w_ref, o_ref, *, dff):
    j = pl.program_id(0)
    tf = o_ref.shape[1]
    rows = lax.broadcasted_iota(jnp.int32, o_ref.shape[1:], 0) + j * tf
    o_ref[0] = jnp.where(rows < dff, w_ref[0], 0.0).astype(BF16)


def _tile_down(w_down, layer, tf):
    _, dff, d = w_down.shape
    nj = -(-dff // tf)
    return pl.pallas_call(
        functools.partial(_tile_down_kernel, dff=dff),
        grid=(nj,),
        in_specs=[pl.BlockSpec((1, tf, d), lambda j: (layer, j, 0))],
        out_specs=pl.BlockSpec((1, tf, d), lambda j: (j, 0, 0)),
        out_shape=jax.ShapeDtypeStruct((nj, tf, d), BF16),
        compiler_params=_params(("parallel",)),
        name="tile_w_down",
    )(w_down)


def _prep_layer(l, w_in, b_i, b_f, m_norm_w, s_conv_w, s_conv_b, dt_bias, A_log, D_skip, s_norm_w, w_out,
                ln1_g, ln1_b, w_up, f_conv_w, f_conv_b, w_down, ln2_g, ln2_b):
    mw = m_norm_w.shape[1]
    sw = s_norm_w.shape[1]
    cd = s_conv_w.shape[2]
    nh = dt_bias.shape[1]
    dff = w_down.shape[1]
    tf = FFN_TILE
    fp = -(-dff // tf) * tf
    o = 0
    cols = {}
    for name, size in (("qkv", 3 * mw), ("o", mw), ("i", M_HEADS), ("f", M_HEADS),
                       ("z", sw), ("xbc", cd), ("dt", nh)):
        cols[name] = (o, o + size)
        o += size
    wl = w_in[l]
    cut = lambda name: wl[:, cols[name][0]:cols[name][1]]
    p = {}
    wg = jnp.concatenate([cut("i"), cut("f"), cut("dt")], axis=1)
    p["w_in"] = [cut("qkv").astype(BF16), cut("o").astype(BF16), cut("z").astype(BF16), cut("xbc").astype(BF16),
                 jnp.pad(wg, ((0, 0), (0, LANES - wg.shape[1]))).astype(BF16)]
    gb = jnp.concatenate([b_i[l], b_f[l]])
    p["gate_bias"] = jnp.pad(gb, (0, LANES - gb.shape[0]))[None, :]
    p["m_norm_w"] = m_norm_w[l][None, :]
    p["s_conv_w"] = s_conv_w[l]
    p["s_conv_b"] = s_conv_b[l][None, :]
    hp = jnp.stack([dt_bias[l], A_log[l], D_skip[l]])
    p["head_params"] = jnp.pad(hp, ((0, SUBLANES - 3), (GATE_DT, LANES - GATE_DT - nh)))
    p["s_norm_w"] = s_norm_w[l][None, :]
    p["w_out"] = w_out[l].astype(BF16).reshape(2, mw, w_out.shape[2])
    p["ln1_g"] = ln1_g[l][None, :]
    p["ln1_b"] = ln1_b[l][None, :]
    padc = lambda a: jnp.pad(a, ((0, 0), (0, fp - dff)))
    p["w_up"] = _tile_up(w_up, l, tf)
    p["f_conv_w"] = _ffn_tiles(padc(f_conv_w[l][:, :dff]), padc(f_conv_w[l][:, dff:]), tf)
    p["f_conv_b"] = _ffn_tiles(padc(f_conv_b[l][None, :dff]), padc(f_conv_b[l][None, dff:]), tf)
    p["w_down"] = _tile_down(w_down, l, tf)
    p["ln2_g"] = ln2_g[l][None, :]
    p["ln2_b"] = ln2_b[l][None, :]
    p["dff"] = dff
    p["fp"] = fp
    return p


def _layer(xf, xb, p, init, prev, *, layer, depth, nseq, T, L, time_major_ffn, alpha):
    qkv, o_pre, z, xbc, gates = _in_proj(xb, p["w_in"], [BF16, BF16, BF16, BF16, F32], tm_target=1024)
    hm, C, n, m = _mlstm(qkv, o_pre, gates, p["gate_bias"], p["m_norm_w"],
                         None if init is None else init["mlstm"], None if prev is None else prev["mlstm"],
                         layer=layer, depth=depth, nseq=nseq, T=T, L=L)
    hs, S, sconv = _ssd(z, xbc, gates, p["s_conv_w"], p["s_conv_b"], p["head_params"], p["s_norm_w"],
                        None if init is None else init["ssd"], None if prev is None else prev["ssd"],
                        layer=layer, depth=depth, nseq=nseq, T=T, L=L)
    x1f, x1b = _outproj(hm, hs, p["w_out"], xf, p["ln1_g"], p["ln1_b"], alpha=alpha)
    d = xf.shape[1]
    dff = p["dff"]
    ffn_w = (p["w_up"], p["f_conv_w"], p["f_conv_b"], p["w_down"])
    h0 = None if init is None else init["ffn"]
    if time_major_ffn:
        tr = lambda a: a.reshape(nseq, T, d).transpose(1, 0, 2).reshape(nseq * T, d)
        x1f_t, x1b_t = tr(x1f), tr(x1b)
        tm = _largest_divisor(nseq * T, 256, (FFN_CONV - 1) * nseq)
        y_f, y_b, oh = _ffn(x1b_t, x1f_t, *ffn_w, h0, p["ln2_g"], p["ln2_b"],
                            nseq=1, tm=tm, sh=nseq, hr=(FFN_CONV - 1) * nseq, alpha=alpha)
        tb = lambda a: a.reshape(T, nseq, d).transpose(1, 0, 2).reshape(nseq * T, d)
        y_f, y_b = tb(y_f), tb(y_b)
        og, ov = _ffn_untile(oh, FFN_TILE)
        fconv = jnp.concatenate([og[0, :, :dff], ov[0, :, :dff]], axis=1)
        fconv = fconv.reshape(FFN_CONV - 1, nseq, 2 * dff).transpose(1, 0, 2)
    else:
        tm = _largest_divisor(T, 512, FFN_ROW_CHUNK)
        y_f, y_b, oh = _ffn(x1b, x1f, *ffn_w, h0, p["ln2_g"], p["ln2_b"],
                            nseq=nseq, tm=tm, sh=1, hr=SUBLANES, alpha=alpha)
        og, ov = _ffn_untile(oh, FFN_TILE)
        k = FFN_CONV - 1
        fconv = jnp.concatenate([og[:, SUBLANES - k:, :dff], ov[:, SUBLANES - k:, :dff]], axis=2)
    return y_f, y_b, dict(mlstm=(C, n, m), ssd=(S, sconv)), fconv


def kernel(x_prompt, x_sample, state_mlstm_C, state_mlstm_n, state_mlstm_m, state_ssm, state_ssm_conv,
           state_ffn_conv, w_in, mlstm_b_i, mlstm_b_f, mlstm_norm_w, ssm_conv_w, ssm_conv_b, ssm_dt_bias,
           ssm_A_log, ssm_D, ssm_norm_w, w_out, ln1_g, ln1_b, ffn_w_up, ffn_conv_w, ffn_conv_b,
           ffn_w_down, ln2_g, ln2_b):
    weights = (w_in, mlstm_b_i, mlstm_b_f, mlstm_norm_w, ssm_conv_w, ssm_conv_b, ssm_dt_bias, ssm_A_log,
               ssm_D, ssm_norm_w, w_out, ln1_g, ln1_b, ffn_w_up, ffn_conv_w, ffn_conv_b, ffn_w_down,
               ln2_g, ln2_b)
    depth = w_in.shape[0]
    alpha = float((2 * depth) ** 0.25)
    B, T, D = x_prompt.shape
    Bs, Ts, _ = x_sample.shape
    nh = ssm_dt_bias.shape[1]
    Lp = PROMPT_CHUNK if T % PROMPT_CHUNK == 0 else T
    Ls = PROMPT_CHUNK if Ts % PROMPT_CHUNK == 0 else Ts

    xpf = x_prompt.reshape(B * T, D)
    xpb = xpf.astype(BF16)
    xsf = x_sample.reshape(Bs * Ts, D)
    xsb = xsf.astype(BF16)
    m_in = jnp.pad(state_mlstm_m, ((0, 0), (0, 0), (GATE_F, LANES - GATE_F - M_HEADS)))[:, :, None, :]
    s_in = state_ssm.reshape(depth, Bs, nh * S_HEADDIM, S_STATE)
    p_prev, s_prev, p_fconv, s_fconv = None, None, [], []
    for l in range(depth):
        p = _prep_layer(l, *weights)
        fp, dff = p["fp"], p["dff"]
        xpf, xpb, p_prev, fc = _layer(xpf, xpb, p, None, p_prev, layer=l, depth=depth, nseq=B, T=T, L=Lp,
                                      time_major_ffn=False, alpha=alpha)
        p_fconv.append(fc)
        fc0 = state_ffn_conv[l].transpose(1, 0, 2).reshape((FFN_CONV - 1) * Bs, 2 * dff)
        padf = lambda a: jnp.pad(a, ((0, 0), (0, fp - dff)))
        init = dict(mlstm=(state_mlstm_C, state_mlstm_n, m_in), ssd=(s_in, state_ssm_conv),
                    ffn=_ffn_tiles(padf(fc0[:, :dff]), padf(fc0[:, dff:]), FFN_TILE)[None])
        xsf, xsb, s_prev, fc = _layer(xsf, xsb, p, init, s_prev, layer=l, depth=depth, nseq=Bs, T=Ts, L=Ls,
                                      time_major_ffn=True, alpha=alpha)
        s_fconv.append(fc)

    def unpack(st, nseq):
        C, n, m = st["mlstm"]
        S, sconv = st["ssd"]
        return (C, n, m[:, :, 0, GATE_F:GATE_F + M_HEADS], S.reshape(depth, nseq, nh, S_HEADDIM, S_STATE),
                sconv[:, :, SUBLANES - (S_CONV - 1):, :])

    pC, pn, pm, pS, psc = unpack(p_prev, B)
    sC, sn, sm, sS, ssc = unpack(s_prev, Bs)
    return (xpf.reshape(B, T, D), xsf.reshape(Bs, Ts, D), pC, pn, pm, pS, psc, jnp.stack(p_fconv),
            sC, sn, sm, sS, ssc, jnp.stack(s_fconv))
```

```python
import functools

import jax
import jax.numpy as jnp
from jax import lax
from jax.experimental import pallas as pl
from jax.experimental.pallas import tpu as pltpu

F32 = jnp.float32
BF16 = jnp.bfloat16
HIGHEST = lax.Precision.HIGHEST

M_HEADS = 4
S_HEADDIM = 64
S_GROUPS = 2
S_STATE = 128
S_CONV = 4
FFN_CONV = 3
LN_EPS = 1e-5
GN_EPS = 1e-6
NEG_BIG = -1e30

LANES = 128
SUBLANES = 8
MXU_DIM = 256
VMEM_LIMIT_BYTES = 52 * 1024 * 1024

GATE_I = 0
GATE_F = 4
GATE_DT = 8

PROMPT_CHUNK = 128
SEQS_PER_STEP = 4
MW_VREG_BUDGET = 32
IN_PROJ_TILE = 2 * MXU_DIM
FFN_TILE = 2 * MXU_DIM
FFN_ROW_CHUNK = 32


def _largest_divisor(n, target, mult):
    best = None
    for d in range(mult, min(n, target) + 1, mult):
        if n % d == 0:
            best = d
    if best is None:
        raise ValueError(f"no tile for {n=} {target=} {mult=}")
    return best


def _params(sem, flags=None):
    return pltpu.CompilerParams(dimension_semantics=sem, vmem_limit_bytes=VMEM_LIMIT_BYTES, flags=flags)


def _dot(a, b):
    return jnp.dot(a, b, preferred_element_type=F32)


def _dot_nt(a, b, precision=None):
    return lax.dot_general(a, b, (((1,), (1,)), ((), ())), precision=precision, preferred_element_type=F32)


def _dot_tn(a, b):
    return lax.dot_general(a, b, (((0,), (0,)), ((), ())), preferred_element_type=F32)


def _sigmoid(x):
    return 1.0 / (1.0 + jnp.exp(-x))


def _softplus(x):
    return jnp.maximum(x, 0.0) + jnp.log1p(jnp.exp(-jnp.abs(x)))


def _layer_norm(r, g, b):
    mu = jnp.mean(r, axis=-1, keepdims=True)
    d = r - mu
    var = jnp.mean(d * d, axis=-1, keepdims=True)
    return d * lax.rsqrt(var + LN_EPS) * g + b


def _in_proj_kernel(*refs, groups):
    x_ref = refs[0]
    ng = len(groups)
    w_refs = refs[1:1 + ng]
    o_refs = refs[1 + ng:1 + 2 * ng]
    j = pl.program_id(1)
    for (start, count), w_ref, o_ref in zip(groups, w_refs, o_refs):
        @pl.when(jnp.logical_and(j >= start, j < start + count))
        def _(w_ref=w_ref, o_ref=o_ref):
            o_ref[...] = _dot(x_ref[...], w_ref[...]).astype(o_ref.dtype)


def _in_proj(x, weights, out_dtypes, tm_target):
    m, k = x.shape
    tm = _largest_divisor(m, tm_target, 16)
    tiles, groups, start = [], [], 0
    for w in weights:
        n = w.shape[1]
        tn = _largest_divisor(n, IN_PROJ_TILE, LANES)
        tiles.append(tn)
        groups.append((start, n // tn))
        start += n // tn

    def clamp(s, c):
        return lambda i, j: (0, jnp.clip(j - s, 0, c - 1))

    def clamp_out(s, c):
        return lambda i, j: (i, jnp.clip(j - s, 0, c - 1))

    in_specs = [pl.BlockSpec((tm, k), lambda i, j: (i, 0))]
    in_specs += [pl.BlockSpec((k, tn), clamp(s, c)) for tn, (s, c) in zip(tiles, groups)]
    out_specs = [pl.BlockSpec((tm, tn), clamp_out(s, c)) for tn, (s, c) in zip(tiles, groups)]
    out_shape = [jax.ShapeDtypeStruct((m, w.shape[1]), dt) for w, dt in zip(weights, out_dtypes)]
    return pl.pallas_call(
        functools.partial(_in_proj_kernel, groups=tuple(groups)),
        grid=(m // tm, start),
        in_specs=in_specs,
        out_specs=out_specs,
        out_shape=out_shape,
        compiler_params=_params(("parallel", "arbitrary")),
        name="in_proj",
    )(x, *weights)


def _mlstm_kernel(*refs, L, dk, has_init, has_prev):
    q_ref, k_ref, v_ref, o_ref, g_ref, gb_ref, nw_ref = refs[:7]
    pos = 7
    if has_init:
        c0_ref, n0_ref, m0_ref = refs[pos:pos + 3]
        pos += 3
    if has_prev:
        pos += 3
    h_ref, c_out, n_out, m_out, c_s, n_s, m_s = refs[pos:pos + 7]
    c = pl.program_id(1)
    nc = pl.num_programs(1)

    @pl.when(c == 0)
    def _():
        if has_init:
            c_s[...] = c0_ref[0]
            n_s[...] = n0_ref[0]
            m_s[...] = m0_ref[0]
        else:
            c_s[...] = jnp.zeros_like(c_s)
            n_s[...] = jnp.zeros_like(n_s)
            m_s[...] = jnp.zeros_like(m_s)

    row = lax.broadcasted_iota(jnp.int32, (L, L), 0)
    col = lax.broadcasted_iota(jnp.int32, (L, L), 1)
    causal = col <= row
    tri = jnp.where(causal, 1.0, 0.0).astype(F32)
    lane = lax.broadcasted_iota(jnp.int32, (L, LANES), 1)
    sel = jnp.where(lax.broadcasted_iota(jnp.int32, (SUBLANES, LANES), 0)
                    == lax.broadcasted_iota(jnp.int32, (SUBLANES, LANES), 1), 1.0, 0.0).astype(F32)
    m_lane = lax.broadcasted_iota(jnp.int32, (1, LANES), 1)
    k_scale = dk ** -0.5

    nsq = q_ref.shape[0]
    pairs = [(s_i, h) for s_i in range(nsq) for h in range(M_HEADS)]
    hsl = lambda h: slice(h * dk, (h + 1) * dk)


    gs, bcums, rowss, m_vecs = [], [], [], []
    for s_i in range(nsq):
        g = g_ref[s_i] + gb_ref[...]
        lf = jnp.minimum(g, 0.0) - jnp.log1p(jnp.exp(-jnp.abs(g)))
        bcum = jnp.dot(tri, lf, precision=HIGHEST, preferred_element_type=F32)
        packed = jnp.where(lane < GATE_F, g, bcum)
        gs.append(g)
        bcums.append(bcum)
        rowss.append(_dot_nt(sel, packed, precision=HIGHEST))
        m_vecs.append(m_s[s_i])

    st = []
    for s_i, h in pairs:
        bc = bcums[s_i][:, GATE_F + h:GATE_F + h + 1]
        igr = rowss[s_i][GATE_I + h:GATE_I + h + 1, :]
        br = rowss[s_i][GATE_F + h:GATE_F + h + 1, :]
        m_h = m_vecs[s_i][:, GATE_F + h:GATE_F + h + 1]
        log_w = jnp.where(causal, bc - br + igr, NEG_BIG)
        log_inter = bc + m_h
        m_t = jnp.maximum(log_inter, jnp.max(log_w, axis=1, keepdims=True))
        st.append(dict(bc=bc, m_h=m_h, m_t=m_t, w_intra=jnp.exp(log_w - m_t), w_inter=jnp.exp(log_inter - m_t)))

    for (s_i, h), e in zip(pairs, st):
        q = q_ref[s_i, :, hsl(h)]
        kf = k_ref[s_i, :, hsl(h)].astype(F32) * k_scale
        v = v_ref[s_i, :, hsl(h)]
        s = _dot_nt(q, kf.astype(BF16)) * e["w_intra"]
        num = _dot(s.astype(BF16), v) + e["w_inter"] * _dot(q, c_s[s_i, h].astype(BF16))
        qn = jnp.sum(q.astype(F32) * n_s[s_i, h:h + 1, :], axis=1, keepdims=True)
        den = jnp.sum(s, axis=1, keepdims=True) + e["w_inter"] * qn
        e["hh"] = num / jnp.maximum(jnp.abs(den), jnp.exp(-e["m_t"]))

    m_news = list(m_vecs)
    for (s_i, h), e in zip(pairs, st):
        bc, m_h, m_t = e["bc"], e["m_h"], e["m_t"]
        igc = gs[s_i][:, GATE_I + h:GATE_I + h + 1]
        m_end = m_t[L - 1:L, :]
        b_end = bc[L - 1:L, :]
        w_end = jnp.exp(b_end - bc + igc - m_end)
        decay = jnp.exp(b_end + m_h - m_end)
        kw = k_ref[s_i, :, hsl(h)].astype(F32) * k_scale * w_end
        c_s[s_i, h] = decay * c_s[s_i, h] + _dot_tn(kw.astype(BF16), v_ref[s_i, :, hsl(h)])
        n_s[s_i, h:h + 1, :] = decay * n_s[s_i, h:h + 1, :] + jnp.sum(kw, axis=0, keepdims=True)
        m_news[s_i] = jnp.where(m_lane == GATE_F + h, m_end, m_news[s_i])
    for s_i in range(nsq):
        m_s[s_i] = m_news[s_i]

    for (s_i, h), e in zip(pairs, st):
        hh = e["hh"]
        mu = jnp.mean(hh, axis=1, keepdims=True)
        d = hh - mu
        var = jnp.mean(d * d, axis=1, keepdims=True)
        hn = d * lax.rsqrt(var + GN_EPS) * nw_ref[:, hsl(h)]
        h_ref[s_i, :, hsl(h)] = (hn * _sigmoid(o_ref[s_i, :, hsl(h)].astype(F32))).astype(h_ref.dtype)

    @pl.when(c == nc - 1)
    def _():
        c_out[0] = c_s[...]
        n_out[0] = n_s[...]
        m_out[0] = m_s[...]


def _mlstm(qkv, o_pre, gates, gate_bias, norm_w, init, prev, *, layer, depth, nseq, T, L):
    W = qkv.shape[1] // 3
    dk = W // M_HEADS
    nc = T // L
    G = _largest_divisor(nseq, SEQS_PER_STEP, 1)
    qkv, o_pre, gates = (a.reshape(nseq, T, a.shape[1]) for a in (qkv, o_pre, gates))
    kern = functools.partial(_mlstm_kernel, L=L, dk=dk, has_init=init is not None, has_prev=prev is not None)
    st_specs = [pl.BlockSpec((1, G, M_HEADS, dk, dk), lambda b, c: (layer, b, 0, 0, 0)),
                pl.BlockSpec((1, G, M_HEADS, dk), lambda b, c: (layer, b, 0, 0)),
                pl.BlockSpec((1, G, 1, LANES), lambda b, c: (layer, b, 0, 0))]
    in_specs = [pl.BlockSpec((G, L, W), lambda b, c: (b, c, 0)),
                pl.BlockSpec((G, L, W), lambda b, c: (b, c, 1)),
                pl.BlockSpec((G, L, W), lambda b, c: (b, c, 2)),
                pl.BlockSpec((G, L, W), lambda b, c: (b, c, 0)),
                pl.BlockSpec((G, L, LANES), lambda b, c: (b, c, 0)),
                pl.BlockSpec((1, LANES), lambda b, c: (0, 0)),
                pl.BlockSpec((1, W), lambda b, c: (0, 0))]
    args = [qkv, qkv, qkv, o_pre, gates, gate_bias, norm_w]
    if init is not None:
        in_specs += st_specs
        args += list(init)
    aliases = {}
    if prev is not None:
        aliases = {len(args) + t: 1 + t for t in range(3)}
        in_specs += [pl.BlockSpec(memory_space=pl.ANY)] * 3
        args += list(prev)
    h, C, n, m = pl.pallas_call(
        kern,
        grid=(nseq // G, nc),
        in_specs=in_specs,
        out_specs=[pl.BlockSpec((G, L, W), lambda b, c: (b, c, 0))] + st_specs,
        out_shape=[jax.ShapeDtypeStruct((nseq, T, W), BF16),
                   jax.ShapeDtypeStruct((depth, nseq, M_HEADS, dk, dk), F32),
                   jax.ShapeDtypeStruct((depth, nseq, M_HEADS, dk), F32),
                   jax.ShapeDtypeStruct((depth, nseq, 1, LANES), F32)],
        scratch_shapes=[pltpu.VMEM((G, M_HEADS, dk, dk), F32),
                        pltpu.VMEM((G, M_HEADS, dk), F32),
                        pltpu.VMEM((G, 1, LANES), F32)],
        input_output_aliases=aliases,
        compiler_params=_params(("parallel", "arbitrary")),
        name="mlstm",
    )(*args)
    return h.reshape(nseq * T, W), C, n, m


def _ssd_kernel(*refs, L, nheads, has_init, has_prev):
    z_ref, x_ref, g_ref, cw_ref, cb_ref, hp_ref, nw_ref = refs[:7]
    pos = 7
    if has_init:
        s0_ref, cv0_ref = refs[pos:pos + 2]
        pos += 2
    if has_prev:
        pos += 2
    y_ref, s_out, cv_out, s_s, xp_s, y_s = refs[pos:pos + 6]
    c = pl.program_id(1)
    nc = pl.num_programs(1)
    P = S_HEADDIM
    N = S_STATE
    hg = nheads // S_GROUPS
    W = nheads * P
    halo = SUBLANES
    kc = S_CONV

    G = z_ref.shape[0]

    @pl.when(c == 0)
    def _():
        xp_s[:, 0:halo, :] = jnp.zeros((G, halo, xp_s.shape[2]), F32)
        if has_init:
            s_s[...] = s0_ref[0]
            xp_s[:, halo - (kc - 1):halo, :] = cv0_ref[0]
        else:
            s_s[...] = jnp.zeros_like(s_s)

    hp = hp_ref[...]
    row = lax.broadcasted_iota(jnp.int32, (L, L), 0)
    col = lax.broadcasted_iota(jnp.int32, (L, L), 1)
    causal = col <= row
    tri = jnp.where(causal, 1.0, 0.0).astype(F32)
    sel = jnp.where(lax.broadcasted_iota(jnp.int32, (nheads, LANES), 0) + GATE_DT
                    == lax.broadcasted_iota(jnp.int32, (nheads, LANES), 1), 1.0, 0.0).astype(F32)
    gw = W // S_GROUPS

    seqs = range(G)
    heads = [(s_i, gi, hh) for s_i in seqs for gi in range(S_GROUPS) for hh in range(hg)]

    sq = []
    for s_i in seqs:
        xp_s[s_i, halo:halo + L, :] = x_ref[s_i].astype(F32)
        conv = cb_ref[...] + xp_s[s_i, halo:halo + L, :] * cw_ref[kc - 1:kc, :]
        for j in range(kc - 1):
            off = halo - (kc - 1) + j
            conv = conv + xp_s[s_i, off:off + L, :] * cw_ref[j:j + 1, :]
        tail = xp_s[s_i, L:L + halo, :]
        xp_s[s_i, 0:halo, :] = tail
        cv_out[0, s_i] = tail
        xbc = conv * _sigmoid(conv)
        dt = _softplus(g_ref[s_i] + hp[0:1, :])
        a = dt * (-jnp.exp(hp[1:2, :]))
        bcum = jnp.dot(tri, a, precision=HIGHEST, preferred_element_type=F32)
        b_last = bcum[L - 1:L, :]
        sq.append(dict(
            xs=xbc[:, 0:W], Bm=xbc[:, W:W + S_GROUPS * N], Cm=xbc[:, W + S_GROUPS * N:W + 2 * S_GROUPS * N],
            bcum=bcum, b_rows=_dot_nt(sel, bcum, precision=HIGHEST), dt_rows=_dot_nt(sel, dt, precision=HIGHEST),
            eb=jnp.exp(bcum), w_end=jnp.exp(b_last - bcum) * dt, dec_end=jnp.exp(b_last)))

    gq = {}
    for s_i in seqs:
        for gi in range(S_GROUPS):
            Bg = sq[s_i]["Bm"][:, gi * N:(gi + 1) * N].astype(BF16)
            Cg = sq[s_i]["Cm"][:, gi * N:(gi + 1) * N].astype(BF16)
            s_g = s_s[s_i, gi * hg * P:(gi + 1) * hg * P, :]
            gq[s_i, gi] = dict(Bg=Bg, cb=_dot_nt(Cg, Bg), y_inter=_dot_nt(Cg, s_g.astype(BF16)))

    mw_vregs = -(-L // (2 * SUBLANES)) * -(-L // LANES)
    batch = max(1, MW_VREG_BUDGET // mw_vregs)
    for b0 in range(0, len(heads), batch):
        mws = {}
        for s_i, gi, hh in heads[b0:b0 + batch]:
            idx = gi * hg + hh
            ln = GATE_DT + idx
            e = sq[s_i]
            decay = jnp.exp(jnp.where(causal, e["bcum"][:, ln:ln + 1] - e["b_rows"][idx:idx + 1, :], NEG_BIG))
            mws[s_i, gi, hh] = (gq[s_i, gi]["cb"] * decay * e["dt_rows"][idx:idx + 1, :]).astype(BF16)

        for s_i, gi, hh in heads[b0:b0 + batch]:
            idx = gi * hg + hh
            ln = GATE_DT + idx
            e = sq[s_i]
            xs_h = e["xs"][:, idx * P:(idx + 1) * P]
            y_s[s_i, :, idx * P:(idx + 1) * P] = (
                _dot(mws[s_i, gi, hh], xs_h.astype(BF16))
                + e["eb"][:, ln:ln + 1] * gq[s_i, gi]["y_inter"][:, hh * P:(hh + 1) * P]
                + hp[2:3, ln:ln + 1] * xs_h)

    for s_i in seqs:
        e = sq[s_i]
        for gi in range(S_GROUPS):
            xw = jnp.concatenate(
                [(e["xs"][:, (gi * hg + hh) * P:(gi * hg + hh + 1) * P]
                  * e["w_end"][:, GATE_DT + gi * hg + hh:GATE_DT + gi * hg + hh + 1]).astype(BF16)
                 for hh in range(hg)], axis=1)
            upd = _dot_tn(xw, gq[s_i, gi]["Bg"])
            for hh in range(hg):
                ln = GATE_DT + gi * hg + hh
                r0 = (gi * hg + hh) * P
                s_s[s_i, r0:r0 + P, :] = (e["dec_end"][:, ln:ln + 1] * s_s[s_i, r0:r0 + P, :]
                                          + upd[hh * P:(hh + 1) * P, :])

    for s_i in seqs:
        for gi in range(S_GROUPS):
            sl = slice(gi * gw, (gi + 1) * gw)
            z = z_ref[s_i, :, sl].astype(F32)
            gg = y_s[s_i, :, sl] * (z * _sigmoid(z))
            gg = gg * lax.rsqrt(jnp.mean(gg * gg, axis=1, keepdims=True) + GN_EPS)
            y_ref[s_i, :, sl] = (gg * nw_ref[:, sl]).astype(y_ref.dtype)

    @pl.when(c == nc - 1)
    def _():
        s_out[0] = s_s[...]


def _ssd(z, xbc, gates, conv_w, conv_b, head_params, norm_w, init, prev, *, layer, depth, nseq, T, L):
    W = z.shape[1]
    nheads = W // S_HEADDIM
    CD = xbc.shape[1]
    nc = T // L
    G = _largest_divisor(nseq, SEQS_PER_STEP, 1) if L < PROMPT_CHUNK else 1
    z, xbc, gates = (a.reshape(nseq, T, a.shape[1]) for a in (z, xbc, gates))
    kern = functools.partial(_ssd_kernel, L=L, nheads=nheads, has_init=init is not None, has_prev=prev is not None)
    s_spec = pl.BlockSpec((1, G, nheads * S_HEADDIM, S_STATE), lambda b, c: (layer, b, 0, 0))
    in_specs = [pl.BlockSpec((G, L, W), lambda b, c: (b, c, 0)),
                pl.BlockSpec((G, L, CD), lambda b, c: (b, c, 0)),
                pl.BlockSpec((G, L, LANES), lambda b, c: (b, c, 0)),
                pl.BlockSpec((S_CONV, CD), lambda b, c: (0, 0)),
                pl.BlockSpec((1, CD), lambda b, c: (0, 0)),
                pl.BlockSpec((SUBLANES, LANES), lambda b, c: (0, 0)),
                pl.BlockSpec((1, W), lambda b, c: (0, 0))]
    args = [z, xbc, gates, conv_w, conv_b, head_params, norm_w]
    if init is not None:
        in_specs += [s_spec, pl.BlockSpec((1, G, S_CONV - 1, CD), lambda b, c: (layer, b, 0, 0))]
        args += list(init)
    aliases = {}
    if prev is not None:
        aliases = {len(args) + t: 1 + t for t in range(2)}
        in_specs += [pl.BlockSpec(memory_space=pl.ANY)] * 2
        args += list(prev)
    y, S, cv = pl.pallas_call(
        kern,
        grid=(nseq // G, nc),
        in_specs=in_specs,
        out_specs=[pl.BlockSpec((G, L, W), lambda b, c: (b, c, 0)),
                   s_spec,
                   pl.BlockSpec((1, G, SUBLANES, CD), lambda b, c: (layer, b, 0, 0))],
        out_shape=[jax.ShapeDtypeStruct((nseq, T, W), BF16),
                   jax.ShapeDtypeStruct((depth, nseq, nheads * S_HEADDIM, S_STATE), F32),
                   jax.ShapeDtypeStruct((depth, nseq, SUBLANES, CD), F32)],
        scratch_shapes=[pltpu.VMEM((G, nheads * S_HEADDIM, S_STATE), F32),
                        pltpu.VMEM((G, SUBLANES + L, CD), F32),
                        pltpu.VMEM((G, L, W), F32)],
        input_output_aliases=aliases,
        compiler_params=_params(("parallel", "arbitrary")),
        name="ssd",
    )(*args)
    return y.reshape(nseq * T, W), S, cv


def _outproj_kernel(hm_ref, hs_ref, wt_ref, wb_ref, x_ref, g_ref, b_ref, y_ref, yb_ref, *, alpha):
    mix = _dot(hm_ref[...], wt_ref[0]) + _dot(hs_ref[...], wb_ref[0])
    y = _layer_norm(alpha * x_ref[...] + mix, g_ref[...], b_ref[...])
    y_ref[...] = y
    yb_ref[...] = y.astype(BF16)


def _outproj(hm, hs, w_out, x, ln_g, ln_b, *, alpha):
    m, d = x.shape
    w = hm.shape[1]
    tm = _largest_divisor(m, 512, 16)
    kern = functools.partial(_outproj_kernel, alpha=alpha)
    return pl.pallas_call(
        kern,
        grid=(m // tm,),
        in_specs=[pl.BlockSpec((tm, w), lambda i: (i, 0)),
                  pl.BlockSpec((tm, w), lambda i: (i, 0)),
                  pl.BlockSpec((1, w, d), lambda i: (0, 0, 0)),
                  pl.BlockSpec((1, w, d), lambda i: (1, 0, 0)),
                  pl.BlockSpec((tm, d), lambda i: (i, 0)),
                  pl.BlockSpec((1, d), lambda i: (0, 0)),
                  pl.BlockSpec((1, d), lambda i: (0, 0))],
        out_specs=[pl.BlockSpec((tm, d), lambda i: (i, 0)),
                   pl.BlockSpec((tm, d), lambda i: (i, 0))],
        out_shape=[jax.ShapeDtypeStruct((m, d), F32),
                   jax.ShapeDtypeStruct((m, d), BF16)],
        compiler_params=_params(("parallel",)),
        name="out_proj_ln",
    )(hm, hs, w_out, w_out, x, ln_g, ln_b)


def _ffn_kernel(*refs, sh, hr, tps, rc, alpha, has_init):
    xb_ref, xf_ref, wu_ref, cw_ref, cb_ref, wd_ref = refs[:6]
    pos = 6
    if has_init:
        h0_ref = refs[pos]
        pos += 1
    lg_ref, lb_ref, y_ref, yb_ref, oh_ref, acc, ext, act, hal = refs[pos:pos + 9]
    i = pl.program_id(0)
    j = pl.program_id(1)
    nj = pl.num_programs(1)
    tm = xb_ref.shape[0]
    tf = wd_ref.shape[1]
    first = (i % tps) == 0

    @pl.when(first)
    def _():
        if has_init:
            ext[0:hr, :] = h0_ref[0, 0]
        else:
            ext[0:hr, :] = jnp.zeros((hr, ext.shape[1]), F32)

    @pl.when(jnp.logical_not(first))
    def _():
        ext[0:hr, :] = hal[j]

    ext[hr:hr + tm, :] = _dot(xb_ref[...], wu_ref[0])
    last = ext[tm:tm + hr, :]
    hal[j] = last
    oh_ref[0, 0] = last

    for r0 in range(0, tm, rc):
        c = (ext[hr - 2 * sh + r0:hr - 2 * sh + r0 + rc, :] * cw_ref[0, 0:1, :]
             + ext[hr - sh + r0:hr - sh + r0 + rc, :] * cw_ref[0, 1:2, :]
             + ext[hr + r0:hr + r0 + rc, :] * cw_ref[0, 2:3, :] + cb_ref[0])
        cg = c[:, :tf]
        act[r0:r0 + rc, :] = (cg * _sigmoid(cg) * c[:, tf:]).astype(BF16)

    @pl.when(j == 0)
    def _():
        acc[...] = jnp.zeros_like(acc)

    acc[...] += _dot(act[...], wd_ref[0])

    @pl.when(j == nj - 1)
    def _():
        y = _layer_norm(alpha * xf_ref[...] + acc[...], lg_ref[...], lb_ref[...])
        y_ref[...] = y
        yb_ref[...] = y.astype(BF16)


def _ffn(xb, xf, w_u, cw, cb, w_d, h0, ln_g, ln_b, *, nseq, tm, sh, hr, alpha):
    m, d = xf.shape
    nj, tf = w_d.shape[0], w_d.shape[1]
    ni = m // tm
    tps = ni // nseq
    kern = functools.partial(_ffn_kernel, sh=sh, hr=hr, tps=tps, rc=FFN_ROW_CHUNK, alpha=alpha,
                             has_init=h0 is not None)
    in_specs = [pl.BlockSpec((tm, d), lambda i, j: (i, 0)),
                pl.BlockSpec((tm, d), lambda i, j: (i, 0)),
                pl.BlockSpec((1, d, 2 * tf), lambda i, j: (j, 0, 0)),
                pl.BlockSpec((1, FFN_CONV, 2 * tf), lambda i, j: (j, 0, 0)),
                pl.BlockSpec((1, 1, 2 * tf), lambda i, j: (j, 0, 0)),
                pl.BlockSpec((1, tf, d), lambda i, j: (j, 0, 0))]
    args = [xb, xf, w_u, cw, cb, w_d]
    if h0 is not None:
        in_specs.append(pl.BlockSpec((1, 1, hr, 2 * tf), lambda i, j: (i // tps, j, 0, 0)))
        args.append(h0)
    in_specs += [pl.BlockSpec((1, d), lambda i, j: (0, 0))] * 2
    args += [ln_g, ln_b]
    return pl.pallas_call(
        kern,
        grid=(ni, nj),
        in_specs=in_specs,
        out_specs=[pl.BlockSpec((tm, d), lambda i, j: (i, 0)),
                   pl.BlockSpec((tm, d), lambda i, j: (i, 0)),
                   pl.BlockSpec((1, 1, hr, 2 * tf),
                                lambda i, j: (i // tps, jnp.where(i % tps == tps - 1, j, 0), 0, 0))],
        out_shape=[jax.ShapeDtypeStruct((m, d), F32),
                   jax.ShapeDtypeStruct((m, d), BF16),
                   jax.ShapeDtypeStruct((nseq, nj, hr, 2 * tf), F32)],
        scratch_shapes=[pltpu.VMEM((tm, d), F32),
                        pltpu.VMEM((hr + tm, 2 * tf), F32),
                        pltpu.VMEM((tm, tf), BF16),
                        pltpu.VMEM((nj, hr, 2 * tf), F32)],
        compiler_params=_params(("arbitrary", "arbitrary")),
        name="conv_ffn_ln",
    )(*args)


def _ffn_tiles(g, v, tf):
    nj = g.shape[-1] // tf
    lead = g.shape[:-1]
    t = jnp.stack([g.reshape(lead + (nj, tf)), v.reshape(lead + (nj, tf))], axis=-2)
    return jnp.moveaxis(t.reshape(lead + (nj, 2 * tf)), -2, 0)


def _ffn_untile(t, tf):
    nj = t.shape[-3]
    t = jnp.moveaxis(t, -3, -2)
    lead = t.shape[:-2]
    return (t[..., :tf].reshape(lead + (nj * tf,)), t[..., tf:].reshape(lead + (nj * tf,)))


def _tile_up_kernel(w_ref, o_ref, *, dff, tf):
    nj = o_ref.shape[0]
    for j in range(nj):
        lo = j * tf
        width = min(tf, dff - lo)
        for half, base in ((0, 0), (1, dff)):
            o_ref[j, :, half * tf:half * tf + width] = w_ref[0, :, base + lo:base + lo + width].astype(BF16)
            if width < tf:
                o_ref[j, :, half * tf + width:(half + 1) * tf] = jnp.zeros((o_ref.shape[1], tf - width), BF16)


def _tile_up(w_up, layer, tf):
    _, d, two_dff = w_up.shape
    dff = two_dff // 2
    assert dff % LANES == 0
    nj = -(-dff // tf)
    tr = _largest_divisor(d, 128, 16)
    return pl.pallas_call(
        functools.partial(_tile_up_kernel, dff=dff, tf=tf),
        grid=(d // tr,),
        in_specs=[pl.BlockSpec((1, tr, two_dff), lambda r: (layer, r, 0))],
        out_specs=pl.BlockSpec((nj, tr, 2 * tf), lambda r: (0, r, 0)),
        out_shape=jax.ShapeDtypeStruct((nj, d, 2 * tf), BF16),
        compiler_params=_params(("parallel",)),
        name="tile_w_up",
    )(w_up)


def _tile_down_kernel(w_ref, o_ref, *, dff):
    j = pl.program_id(0)
    tf = o_ref.shape[1]
    rows = lax.broadcasted_iota(jnp.int32, o_ref.shape[1:], 0) + j * tf
    o_ref[0] = jnp.where(rows < dff, w_ref[0], 0.0).astype(BF16)


def _tile_down(w_down, layer, tf):
    _, dff, d = w_down.shape
    nj = -(-dff // tf)
    return pl.pallas_call(
        functools.partial(_tile_down_kernel, dff=dff),
        grid=(nj,),
        in_specs=[pl.BlockSpec((1, tf, d), lambda j: (layer, j, 0))],
        out_specs=pl.BlockSpec((1, tf, d), lambda j: (j, 0, 0)),
        out_shape=jax.ShapeDtypeStruct((nj, tf, d), BF16),
        compiler_params=_params(("parallel",)),
        name="tile_w_down",
    )(w_down)


def _prep_layer(l, w_in, b_i, b_f, m_norm_w, s_conv_w, s_conv_b, dt_bias, A_log, D_skip, s_norm_w, w_out,
                ln1_g, ln1_b, w_up, f_conv_w, f_conv_b, w_down, ln2_g, ln2_b):
    mw = m_norm_w.shape[1]
    sw = s_norm_w.shape[1]
    cd = s_conv_w.shape[2]
    nh = dt_bias.shape[1]
    dff = w_down.shape[1]
    tf = FFN_TILE
    fp = -(-dff // tf) * tf
    o = 0
    cols = {}
    for name, size in (("qkv", 3 * mw), ("o", mw), ("i", M_HEADS), ("f", M_HEADS),
                       ("z", sw), ("xbc", cd), ("dt", nh)):
        cols[name] = (o, o + size)
        o += size
    wl = w_in[l]
    cut = lambda name: wl[:, cols[name][0]:cols[name][1]]
    p = {}
    wg = jnp.concatenate([cut("i"), cut("f"), cut("dt")], axis=1)
    p["w_in"] = [cut("qkv").astype(BF16), cut("o").astype(BF16), cut("z").astype(BF16), cut("xbc").astype(BF16),
                 jnp.pad(wg, ((0, 0), (0, LANES - wg.shape[1]))).astype(BF16)]
    gb = jnp.concatenate([b_i[l], b_f[l]])
    p["gate_bias"] = jnp.pad(gb, (0, LANES - gb.shape[0]))[None, :]
    p["m_norm_w"] = m_norm_w[l][None, :]
    p["s_conv_w"] = s_conv_w[l]
    p["s_conv_b"] = s_conv_b[l][None, :]
    hp = jnp.stack([dt_bias[l], A_log[l], D_skip[l]])
    p["head_params"] = jnp.pad(hp, ((0, SUBLANES - 3), (GATE_DT, LANES - GATE_DT - nh)))
    p["s_norm_w"] = s_norm_w[l][None, :]
    p["w_out"] = w_out[l].astype(BF16).reshape(2, mw, w_out.shape[2])
    p["ln1_g"] = ln1_g[l][None, :]
    p["ln1_b"] = ln1_b[l][None, :]
    padc = lambda a: jnp.pad(a, ((0, 0), (0, fp - dff)))
    p["w_up"] = _tile_up(w_up, l, tf)
    p["f_conv_w"] = _ffn_tiles(padc(f_conv_w[l][:, :dff]), padc(f_conv_w[l][:, dff:]), tf)
    p["f_conv_b"] = _ffn_tiles(padc(f_conv_b[l][None, :dff]), padc(f_conv_b[l][None, dff:]), tf)
    p["w_down"] = _tile_down(w_down, l, tf)
    p["ln2_g"] = ln2_g[l][None, :]
    p["ln2_b"] = ln2_b[l][None, :]
    p["dff"] = dff
    p["fp"] = fp
    return p


def _layer(xf, xb, p, init, prev, *, layer, depth, nseq, T, L, time_major_ffn, alpha):
    qkv, o_pre, z, xbc, gates = _in_proj(xb, p["w_in"], [BF16, BF16, BF16, BF16, F32], tm_target=1024)
    hm, C, n, m = _mlstm(qkv, o_pre, gates, p["gate_bias"], p["m_norm_w"],
                         None if init is None else init["mlstm"], None if prev is None else prev["mlstm"],
                         layer=layer, depth=depth, nseq=nseq, T=T, L=L)
    hs, S, sconv = _ssd(z, xbc, gates, p["s_conv_w"], p["s_conv_b"], p["head_params"], p["s_norm_w"],
                        None if init is None else init["ssd"], None if prev is None else prev["ssd"],
                        layer=layer, depth=depth, nseq=nseq, T=T, L=L)
    x1f, x1b = _outproj(hm, hs, p["w_out"], xf, p["ln1_g"], p["ln1_b"], alpha=alpha)
    d = xf.shape[1]
    dff = p["dff"]
    ffn_w = (p["w_up"], p["f_conv_w"], p["f_conv_b"], p["w_down"])
    h0 = None if init is None else init["ffn"]
    if time_major_ffn:
        tr = lambda a: a.reshape(nseq, T, d).transpose(1, 0, 2).reshape(nseq * T, d)
        x1f_t, x1b_t = tr(x1f), tr(x1b)
        tm = _largest_divisor(nseq * T, 256, (FFN_CONV - 1) * nseq)
        y_f, y_b, oh = _ffn(x1b_t, x1f_t, *ffn_w, h0, p["ln2_g"], p["ln2_b"],
                            nseq=1, tm=tm, sh=nseq, hr=(FFN_CONV - 1) * nseq, alpha=alpha)
        tb = lambda a: a.reshape(T, nseq, d).transpose(1, 0, 2).reshape(nseq * T, d)
        y_f, y_b = tb(y_f), tb(y_b)
        og, ov = _ffn_untile(oh, FFN_TILE)
        fconv = jnp.concatenate([og[0, :, :dff], ov[0, :, :dff]], axis=1)
        fconv = fconv.reshape(FFN_CONV - 1, nseq, 2 * dff).transpose(1, 0, 2)
    else:
        tm = _largest_divisor(T, 512, FFN_ROW_CHUNK)
        y_f, y_b, oh = _ffn(x1b, x1f, *ffn_w, h0, p["ln2_g"], p["ln2_b"],
                            nseq=nseq, tm=tm, sh=1, hr=SUBLANES, alpha=alpha)
        og, ov = _ffn_untile(oh, FFN_TILE)
        k = FFN_CONV - 1
        fconv = jnp.concatenate([og[:, SUBLANES - k:, :dff], ov[:, SUBLANES - k:, :dff]], axis=2)
    return y_f, y_b, dict(mlstm=(C, n, m), ssd=(S, sconv)), fconv


def kernel(x_prompt, x_sample, state_mlstm_C, state_mlstm_n, state_mlstm_m, state_ssm, state_ssm_conv,
           state_ffn_conv, w_in, mlstm_b_i, mlstm_b_f, mlstm_norm_w, ssm_conv_w, ssm_conv_b, ssm_dt_bias,
           ssm_A_log, ssm_D, ssm_norm_w, w_out, ln1_g, ln1_b, ffn_w_up, ffn_conv_w, ffn_conv_b,
           ffn_w_down, ln2_g, ln2_b):
    weights = (w_in, mlstm_b_i, mlstm_b_f, mlstm_norm_w, ssm_conv_w, ssm_conv_b, ssm_dt_bias, ssm_A_log,
               ssm_D, ssm_norm_w, w_out, ln1_g, ln1_b, ffn_w_up, ffn_conv_w, ffn_conv_b, ffn_w_down,
               ln2_g, ln2_b)
    depth = w_in.shape[0]
    alpha = float((2 * depth) ** 0.25)
    B, T, D = x_prompt.shape
    Bs, Ts, _ = x_sample.shape
    nh = ssm_dt_bias.shape[1]
    Lp = PROMPT_CHUNK if T % PROMPT_CHUNK == 0 else T
    Ls = PROMPT_CHUNK if Ts % PROMPT_CHUNK == 0 else Ts

    xpf = x_prompt.reshape(B * T, D)
    xpb = xpf.astype(BF16)
    xsf = x_sample.reshape(Bs * Ts, D)
    xsb = xsf.astype(BF16)
    m_in = jnp.pad(state_mlstm_m, ((0, 0), (0, 0), (GATE_F, LANES - GATE_F - M_HEADS)))[:, :, None, :]
    s_in = state_ssm.reshape(depth, Bs, nh * S_HEADDIM, S_STATE)
    p_prev, s_prev, p_fconv, s_fconv = None, None, [], []
    for l in range(depth):
        p = _prep_layer(l, *weights)
        fp, dff = p["fp"], p["dff"]
        xpf, xpb, p_prev, fc = _layer(xpf, xpb, p, None, p_prev, layer=l, depth=depth, nseq=B, T=T, L=Lp,
                                      time_major_ffn=False, alpha=alpha)
        p_fconv.append(fc)
        fc0 = state_ffn_conv[l].transpose(1, 0, 2).reshape((FFN_CONV - 1) * Bs, 2 * dff)
        padf = lambda a: jnp.pad(a, ((0, 0), (0, fp - dff)))
        init = dict(mlstm=(state_mlstm_C, state_mlstm_n, m_in), ssd=(s_in, state_ssm_conv),
                    ffn=_ffn_tiles(padf(fc0[:, :dff]), padf(fc0[:, dff:]), FFN_TILE)[None])
        xsf, xsb, s_prev, fc = _layer(xsf, xsb, p, init, s_prev, layer=l, depth=depth, nseq=Bs, T=Ts, L=Ls,
                                      time_major_ffn=True, alpha=alpha)
        s_fconv.append(fc)

    def unpack(st, nseq):
        C, n, m = st["mlstm"]
        S, sconv = st["ssd"]
        return (C, n, m[:, :, 0, GATE_F:GATE_F + M_HEADS], S.reshape(depth, nseq, nh, S_HEADDIM, S_STATE),
                sconv[:, :, SUBLANES - (S_CONV - 1):, :])

    pC, pn, pm, pS, psc = unpack(p_prev, B)
    sC, sn, sm, sS, ssc = unpack(s_prev, Bs)
    return (xpf.reshape(B, T, D), xsf.reshape(Bs, Ts, D), pC, pn, pm, pS, psc, jnp.stack(p_fconv),
            sC, sn, sm, sS, ssc, jnp.stack(s_fconv))
```

```python
import functools

import jax
import jax.numpy as jnp
from jax import lax
from jax.experimental import pallas as pl
from jax.experimental.pallas import tpu as pltpu

F32 = jnp.float32
BF16 = jnp.bfloat16
HIGHEST = lax.Precision.HIGHEST

M_HEADS = 4
S_HEADDIM = 64
S_GROUPS = 2
S_STATE = 128
S_CONV = 4
FFN_CONV = 3
LN_EPS = 1e-5
GN_EPS = 1e-6
NEG_BIG = -1e30

LANES = 128
SUBLANES = 8
MXU_DIM = 256
VMEM_LIMIT_BYTES = 52 * 1024 * 1024

GATE_I = 0
GATE_F = 4
GATE_DT = 8

PROMPT_CHUNK = 128
SEQS_PER_STEP = 4
VREG_FILE = 64
IN_PROJ_TILE = 2 * MXU_DIM
FFN_TILE = 2 * MXU_DIM
FFN_ROW_CHUNK = 32
FFN_COMPILER_SCRATCH_BYTES = 8 * 1024 * 1024


def _largest_divisor(n, target, mult):
    best = None
    for d in range(mult, min(n, target) + 1, mult):
        if n % d == 0:
            best = d
    if best is None:
        raise ValueError(f"no tile for {n=} {target=} {mult=}")
    return best


def _params(sem, vmem_limit_bytes=VMEM_LIMIT_BYTES):
    return pltpu.CompilerParams(dimension_semantics=sem, vmem_limit_bytes=vmem_limit_bytes)


def _dot(a, b):
    return jnp.dot(a, b, preferred_element_type=F32)


def _dot_nt(a, b, precision=None):
    return lax.dot_general(a, b, (((1,), (1,)), ((), ())), precision=precision, preferred_element_type=F32)


def _dot_tn(a, b):
    return lax.dot_general(a, b, (((0,), (0,)), ((), ())), preferred_element_type=F32)


def _sigmoid(x):
    return 1.0 / (1.0 + jnp.exp(-x))


def _softplus(x):
    return jnp.maximum(x, 0.0) + jnp.log1p(jnp.exp(-jnp.abs(x)))


def _layer_norm(r, g, b):
    mu = jnp.mean(r, axis=-1, keepdims=True)
    d = r - mu
    var = jnp.mean(d * d, axis=-1, keepdims=True)
    return d * lax.rsqrt(var + LN_EPS) * g + b


def _in_proj_kernel(*refs, groups):
    x_ref = refs[0]
    ng = len(groups)
    w_refs = refs[1:1 + ng]
    o_refs = refs[1 + ng:1 + 2 * ng]
    j = pl.program_id(1)
    for (start, count), w_ref, o_ref in zip(groups, w_refs, o_refs):
        @pl.when(jnp.logical_and(j >= start, j < start + count))
        def _(w_ref=w_ref, o_ref=o_ref):
            o_ref[...] = _dot(x_ref[...], w_ref[...]).astype(o_ref.dtype)


def _in_proj(x, weights, out_dtypes, tm_target):
    m, k = x.shape
    tm = _largest_divisor(m, tm_target, 16)
    tiles, groups, start = [], [], 0
    for w in weights:
        n = w.shape[1]
        tn = _largest_divisor(n, IN_PROJ_TILE, LANES)
        tiles.append(tn)
        groups.append((start, n // tn))
        start += n // tn

    def clamp(s, c):
        return lambda i, j: (0, jnp.clip(j - s, 0, c - 1))

    def clamp_out(s, c):
        return lambda i, j: (i, jnp.clip(j - s, 0, c - 1))

    in_specs = [pl.BlockSpec((tm, k), lambda i, j: (i, 0))]
    in_specs += [pl.BlockSpec((k, tn), clamp(s, c)) for tn, (s, c) in zip(tiles, groups)]
    out_specs = [pl.BlockSpec((tm, tn), clamp_out(s, c)) for tn, (s, c) in zip(tiles, groups)]
    out_shape = [jax.ShapeDtypeStruct((m, w.shape[1]), dt) for w, dt in zip(weights, out_dtypes)]
    return pl.pallas_call(
        functools.partial(_in_proj_kernel, groups=tuple(groups)),
        grid=(m // tm, start),
        in_specs=in_specs,
        out_specs=out_specs,
        out_shape=out_shape,
        compiler_params=_params(("parallel", "arbitrary")),
        name="in_proj",
    )(x, *weights)


def _mlstm_kernel(*refs, L, dk, has_init, has_prev):
    q_ref, k_ref, v_ref, o_ref, g_ref, gb_ref, nw_ref = refs[:7]
    pos = 7
    if has_init:
        c0_ref, n0_ref, m0_ref = refs[pos:pos + 3]
        pos += 3
    if has_prev:
        pos += 3
    h_ref, c_out, n_out, m_out, c_s, n_s, m_s = refs[pos:pos + 7]
    c = pl.program_id(1)
    nc = pl.num_programs(1)

    @pl.when(c == 0)
    def _():
        if has_init:
            c_s[...] = c0_ref[0]
            n_s[...] = n0_ref[0]
            m_s[...] = m0_ref[0]
        else:
            c_s[...] = jnp.zeros_like(c_s)
            n_s[...] = jnp.zeros_like(n_s)
            m_s[...] = jnp.zeros_like(m_s)

    row = lax.broadcasted_iota(jnp.int32, (L, L), 0)
    col = lax.broadcasted_iota(jnp.int32, (L, L), 1)
    causal = col <= row
    tri = jnp.where(causal, 1.0, 0.0).astype(F32)
    lane = lax.broadcasted_iota(jnp.int32, (L, LANES), 1)
    sel = jnp.where(lax.broadcasted_iota(jnp.int32, (SUBLANES, LANES), 0)
                    == lax.broadcasted_iota(jnp.int32, (SUBLANES, LANES), 1), 1.0, 0.0).astype(F32)
    m_lane = lax.broadcasted_iota(jnp.int32, (1, LANES), 1)
    k_scale = dk ** -0.5

    nsq = q_ref.shape[0]
    pairs = [(s_i, h) for s_i in range(nsq) for h in range(M_HEADS)]
    hsl = lambda h: slice(h * dk, (h + 1) * dk)


    gs, bcums, rowss, m_vecs = [], [], [], []
    for s_i in range(nsq):
        g = g_ref[s_i] + gb_ref[...]
        lf = jnp.minimum(g, 0.0) - jnp.log1p(jnp.exp(-jnp.abs(g)))
        bcum = jnp.dot(tri, lf, precision=HIGHEST, preferred_element_type=F32)
        packed = jnp.where(lane < GATE_F, g, bcum)
        gs.append(g)
        bcums.append(bcum)
        rowss.append(_dot_nt(sel, packed, precision=HIGHEST))
        m_vecs.append(m_s[s_i])

    st = []
    for s_i, h in pairs:
        bc = bcums[s_i][:, GATE_F + h:GATE_F + h + 1]
        igr = rowss[s_i][GATE_I + h:GATE_I + h + 1, :]
        br = rowss[s_i][GATE_F + h:GATE_F + h + 1, :]
        m_h = m_vecs[s_i][:, GATE_F + h:GATE_F + h + 1]
        log_w = jnp.where(causal, bc - br + igr, NEG_BIG)
        log_inter = bc + m_h
        m_t = jnp.maximum(log_inter, jnp.max(log_w, axis=1, keepdims=True))
        st.append(dict(bc=bc, m_h=m_h, m_t=m_t, w_intra=jnp.exp(log_w - m_t), w_inter=jnp.exp(log_inter - m_t)))

    for (s_i, h), e in zip(pairs, st):
        q = q_ref[s_i, :, hsl(h)]
        kf = k_ref[s_i, :, hsl(h)].astype(F32) * k_scale
        v = v_ref[s_i, :, hsl(h)]
        s = _dot_nt(q, kf.astype(BF16)) * e["w_intra"]
        num = _dot(s.astype(BF16), v) + e["w_inter"] * _dot(q, c_s[s_i, h].astype(BF16))
        qn = jnp.sum(q.astype(F32) * n_s[s_i, h:h + 1, :], axis=1, keepdims=True)
        den = jnp.sum(s, axis=1, keepdims=True) + e["w_inter"] * qn
        e["hh"] = num / jnp.maximum(jnp.abs(den), jnp.exp(-e["m_t"]))

    m_news = list(m_vecs)
    for (s_i, h), e in zip(pairs, st):
        bc, m_h, m_t = e["bc"], e["m_h"], e["m_t"]
        igc = gs[s_i][:, GATE_I + h:GATE_I + h + 1]
        m_end = m_t[L - 1:L, :]
        b_end = bc[L - 1:L, :]
        w_end = jnp.exp(b_end - bc + igc - m_end)
        decay = jnp.exp(b_end + m_h - m_end)
        kw = k_ref[s_i, :, hsl(h)].astype(F32) * k_scale * w_end
        c_s[s_i, h] = decay * c_s[s_i, h] + _dot_tn(kw.astype(BF16), v_ref[s_i, :, hsl(h)])
        n_s[s_i, h:h + 1, :] = decay * n_s[s_i, h:h + 1, :] + jnp.sum(kw, axis=0, keepdims=True)
        m_news[s_i] = jnp.where(m_lane == GATE_F + h, m_end, m_news[s_i])
    for s_i in range(nsq):
        m_s[s_i] = m_news[s_i]

    for (s_i, h), e in zip(pairs, st):
        hh = e["hh"]
        mu = jnp.mean(hh, axis=1, keepdims=True)
        d = hh - mu
        var = jnp.mean(d * d, axis=1, keepdims=True)
        hn = d * lax.rsqrt(var + GN_EPS) * nw_ref[:, hsl(h)]
        h_ref[s_i, :, hsl(h)] = (hn * _sigmoid(o_ref[s_i, :, hsl(h)].astype(F32))).astype(h_ref.dtype)

    @pl.when(c == nc - 1)
    def _():
        c_out[0] = c_s[...]
        n_out[0] = n_s[...]
        m_out[0] = m_s[...]


def _mlstm(qkv, o_pre, gates, gate_bias, norm_w, init, prev, *, layer, depth, nseq, T, L):
    W = qkv.shape[1] // 3
    dk = W // M_HEADS
    nc = T // L
    G = _largest_divisor(nseq, SEQS_PER_STEP, 1)
    qkv, o_pre, gates = (a.reshape(nseq, T, a.shape[1]) for a in (qkv, o_pre, gates))
    kern = functools.partial(_mlstm_kernel, L=L, dk=dk, has_init=init is not None, has_prev=prev is not None)
    st_specs = [pl.BlockSpec((1, G, M_HEADS, dk, dk), lambda b, c: (layer, b, 0, 0, 0)),
                pl.BlockSpec((1, G, M_HEADS, dk), lambda b, c: (layer, b, 0, 0)),
                pl.BlockSpec((1, G, 1, LANES), lambda b, c: (layer, b, 0, 0))]
    in_specs = [pl.BlockSpec((G, L, W), lambda b, c: (b, c, 0)),
                pl.BlockSpec((G, L, W), lambda b, c: (b, c, 1)),
                pl.BlockSpec((G, L, W), lambda b, c: (b, c, 2)),
                pl.BlockSpec((G, L, W), lambda b, c: (b, c, 0)),
                pl.BlockSpec((G, L, LANES), lambda b, c: (b, c, 0)),
                pl.BlockSpec((1, LANES), lambda b, c: (0, 0)),
                pl.BlockSpec((1, W), lambda b, c: (0, 0))]
    args = [qkv, qkv, qkv, o_pre, gates, gate_bias, norm_w]
    if init is not None:
        in_specs += st_specs
        args += list(init)
    aliases = {}
    if prev is not None:
        aliases = {len(args) + t: 1 + t for t in range(3)}
        in_specs += [pl.BlockSpec(memory_space=pl.ANY)] * 3
        args += list(prev)
    h, C, n, m = pl.pallas_call(
        kern,
        grid=(nseq // G, nc),
        in_specs=in_specs,
        out_specs=[pl.BlockSpec((G, L, W), lambda b, c: (b, c, 0))] + st_specs,
        out_shape=[jax.ShapeDtypeStruct((nseq, T, W), BF16),
                   jax.ShapeDtypeStruct((depth, nseq, M_HEADS, dk, dk), F32),
                   jax.ShapeDtypeStruct((depth, nseq, M_HEADS, dk), F32),
                   jax.ShapeDtypeStruct((depth, nseq, 1, LANES), F32)],
        scratch_shapes=[pltpu.VMEM((G, M_HEADS, dk, dk), F32),
                        pltpu.VMEM((G, M_HEADS, dk), F32),
                        pltpu.VMEM((G, 1, LANES), F32)],
        input_output_aliases=aliases,
        compiler_params=_params(("parallel", "arbitrary")),
        name="mlstm",
    )(*args)
    return h.reshape(nseq * T, W), C, n, m


def _ssd_kernel(*refs, L, nheads, has_init, has_prev):
    z_ref, x_ref, g_ref, cw_ref, cb_ref, hp_ref, nw_ref = refs[:7]
    pos = 7
    if has_init:
        s0_ref, cv0_ref = refs[pos:pos + 2]
        pos += 2
    if has_prev:
        pos += 2
    y_ref, s_out, cv_out, s_s, xp_s, y_s = refs[pos:pos + 6]
    c = pl.program_id(1)
    nc = pl.num_programs(1)
    P = S_HEADDIM
    N = S_STATE
    hg = nheads // S_GROUPS
    W = nheads * P
    halo = SUBLANES
    kc = S_CONV

    G = z_ref.shape[0]

    @pl.when(c == 0)
    def _():
        xp_s[:, 0:halo, :] = jnp.zeros((G, halo, xp_s.shape[2]), F32)
        if has_init:
            s_s[...] = s0_ref[0]
            xp_s[:, halo - (kc - 1):halo, :] = cv0_ref[0]
        else:
            s_s[...] = jnp.zeros_like(s_s)

    hp = hp_ref[...]
    row = lax.broadcasted_iota(jnp.int32, (L, L), 0)
    col = lax.broadcasted_iota(jnp.int32, (L, L), 1)
    causal = col <= row
    tri = jnp.where(causal, 1.0, 0.0).astype(F32)
    sel = jnp.where(lax.broadcasted_iota(jnp.int32, (nheads, LANES), 0) + GATE_DT
                    == lax.broadcasted_iota(jnp.int32, (nheads, LANES), 1), 1.0, 0.0).astype(F32)
    gw = W // S_GROUPS

    seqs = range(G)
    heads = [(s_i, gi, hh) for s_i in seqs for gi in range(S_GROUPS) for hh in range(hg)]

    sq = []
    for s_i in seqs:
        xp_s[s_i, halo:halo + L, :] = x_ref[s_i].astype(F32)
        conv = cb_ref[...] + xp_s[s_i, halo:halo + L, :] * cw_ref[kc - 1:kc, :]
        for j in range(kc - 1):
            off = halo - (kc - 1) + j
            conv = conv + xp_s[s_i, off:off + L, :] * cw_ref[j:j + 1, :]
        tail = xp_s[s_i, L:L + halo, :]
        xp_s[s_i, 0:halo, :] = tail
        cv_out[0, s_i] = tail
        xbc = conv * _sigmoid(conv)
        dt = _softplus(g_ref[s_i] + hp[0:1, :])
        a = dt * (-jnp.exp(hp[1:2, :]))
        bcum = jnp.dot(tri, a, precision=HIGHEST, preferred_element_type=F32)
        b_last = bcum[L - 1:L, :]
        sq.append(dict(
            xs=xbc[:, 0:W], Bm=xbc[:, W:W + S_GROUPS * N], Cm=xbc[:, W + S_GROUPS * N:W + 2 * S_GROUPS * N],
            bcum=bcum, b_rows=_dot_nt(sel, bcum, precision=HIGHEST), dt_rows=_dot_nt(sel, dt, precision=HIGHEST),
            eb=jnp.exp(bcum), w_end=jnp.exp(b_last - bcum) * dt, dec_end=jnp.exp(b_last)))

    gq = {}
    for s_i in seqs:
        for gi in range(S_GROUPS):
            Bg = sq[s_i]["Bm"][:, gi * N:(gi + 1) * N].astype(BF16)
            Cg = sq[s_i]["Cm"][:, gi * N:(gi + 1) * N].astype(BF16)
            s_g = s_s[s_i, gi * hg * P:(gi + 1) * hg * P, :]
            gq[s_i, gi] = dict(Bg=Bg, cb=_dot_nt(Cg, Bg), y_inter=_dot_nt(Cg, s_g.astype(BF16)))

    mw_vregs = -(-L // (2 * SUBLANES)) * -(-L // LANES)
    batch = len(heads) if len(heads) * mw_vregs <= VREG_FILE else 1
    for b0 in range(0, len(heads), batch):
        mws = {}
        for s_i, gi, hh in heads[b0:b0 + batch]:
            idx = gi * hg + hh
            ln = GATE_DT + idx
            e = sq[s_i]
            decay = jnp.exp(jnp.where(causal, e["bcum"][:, ln:ln + 1] - e["b_rows"][idx:idx + 1, :], NEG_BIG))
            mws[s_i, gi, hh] = (gq[s_i, gi]["cb"] * decay * e["dt_rows"][idx:idx + 1, :]).astype(BF16)

        for s_i, gi, hh in heads[b0:b0 + batch]:
            idx = gi * hg + hh
            ln = GATE_DT + idx
            e = sq[s_i]
            xs_h = e["xs"][:, idx * P:(idx + 1) * P]
            y_s[s_i, :, idx * P:(idx + 1) * P] = (
                _dot(mws[s_i, gi, hh], xs_h.astype(BF16))
                + e["eb"][:, ln:ln + 1] * gq[s_i, gi]["y_inter"][:, hh * P:(hh + 1) * P]
                + hp[2:3, ln:ln + 1] * xs_h)

    for s_i in seqs:
        e = sq[s_i]
        for gi in range(S_GROUPS):
            xw = jnp.concatenate(
                [(e["xs"][:, (gi * hg + hh) * P:(gi * hg + hh + 1) * P]
                  * e["w_end"][:, GATE_DT + gi * hg + hh:GATE_DT + gi * hg + hh + 1]).astype(BF16)
                 for hh in range(hg)], axis=1)
            upd = _dot_tn(xw, gq[s_i, gi]["Bg"])
            for hh in range(hg):
                ln = GATE_DT + gi * hg + hh
                r0 = (gi * hg + hh) * P
                s_s[s_i, r0:r0 + P, :] = (e["dec_end"][:, ln:ln + 1] * s_s[s_i, r0:r0 + P, :]
                                          + upd[hh * P:(hh + 1) * P, :])

    for s_i in seqs:
        for gi in range(S_GROUPS):
            sl = slice(gi * gw, (gi + 1) * gw)
            z = z_ref[s_i, :, sl].astype(F32)
            gg = y_s[s_i, :, sl] * (z * _sigmoid(z))
            gg = gg * lax.rsqrt(jnp.mean(gg * gg, axis=1, keepdims=True) + GN_EPS)
            y_ref[s_i, :, sl] = (gg * nw_ref[:, sl]).astype(y_ref.dtype)

    @pl.when(c == nc - 1)
    def _():
        s_out[0] = s_s[...]


def _ssd(z, xbc, gates, conv_w, conv_b, head_params, norm_w, init, prev, *, layer, depth, nseq, T, L):
    W = z.shape[1]
    nheads = W // S_HEADDIM
    CD = xbc.shape[1]
    nc = T // L
    G = _largest_divisor(nseq, SEQS_PER_STEP, 1) if L < PROMPT_CHUNK else 1
    z, xbc, gates = (a.reshape(nseq, T, a.shape[1]) for a in (z, xbc, gates))
    kern = functools.partial(_ssd_kernel, L=L, nheads=nheads, has_init=init is not None, has_prev=prev is not None)
    s_spec = pl.BlockSpec((1, G, nheads * S_HEADDIM, S_STATE), lambda b, c: (layer, b, 0, 0))
    in_specs = [pl.BlockSpec((G, L, W), lambda b, c: (b, c, 0)),
                pl.BlockSpec((G, L, CD), lambda b, c: (b, c, 0)),
                pl.BlockSpec((G, L, LANES), lambda b, c: (b, c, 0)),
                pl.BlockSpec((S_CONV, CD), lambda b, c: (0, 0)),
                pl.BlockSpec((1, CD), lambda b, c: (0, 0)),
                pl.BlockSpec((SUBLANES, LANES), lambda b, c: (0, 0)),
                pl.BlockSpec((1, W), lambda b, c: (0, 0))]
    args = [z, xbc, gates, conv_w, conv_b, head_params, norm_w]
    if init is not None:
        in_specs += [s_spec, pl.BlockSpec((1, G, S_CONV - 1, CD), lambda b, c: (layer, b, 0, 0))]
        args += list(init)
    aliases = {}
    if prev is not None:
        aliases = {len(args) + t: 1 + t for t in range(2)}
        in_specs += [pl.BlockSpec(memory_space=pl.ANY)] * 2
        args += list(prev)
    y, S, cv = pl.pallas_call(
        kern,
        grid=(nseq // G, nc),
        in_specs=in_specs,
        out_specs=[pl.BlockSpec((G, L, W), lambda b, c: (b, c, 0)),
                   s_spec,
                   pl.BlockSpec((1, G, SUBLANES, CD), lambda b, c: (layer, b, 0, 0))],
        out_shape=[jax.ShapeDtypeStruct((nseq, T, W), BF16),
                   jax.ShapeDtypeStruct((depth, nseq, nheads * S_HEADDIM, S_STATE), F32),
                   jax.ShapeDtypeStruct((depth, nseq, SUBLANES, CD), F32)],
        scratch_shapes=[pltpu.VMEM((G, nheads * S_HEADDIM, S_STATE), F32),
                        pltpu.VMEM((G, SUBLANES + L, CD), F32),
                        pltpu.VMEM((G, L, W), F32)],
        input_output_aliases=aliases,
        compiler_params=_params(("parallel", "arbitrary")),
        name="ssd",
    )(*args)
    return y.reshape(nseq * T, W), S, cv


def _outproj_kernel(hm_ref, hs_ref, wt_ref, wb_ref, x_ref, g_ref, b_ref, y_ref, yb_ref, *, alpha):
    mix = _dot(hm_ref[...], wt_ref[0]) + _dot(hs_ref[...], wb_ref[0])
    y = _layer_norm(alpha * x_ref[...] + mix, g_ref[...], b_ref[...])
    y_ref[...] = y
    yb_ref[...] = y.astype(BF16)


def _outproj(hm, hs, w_out, x, ln_g, ln_b, *, alpha):
    m, d = x.shape
    w = hm.shape[1]
    tm = _largest_divisor(m, 512, 16)
    kern = functools.partial(_outproj_kernel, alpha=alpha)
    return pl.pallas_call(
        kern,
        grid=(m // tm,),
        in_specs=[pl.BlockSpec((tm, w), lambda i: (i, 0)),
                  pl.BlockSpec((tm, w), lambda i: (i, 0)),
                  pl.BlockSpec((1, w, d), lambda i: (0, 0, 0)),
                  pl.BlockSpec((1, w, d), lambda i: (1, 0, 0)),
                  pl.BlockSpec((tm, d), lambda i: (i, 0)),
                  pl.BlockSpec((1, d), lambda i: (0, 0)),
                  pl.BlockSpec((1, d), lambda i: (0, 0))],
        out_specs=[pl.BlockSpec((tm, d), lambda i: (i, 0)),
                   pl.BlockSpec((tm, d), lambda i: (i, 0))],
        out_shape=[jax.ShapeDtypeStruct((m, d), F32),
                   jax.ShapeDtypeStruct((m, d), BF16)],
        compiler_params=_params(("parallel",)),
        name="out_proj_ln",
    )(hm, hs, w_out, w_out, x, ln_g, ln_b)


def _ffn_kernel(*refs, sh, hr, tps, rc, alpha, has_init):
    xb_ref, xf_ref, wu_ref, cw_ref, cb_ref, wd_ref = refs[:6]
    pos = 6
    if has_init:
        h0_ref = refs[pos]
        pos += 1
    lg_ref, lb_ref, y_ref, yb_ref, oh_ref, acc, ext, act, hal = refs[pos:pos + 9]
    i = pl.program_id(0)
    j = pl.program_id(1)
    nj = pl.num_programs(1)
    tm = xb_ref.shape[0]
    tf = wd_ref.shape[1]
    first = (i % tps) == 0

    @pl.when(first)
    def _():
        if has_init:
            ext[0:hr, :] = h0_ref[0, 0]
        else:
            ext[0:hr, :] = jnp.zeros((hr, ext.shape[1]), F32)

    @pl.when(jnp.logical_not(first))
    def _():
        ext[0:hr, :] = hal[j]

    ext[hr:hr + tm, :] = _dot(xb_ref[...], wu_ref[0])
    last = ext[tm:tm + hr, :]
    hal[j] = last
    oh_ref[0, 0] = last

    for r0 in range(0, tm, rc):
        c = (ext[hr - 2 * sh + r0:hr - 2 * sh + r0 + rc, :] * cw_ref[0, 0:1, :]
             + ext[hr - sh + r0:hr - sh + r0 + rc, :] * cw_ref[0, 1:2, :]
             + ext[hr + r0:hr + r0 + rc, :] * cw_ref[0, 2:3, :] + cb_ref[0])
        cg = c[:, :tf]
        act[r0:r0 + rc, :] = (cg * _sigmoid(cg) * c[:, tf:]).astype(BF16)

    @pl.when(j == 0)
    def _():
        acc[...] = jnp.zeros_like(acc)

    acc[...] += _dot(act[...], wd_ref[0])

    @pl.when(j == nj - 1)
    def _():
        y = _layer_norm(alpha * xf_ref[...] + acc[...], lg_ref[...], lb_ref[...])
        y_ref[...] = y
        yb_ref[...] = y.astype(BF16)


def _ffn(xb, xf, w_u, cw, cb, w_d, h0, ln_g, ln_b, *, nseq, tm, sh, hr, alpha, single_buffer):
    m, d = xf.shape
    nj, tf = w_d.shape[0], w_d.shape[1]
    ni = m // tm
    tps = ni // nseq
    kern = functools.partial(_ffn_kernel, sh=sh, hr=hr, tps=tps, rc=FFN_ROW_CHUNK, alpha=alpha,
                             has_init=h0 is not None)
    mode = dict(pipeline_mode=pl.Buffered(1)) if single_buffer else {}
    nbuf = 1 if single_buffer else 2
    vmem_bytes = (nbuf * tm * d * (2 + 4 + 4 + 2)
                  + 2 * (d * 2 * tf * 2 + tf * d * 2)
                  + 2 * 2 * hr * 2 * tf * 4
                  + tm * d * 4 + (hr + tm) * 2 * tf * 4 + tm * tf * 2 + nj * hr * 2 * tf * 4
                  + FFN_COMPILER_SCRATCH_BYTES)
    in_specs = [pl.BlockSpec((tm, d), lambda i, j: (i, 0), **mode),
                pl.BlockSpec((tm, d), lambda i, j: (i, 0), **mode),
                pl.BlockSpec((1, d, 2 * tf), lambda i, j: (j, 0, 0)),
                pl.BlockSpec((1, FFN_CONV, 2 * tf), lambda i, j: (j, 0, 0)),
                pl.BlockSpec((1, 1, 2 * tf), lambda i, j: (j, 0, 0)),
                pl.BlockSpec((1, tf, d), lambda i, j: (j, 0, 0))]
    args = [xb, xf, w_u, cw, cb, w_d]
    if h0 is not None:
        in_specs.append(pl.BlockSpec((1, 1, hr, 2 * tf), lambda i, j: (i // tps, j, 0, 0)))
        args.append(h0)
    in_specs += [pl.BlockSpec((1, d), lambda i, j: (0, 0))] * 2
    args += [ln_g, ln_b]
    return pl.pallas_call(
        kern,
        grid=(ni, nj),
        in_specs=in_specs,
        out_specs=[pl.BlockSpec((tm, d), lambda i, j: (i, 0), **mode),
                   pl.BlockSpec((tm, d), lambda i, j: (i, 0), **mode),
                   pl.BlockSpec((1, 1, hr, 2 * tf),
                                lambda i, j: (i // tps, jnp.where(i % tps == tps - 1, j, 0), 0, 0))],
        out_shape=[jax.ShapeDtypeStruct((m, d), F32),
                   jax.ShapeDtypeStruct((m, d), BF16),
                   jax.ShapeDtypeStruct((nseq, nj, hr, 2 * tf), F32)],
        scratch_shapes=[pltpu.VMEM((tm, d), F32),
                        pltpu.VMEM((hr + tm, 2 * tf), F32),
                        pltpu.VMEM((tm, tf), BF16),
                        pltpu.VMEM((nj, hr, 2 * tf), F32)],
        compiler_params=_params(("arbitrary", "arbitrary"), vmem_limit_bytes=vmem_bytes),
        name="conv_ffn_ln",
    )(*args)


def _ffn_tiles(g, v, tf):
    nj = g.shape[-1] // tf
    lead = g.shape[:-1]
    t = jnp.stack([g.reshape(lead + (nj, tf)), v.reshape(lead + (nj, tf))], axis=-2)
    return jnp.moveaxis(t.reshape(lead + (nj, 2 * tf)), -2, 0)


def _ffn_untile(t, tf):
    nj = t.shape[-3]
    t = jnp.moveaxis(t, -3, -2)
    lead = t.shape[:-2]
    return (t[..., :tf].reshape(lead + (nj * tf,)), t[..., tf:].reshape(lead + (nj * tf,)))


def _tile_up_kernel(w_ref, o_ref, *, dff, tf):
    nj = o_ref.shape[0]
    for j in range(nj):
        lo = j * tf
        width = min(tf, dff - lo)
        for half, base in ((0, 0), (1, dff)):
            o_ref[j, :, half * tf:half * tf + width] = w_ref[0, :, base + lo:base + lo + width].astype(BF16)
            if width < tf:
                o_ref[j, :, half * tf + width:(half + 1) * tf] = jnp.zeros((o_ref.shape[1], tf - width), BF16)


def _tile_up(w_up, layer, tf):
    _, d, two_dff = w_up.shape
    dff = two_dff // 2
    assert dff % LANES == 0
    nj = -(-dff // tf)
    tr = _largest_divisor(d, 128, 16)
    return pl.pallas_call(
        functools.partial(_tile_up_kernel, dff=dff, tf=tf),
        grid=(d // tr,),
        in_specs=[pl.BlockSpec((1, tr, two_dff), lambda r: (layer, r, 0))],
        out_specs=pl.BlockSpec((nj, tr, 2 * tf), lambda r: (0, r, 0)),
        out_shape=jax.ShapeDtypeStruct((nj, d, 2 * tf), BF16),
        compiler_params=_params(("parallel",)),
        name="tile_w_up",
    )(w_up)


def _tile_down_kernel(w_ref, o_ref, *, dff):
    j = pl.program_id(0)
    tf = o_ref.shape[1]
    rows = lax.broadcasted_iota(jnp.int32, o_ref.shape[1:], 0) + j * tf
    o_ref[0] = jnp.where(rows < dff, w_ref[0], 0.0).astype(BF16)


def _tile_down(w_down, layer, tf):
    _, dff, d = w_down.shape
    nj = -(-dff // tf)
    return pl.pallas_call(
        functools.partial(_tile_down_kernel, dff=dff),
        grid=(nj,),
        in_specs=[pl.BlockSpec((1, tf, d), lambda j: (layer, j, 0))],
        out_specs=pl.BlockSpec((1, tf, d), lambda j: (j, 0, 0)),
        out_shape=jax.ShapeDtypeStruct((nj, tf, d), BF16),
        compiler_params=_params(("parallel",)),
        name="tile_w_down",
    )(w_down)


def _prep_layer(l, w_in, b_i, b_f, m_norm_w, s_conv_w, s_conv_b, dt_bias, A_log, D_skip, s_norm_w, w_out,
                ln1_g, ln1_b, w_up, f_conv_w, f_conv_b, w_down, ln2_g, ln2_b):
    mw = m_norm_w.shape[1]
    sw = s_norm_w.shape[1]
    cd = s_conv_w.shape[2]
    nh = dt_bias.shape[1]
    dff = w_down.shape[1]
    tf = FFN_TILE
    fp = -(-dff // tf) * tf
    o = 0
    cols = {}
    for name, size in (("qkv", 3 * mw), ("o", mw), ("i", M_HEADS), ("f", M_HEADS),
                       ("z", sw), ("xbc", cd), ("dt", nh)):
        cols[name] = (o, o + size)
        o += size
    wl = w_in[l]
    cut = lambda name: wl[:, cols[name][0]:cols[name][1]]
    p = {}
    wg = jnp.concatenate([cut("i"), cut("f"), cut("dt")], axis=1)
    p["w_in"] = [cut("qkv").astype(BF16), cut("o").astype(BF16), cut("z").astype(BF16), cut("xbc").astype(BF16),
                 jnp.pad(wg, ((0, 0), (0, LANES - wg.shape[1]))).astype(BF16)]
    gb = jnp.concatenate([b_i[l], b_f[l]])
    p["gate_bias"] = jnp.pad(gb, (0, LANES - gb.shape[0]))[None, :]
    p["m_norm_w"] = m_norm_w[l][None, :]
    p["s_conv_w"] = s_conv_w[l]
    p["s_conv_b"] = s_conv_b[l][None, :]
    hp = jnp.stack([dt_bias[l], A_log[l], D_skip[l]])
    p["head_params"] = jnp.pad(hp, ((0, SUBLANES - 3), (GATE_DT, LANES - GATE_DT - nh)))
    p["s_norm_w"] = s_norm_w[l][None, :]
    p["w_out"] = w_out[l].astype(BF16).reshape(2, mw, w_out.shape[2])
    p["ln1_g"] = ln1_g[l][None, :]
    p["ln1_b"] = ln1_b[l][None, :]
    padc = lambda a: jnp.pad(a, ((0, 0), (0, fp - dff)))
    p["w_up"] = _tile_up(w_up, l, tf)
    p["f_conv_w"] = _ffn_tiles(padc(f_conv_w[l][:, :dff]), padc(f_conv_w[l][:, dff:]), tf)
    p["f_conv_b"] = _ffn_tiles(padc(f_conv_b[l][None, :dff]), padc(f_conv_b[l][None, dff:]), tf)
    p["w_down"] = _tile_down(w_down, l, tf)
    p["ln2_g"] = ln2_g[l][None, :]
    p["ln2_b"] = ln2_b[l][None, :]
    p["dff"] = dff
    p["fp"] = fp
    return p


def _layer(xf, xb, p, init, prev, *, layer, depth, nseq, T, L, time_major_ffn, alpha):
    qkv, o_pre, z, xbc, gates = _in_proj(xb, p["w_in"], [BF16, BF16, BF16, BF16, F32], tm_target=1024)
    hm, C, n, m = _mlstm(qkv, o_pre, gates, p["gate_bias"], p["m_norm_w"],
                         None if init is None else init["mlstm"], None if prev is None else prev["mlstm"],
                         layer=layer, depth=depth, nseq=nseq, T=T, L=L)
    hs, S, sconv = _ssd(z, xbc, gates, p["s_conv_w"], p["s_conv_b"], p["head_params"], p["s_norm_w"],
                        None if init is None else init["ssd"], None if prev is None else prev["ssd"],
                        layer=layer, depth=depth, nseq=nseq, T=T, L=L)
    x1f, x1b = _outproj(hm, hs, p["w_out"], xf, p["ln1_g"], p["ln1_b"], alpha=alpha)
    d = xf.shape[1]
    dff = p["dff"]
    ffn_w = (p["w_up"], p["f_conv_w"], p["f_conv_b"], p["w_down"])
    h0 = None if init is None else init["ffn"]
    if time_major_ffn:
        tr = lambda a: a.reshape(nseq, T, d).transpose(1, 0, 2).reshape(nseq * T, d)
        x1f_t, x1b_t = tr(x1f), tr(x1b)
        tm = _largest_divisor(nseq * T, 512, (FFN_CONV - 1) * nseq)
        y_f, y_b, oh = _ffn(x1b_t, x1f_t, *ffn_w, h0, p["ln2_g"], p["ln2_b"], nseq=1, tm=tm, sh=nseq,
                            hr=(FFN_CONV - 1) * nseq, alpha=alpha, single_buffer=True)
        tb = lambda a: a.reshape(T, nseq, d).transpose(1, 0, 2).reshape(nseq * T, d)
        y_f, y_b = tb(y_f), tb(y_b)
        og, ov = _ffn_untile(oh, FFN_TILE)
        fconv = jnp.concatenate([og[0, :, :dff], ov[0, :, :dff]], axis=1)
        fconv = fconv.reshape(FFN_CONV - 1, nseq, 2 * dff).transpose(1, 0, 2)
    else:
        tm = _largest_divisor(T, 512, FFN_ROW_CHUNK)
        y_f, y_b, oh = _ffn(x1b, x1f, *ffn_w, h0, p["ln2_g"], p["ln2_b"],
                            nseq=nseq, tm=tm, sh=1, hr=SUBLANES, alpha=alpha, single_buffer=False)
        og, ov = _ffn_untile(oh, FFN_TILE)
        k = FFN_CONV - 1
        fconv = jnp.concatenate([og[:, SUBLANES - k:, :dff], ov[:, SUBLANES - k:, :dff]], axis=2)
    return y_f, y_b, dict(mlstm=(C, n, m), ssd=(S, sconv)), fconv


def kernel(x_prompt, x_sample, state_mlstm_C, state_mlstm_n, state_mlstm_m, state_ssm, state_ssm_conv,
           state_ffn_conv, w_in, mlstm_b_i, mlstm_b_f, mlstm_norm_w, ssm_conv_w, ssm_conv_b, ssm_dt_bias,
           ssm_A_log, ssm_D, ssm_norm_w, w_out, ln1_g, ln1_b, ffn_w_up, ffn_conv_w, ffn_conv_b,
           ffn_w_down, ln2_g, ln2_b):
    weights = (w_in, mlstm_b_i, mlstm_b_f, mlstm_norm_w, ssm_conv_w, ssm_conv_b, ssm_dt_bias, ssm_A_log,
               ssm_D, ssm_norm_w, w_out, ln1_g, ln1_b, ffn_w_up, ffn_conv_w, ffn_conv_b, ffn_w_down,
               ln2_g, ln2_b)
    depth = w_in.shape[0]
    alpha = float((2 * depth) ** 0.25)
    B, T, D = x_prompt.shape
    Bs, Ts, _ = x_sample.shape
    nh = ssm_dt_bias.shape[1]
    Lp = PROMPT_CHUNK if T % PROMPT_CHUNK == 0 else T
    Ls = PROMPT_CHUNK if Ts % PROMPT_CHUNK == 0 else Ts

    xpf = x_prompt.reshape(B * T, D)
    xpb = xpf.astype(BF16)
    xsf = x_sample.reshape(Bs * Ts, D)
    xsb = xsf.astype(BF16)
    m_in = jnp.pad(state_mlstm_m, ((0, 0), (0, 0), (GATE_F, LANES - GATE_F - M_HEADS)))[:, :, None, :]
    s_in = state_ssm.reshape(depth, Bs, nh * S_HEADDIM, S_STATE)
    p_prev, s_prev, p_fconv, s_fconv = None, None, [], []
    for l in range(depth):
        p = _prep_layer(l, *weights)
        fp, dff = p["fp"], p["dff"]
        xpf, xpb, p_prev, fc = _layer(xpf, xpb, p, None, p_prev, layer=l, depth=depth, nseq=B, T=T, L=Lp,
                                      time_major_ffn=False, alpha=alpha)
        p_fconv.append(fc)
        fc0 = state_ffn_conv[l].transpose(1, 0, 2).reshape((FFN_CONV - 1) * Bs, 2 * dff)
        padf = lambda a: jnp.pad(a, ((0, 0), (0, fp - dff)))
        init = dict(mlstm=(state_mlstm_C, state_mlstm_n, m_in), ssd=(s_in, state_ssm_conv),
                    ffn=_ffn_tiles(padf(fc0[:, :dff]), padf(fc0[:, dff:]), FFN_TILE)[None])
        xsf, xsb, s_prev, fc = _layer(xsf, xsb, p, init, s_prev, layer=l, depth=depth, nseq=Bs, T=Ts, L=Ls,
                                      time_major_ffn=True, alpha=alpha)
        s_fconv.append(fc)

    def unpack(st, nseq):
        C, n, m = st["mlstm"]
        S, sconv = st["ssd"]
        return (C, n, m[:, :, 0, GATE_F:GATE_F + M_HEADS], S.reshape(depth, nseq, nh, S_HEADDIM, S_STATE),
                sconv[:, :, SUBLANES - (S_CONV - 1):, :])

    pC, pn, pm, pS, psc = unpack(p_prev, B)
    sC, sn, sm, sS, ssc = unpack(s_prev, Bs)
    return (xpf.reshape(B, T, D), xsf.reshape(Bs, Ts, D), pC, pn, pm, pS, psc, jnp.stack(p_fconv),
            sC, sn, sm, sS, ssc, jnp.stack(s_fconv))
```

```python
import functools

import jax
import jax.numpy as jnp
from jax import lax
from jax.experimental import pallas as pl
from jax.experimental.pallas import tpu as pltpu

F32 = jnp.float32
BF16 = jnp.bfloat16
HIGHEST = lax.Precision.HIGHEST

M_HEADS = 4
S_HEADDIM = 64
S_GROUPS = 2
S_STATE = 128
S_CONV = 4
FFN_CONV = 3
LN_EPS = 1e-5
GN_EPS = 1e-6
NEG_BIG = -1e30

LANES = 128
SUBLANES = 8
MXU_DIM = 256
VMEM_LIMIT_BYTES = 52 * 1024 * 1024

GATE_I = 0
GATE_F = 4
GATE_DT = 8

PROMPT_CHUNK = 128
SEQS_PER_STEP = 4
VREG_FILE = 64
IN_PROJ_TILE = 4 * MXU_DIM
FFN_TILE = 2 * MXU_DIM
FFN_ROW_CHUNK = 32
FFN_COMPILER_SCRATCH_BYTES = 8 * 1024 * 1024


def _largest_divisor(n, target, mult):
    best = None
    for d in range(mult, min(n, target) + 1, mult):
        if n % d == 0:
            best = d
    if best is None:
        raise ValueError(f"no tile for {n=} {target=} {mult=}")
    return best


def _params(sem, vmem_limit_bytes=VMEM_LIMIT_BYTES):
    return pltpu.CompilerParams(dimension_semantics=sem, vmem_limit_bytes=vmem_limit_bytes)


def _dot(a, b):
    return jnp.dot(a, b, preferred_element_type=F32)


def _dot_nt(a, b, precision=None):
    return lax.dot_general(a, b, (((1,), (1,)), ((), ())), precision=precision, preferred_element_type=F32)


def _dot_tn(a, b):
    return lax.dot_general(a, b, (((0,), (0,)), ((), ())), preferred_element_type=F32)


def _sigmoid(x):
    return 1.0 / (1.0 + jnp.exp(-x))


def _softplus(x):
    return jnp.maximum(x, 0.0) + jnp.log1p(jnp.exp(-jnp.abs(x)))


def _layer_norm(r, g, b):
    mu = jnp.mean(r, axis=-1, keepdims=True)
    d = r - mu
    var = jnp.mean(d * d, axis=-1, keepdims=True)
    return d * lax.rsqrt(var + LN_EPS) * g + b


def _in_proj_kernel(*refs, groups):
    x_ref = refs[0]
    ng = len(groups)
    w_refs = refs[1:1 + ng]
    o_refs = refs[1 + ng:1 + 2 * ng]
    j = pl.program_id(1)
    for (start, count), w_ref, o_ref in zip(groups, w_refs, o_refs):
        @pl.when(jnp.logical_and(j >= start, j < start + count))
        def _(w_ref=w_ref, o_ref=o_ref):
            o_ref[...] = _dot(x_ref[...], w_ref[...]).astype(o_ref.dtype)


def _in_proj(x, weights, out_dtypes, tm_target):
    m, k = x.shape
    tm = _largest_divisor(m, tm_target, 16)
    tiles, groups, start = [], [], 0
    for w in weights:
        n = w.shape[1]
        tn = _largest_divisor(n, IN_PROJ_TILE, LANES)
        tiles.append(tn)
        groups.append((start, n // tn))
        start += n // tn

    def clamp(s, c):
        return lambda i, j: (0, jnp.clip(j - s, 0, c - 1))

    def clamp_out(s, c):
        return lambda i, j: (i, jnp.clip(j - s, 0, c - 1))

    one = lambda c: dict(pipeline_mode=pl.Buffered(1)) if c == 1 else {}
    in_specs = [pl.BlockSpec((tm, k), lambda i, j: (i, 0))]
    in_specs += [pl.BlockSpec((k, tn), clamp(s, c), **one(c)) for tn, (s, c) in zip(tiles, groups)]
    out_specs = [pl.BlockSpec((tm, tn), clamp_out(s, c)) for tn, (s, c) in zip(tiles, groups)]
    out_shape = [jax.ShapeDtypeStruct((m, w.shape[1]), dt) for w, dt in zip(weights, out_dtypes)]
    return pl.pallas_call(
        functools.partial(_in_proj_kernel, groups=tuple(groups)),
        grid=(m // tm, start),
        in_specs=in_specs,
        out_specs=out_specs,
        out_shape=out_shape,
        compiler_params=_params(("parallel", "arbitrary")),
        name="in_proj",
    )(x, *weights)


def _mlstm_kernel(*refs, L, dk, nc, has_init, has_prev):
    q_ref, k_ref, v_ref, o_ref, g_ref, gb_ref, nw_ref = refs[:7]
    pos = 7
    if has_init:
        c0_ref, n0_ref, m0_ref = refs[pos:pos + 3]
        pos += 3
    if has_prev:
        pos += 3
    h_ref, c_out, n_out, m_out, c_s, n_s, m_s = refs[pos:pos + 7]
    c = pl.program_id(1)

    @pl.when(c == 0)
    def _():
        if has_init:
            c_s[...] = c0_ref[0]
            n_s[...] = n0_ref[0]
            m_s[...] = m0_ref[0]
        else:
            c_s[...] = jnp.zeros_like(c_s)
            n_s[...] = jnp.zeros_like(n_s)
            m_s[...] = jnp.zeros_like(m_s)

    row = lax.broadcasted_iota(jnp.int32, (L, L), 0)
    col = lax.broadcasted_iota(jnp.int32, (L, L), 1)
    causal = col <= row
    tri = jnp.where(causal, 1.0, 0.0).astype(F32)
    lane = lax.broadcasted_iota(jnp.int32, (L, LANES), 1)
    sel = jnp.where(lax.broadcasted_iota(jnp.int32, (SUBLANES, LANES), 0)
                    == lax.broadcasted_iota(jnp.int32, (SUBLANES, LANES), 1), 1.0, 0.0).astype(F32)
    m_lane = lax.broadcasted_iota(jnp.int32, (1, LANES), 1)
    k_scale = dk ** -0.5

    nsq = q_ref.shape[0]
    pairs = [(s_i, h) for s_i in range(nsq) for h in range(M_HEADS)]
    hsl = lambda h: slice(h * dk, (h + 1) * dk)


    gs, bcums, rowss, m_vecs = [], [], [], []
    for s_i in range(nsq):
        g = g_ref[s_i] + gb_ref[...]
        lf = jnp.minimum(g, 0.0) - jnp.log1p(jnp.exp(-jnp.abs(g)))
        bcum = jnp.dot(tri, lf, precision=HIGHEST, preferred_element_type=F32)
        packed = jnp.where(lane < GATE_F, g, bcum)
        gs.append(g)
        bcums.append(bcum)
        rowss.append(_dot_nt(sel, packed, precision=HIGHEST))
        m_vecs.append(m_s[s_i])

    st = []
    for s_i, h in pairs:
        bc = bcums[s_i][:, GATE_F + h:GATE_F + h + 1]
        igr = rowss[s_i][GATE_I + h:GATE_I + h + 1, :]
        br = rowss[s_i][GATE_F + h:GATE_F + h + 1, :]
        m_h = m_vecs[s_i][:, GATE_F + h:GATE_F + h + 1]
        log_w = jnp.where(causal, bc - br + igr, NEG_BIG)
        log_inter = bc + m_h
        m_t = jnp.maximum(log_inter, jnp.max(log_w, axis=1, keepdims=True))
        st.append(dict(bc=bc, m_h=m_h, m_t=m_t, w_intra=jnp.exp(log_w - m_t), w_inter=jnp.exp(log_inter - m_t)))

    for (s_i, h), e in zip(pairs, st):
        q = q_ref[s_i, :, hsl(h)]
        kf = k_ref[s_i, :, hsl(h)].astype(F32) * k_scale
        v = v_ref[s_i, :, hsl(h)]
        s = _dot_nt(q, kf.astype(BF16)) * e["w_intra"]
        num = _dot(s.astype(BF16), v) + e["w_inter"] * _dot(q, c_s[s_i, h].astype(BF16))
        qn = jnp.sum(q.astype(F32) * n_s[s_i, h:h + 1, :], axis=1, keepdims=True)
        den = jnp.sum(s, axis=1, keepdims=True) + e["w_inter"] * qn
        e["hh"] = num / jnp.maximum(jnp.abs(den), jnp.exp(-e["m_t"]))

    m_news = list(m_vecs)
    for (s_i, h), e in zip(pairs, st):
        bc, m_h, m_t = e["bc"], e["m_h"], e["m_t"]
        igc = gs[s_i][:, GATE_I + h:GATE_I + h + 1]
        m_end = m_t[L - 1:L, :]
        b_end = bc[L - 1:L, :]
        w_end = jnp.exp(b_end - bc + igc - m_end)
        decay = jnp.exp(b_end + m_h - m_end)
        kw = k_ref[s_i, :, hsl(h)].astype(F32) * k_scale * w_end
        c_s[s_i, h] = decay * c_s[s_i, h] + _dot_tn(kw.astype(BF16), v_ref[s_i, :, hsl(h)])
        n_s[s_i, h:h + 1, :] = decay * n_s[s_i, h:h + 1, :] + jnp.sum(kw, axis=0, keepdims=True)
        m_news[s_i] = jnp.where(m_lane == GATE_F + h, m_end, m_news[s_i])
    for s_i in range(nsq):
        m_s[s_i] = m_news[s_i]

    for (s_i, h), e in zip(pairs, st):
        hh = e["hh"]
        mu = jnp.mean(hh, axis=1, keepdims=True)
        d = hh - mu
        var = jnp.mean(d * d, axis=1, keepdims=True)
        hn = d * lax.rsqrt(var + GN_EPS) * nw_ref[:, hsl(h)]
        h_ref[s_i, :, hsl(h)] = (hn * _sigmoid(o_ref[s_i, :, hsl(h)].astype(F32))).astype(h_ref.dtype)

    @pl.when(c == nc - 1)
    def _():
        c_out[0] = c_s[...]
        n_out[0] = n_s[...]
        m_out[0] = m_s[...]


def _mlstm(qkv, o_pre, gates, gate_bias, norm_w, init, prev, *, layer, depth, nseq, T, L):
    W = qkv.shape[1] // 3
    dk = W // M_HEADS
    nc = T // L
    G = _largest_divisor(nseq, SEQS_PER_STEP, 1)
    qkv, o_pre, gates = (a.reshape(nseq, T, a.shape[1]) for a in (qkv, o_pre, gates))
    kern = functools.partial(_mlstm_kernel, L=L, dk=dk, nc=nc, has_init=init is not None,
                             has_prev=prev is not None)
    st_specs = [pl.BlockSpec((1, G, M_HEADS, dk, dk), lambda b, c: (layer, b, 0, 0, 0)),
                pl.BlockSpec((1, G, M_HEADS, dk), lambda b, c: (layer, b, 0, 0)),
                pl.BlockSpec((1, G, 1, LANES), lambda b, c: (layer, b, 0, 0))]
    in_specs = [pl.BlockSpec((G, L, W), lambda b, c: (b, c, 0)),
                pl.BlockSpec((G, L, W), lambda b, c: (b, c, 1)),
                pl.BlockSpec((G, L, W), lambda b, c: (b, c, 2)),
                pl.BlockSpec((G, L, W), lambda b, c: (b, c, 0)),
                pl.BlockSpec((G, L, LANES), lambda b, c: (b, c, 0)),
                pl.BlockSpec((1, LANES), lambda b, c: (0, 0)),
                pl.BlockSpec((1, W), lambda b, c: (0, 0))]
    args = [qkv, qkv, qkv, o_pre, gates, gate_bias, norm_w]
    if init is not None:
        in_specs += st_specs
        args += list(init)
    aliases = {}
    if prev is not None:
        aliases = {len(args) + t: 1 + t for t in range(3)}
        in_specs += [pl.BlockSpec(memory_space=pl.ANY)] * 3
        args += list(prev)
    h, C, n, m = pl.pallas_call(
        kern,
        grid=(nseq // G, nc),
        in_specs=in_specs,
        out_specs=[pl.BlockSpec((G, L, W), lambda b, c: (b, c, 0))] + st_specs,
        out_shape=[jax.ShapeDtypeStruct((nseq, T, W), BF16),
                   jax.ShapeDtypeStruct((depth, nseq, M_HEADS, dk, dk), F32),
                   jax.ShapeDtypeStruct((depth, nseq, M_HEADS, dk), F32),
                   jax.ShapeDtypeStruct((depth, nseq, 1, LANES), F32)],
        scratch_shapes=[pltpu.VMEM((G, M_HEADS, dk, dk), F32),
                        pltpu.VMEM((G, M_HEADS, dk), F32),
                        pltpu.VMEM((G, 1, LANES), F32)],
        input_output_aliases=aliases,
        compiler_params=_params(("parallel", "arbitrary")),
        name="mlstm",
    )(*args)
    return h.reshape(nseq * T, W), C, n, m


def _ssd_kernel(*refs, L, nheads, has_init, has_prev):
    z_ref, x_ref, g_ref, cw_ref, cb_ref, hp_ref, nw_ref = refs[:7]
    pos = 7
    if has_init:
        s0_ref, cv0_ref = refs[pos:pos + 2]
        pos += 2
    if has_prev:
        pos += 2
    y_ref, s_out, cv_out, s_s, xp_s, y_s = refs[pos:pos + 6]
    c = pl.program_id(1)
    nc = pl.num_programs(1)
    P = S_HEADDIM
    N = S_STATE
    hg = nheads // S_GROUPS
    W = nheads * P
    halo = SUBLANES
    kc = S_CONV

    G = z_ref.shape[0]

    @pl.when(c == 0)
    def _():
        xp_s[:, 0:halo, :] = jnp.zeros((G, halo, xp_s.shape[2]), F32)
        if has_init:
            s_s[...] = s0_ref[0]
            xp_s[:, halo - (kc - 1):halo, :] = cv0_ref[0]
        else:
            s_s[...] = jnp.zeros_like(s_s)

    hp = hp_ref[...]
    row = lax.broadcasted_iota(jnp.int32, (L, L), 0)
    col = lax.broadcasted_iota(jnp.int32, (L, L), 1)
    causal = col <= row
    tri = jnp.where(causal, 1.0, 0.0).astype(F32)
    sel = jnp.where(lax.broadcasted_iota(jnp.int32, (nheads, LANES), 0) + GATE_DT
                    == lax.broadcasted_iota(jnp.int32, (nheads, LANES), 1), 1.0, 0.0).astype(F32)
    gw = W // S_GROUPS

    seqs = range(G)
    heads = [(s_i, gi, hh) for s_i in seqs for gi in range(S_GROUPS) for hh in range(hg)]

    sq = []
    for s_i in seqs:
        xp_s[s_i, halo:halo + L, :] = x_ref[s_i].astype(F32)
        conv = cb_ref[...] + xp_s[s_i, halo:halo + L, :] * cw_ref[kc - 1:kc, :]
        for j in range(kc - 1):
            off = halo - (kc - 1) + j
            conv = conv + xp_s[s_i, off:off + L, :] * cw_ref[j:j + 1, :]
        tail = xp_s[s_i, L:L + halo, :]
        xp_s[s_i, 0:halo, :] = tail
        cv_out[0, s_i] = tail
        xbc = conv * _sigmoid(conv)
        dt = _softplus(g_ref[s_i] + hp[0:1, :])
        a = dt * (-jnp.exp(hp[1:2, :]))
        bcum = jnp.dot(tri, a, precision=HIGHEST, preferred_element_type=F32)
        b_last = bcum[L - 1:L, :]
        sq.append(dict(
            xs=xbc[:, 0:W], Bm=xbc[:, W:W + S_GROUPS * N], Cm=xbc[:, W + S_GROUPS * N:W + 2 * S_GROUPS * N],
            bcum=bcum, b_rows=_dot_nt(sel, bcum, precision=HIGHEST), dt_rows=_dot_nt(sel, dt, precision=HIGHEST),
            eb=jnp.exp(bcum), w_end=jnp.exp(b_last - bcum) * dt, dec_end=jnp.exp(b_last)))

    gq = {}
    for s_i in seqs:
        for gi in range(S_GROUPS):
            Bg = sq[s_i]["Bm"][:, gi * N:(gi + 1) * N].astype(BF16)
            Cg = sq[s_i]["Cm"][:, gi * N:(gi + 1) * N].astype(BF16)
            s_g = s_s[s_i, gi * hg * P:(gi + 1) * hg * P, :]
            gq[s_i, gi] = dict(Bg=Bg, cb=_dot_nt(Cg, Bg), y_inter=_dot_nt(Cg, s_g.astype(BF16)))

    mw_vregs = -(-L // (2 * SUBLANES)) * -(-L // LANES)
    batch = len(heads) if len(heads) * mw_vregs <= VREG_FILE else 1
    for b0 in range(0, len(heads), batch):
        mws = {}
        for s_i, gi, hh in heads[b0:b0 + batch]:
            idx = gi * hg + hh
            ln = GATE_DT + idx
            e = sq[s_i]
            decay = jnp.exp(jnp.where(causal, e["bcum"][:, ln:ln + 1] - e["b_rows"][idx:idx + 1, :], NEG_BIG))
            mws[s_i, gi, hh] = (gq[s_i, gi]["cb"] * decay * e["dt_rows"][idx:idx + 1, :]).astype(BF16)

        for s_i, gi, hh in heads[b0:b0 + batch]:
            idx = gi * hg + hh
            ln = GATE_DT + idx
            e = sq[s_i]
            xs_h = e["xs"][:, idx * P:(idx + 1) * P]
            y_s[s_i, :, idx * P:(idx + 1) * P] = (
                _dot(mws[s_i, gi, hh], xs_h.astype(BF16))
                + e["eb"][:, ln:ln + 1] * gq[s_i, gi]["y_inter"][:, hh * P:(hh + 1) * P]
                + hp[2:3, ln:ln + 1] * xs_h)

    for s_i in seqs:
        e = sq[s_i]
        for gi in range(S_GROUPS):
            xw = jnp.concatenate(
                [(e["xs"][:, (gi * hg + hh) * P:(gi * hg + hh + 1) * P]
                  * e["w_end"][:, GATE_DT + gi * hg + hh:GATE_DT + gi * hg + hh + 1]).astype(BF16)
                 for hh in range(hg)], axis=1)
            upd = _dot_tn(xw, gq[s_i, gi]["Bg"])
            for hh in range(hg):
                ln = GATE_DT + gi * hg + hh
                r0 = (gi * hg + hh) * P
                s_s[s_i, r0:r0 + P, :] = (e["dec_end"][:, ln:ln + 1] * s_s[s_i, r0:r0 + P, :]
                                          + upd[hh * P:(hh + 1) * P, :])

    for s_i in seqs:
        for gi in range(S_GROUPS):
            sl = slice(gi * gw, (gi + 1) * gw)
            z = z_ref[s_i, :, sl].astype(F32)
            gg = y_s[s_i, :, sl] * (z * _sigmoid(z))
            gg = gg * lax.rsqrt(jnp.mean(gg * gg, axis=1, keepdims=True) + GN_EPS)
            y_ref[s_i, :, sl] = (gg * nw_ref[:, sl]).astype(y_ref.dtype)

    @pl.when(c == nc - 1)
    def _():
        s_out[0] = s_s[...]


def _ssd(z, xbc, gates, conv_w, conv_b, head_params, norm_w, init, prev, *, layer, depth, nseq, T, L):
    W = z.shape[1]
    nheads = W // S_HEADDIM
    CD = xbc.shape[1]
    nc = T // L
    G = _largest_divisor(nseq, SEQS_PER_STEP, 1) if L < PROMPT_CHUNK else 1
    z, xbc, gates = (a.reshape(nseq, T, a.shape[1]) for a in (z, xbc, gates))
    kern = functools.partial(_ssd_kernel, L=L, nheads=nheads, has_init=init is not None, has_prev=prev is not None)
    s_spec = pl.BlockSpec((1, G, nheads * S_HEADDIM, S_STATE), lambda b, c: (layer, b, 0, 0))
    in_specs = [pl.BlockSpec((G, L, W), lambda b, c: (b, c, 0)),
                pl.BlockSpec((G, L, CD), lambda b, c: (b, c, 0)),
                pl.BlockSpec((G, L, LANES), lambda b, c: (b, c, 0)),
                pl.BlockSpec((S_CONV, CD), lambda b, c: (0, 0)),
                pl.BlockSpec((1, CD), lambda b, c: (0, 0)),
                pl.BlockSpec((SUBLANES, LANES), lambda b, c: (0, 0)),
                pl.BlockSpec((1, W), lambda b, c: (0, 0))]
    args = [z, xbc, gates, conv_w, conv_b, head_params, norm_w]
    if init is not None:
        in_specs += [s_spec, pl.BlockSpec((1, G, S_CONV - 1, CD), lambda b, c: (layer, b, 0, 0))]
        args += list(init)
    aliases = {}
    if prev is not None:
        aliases = {len(args) + t: 1 + t for t in range(2)}
        in_specs += [pl.BlockSpec(memory_space=pl.ANY)] * 2
        args += list(prev)
    y, S, cv = pl.pallas_call(
        kern,
        grid=(nseq // G, nc),
        in_specs=in_specs,
        out_specs=[pl.BlockSpec((G, L, W), lambda b, c: (b, c, 0)),
                   s_spec,
                   pl.BlockSpec((1, G, SUBLANES, CD), lambda b, c: (layer, b, 0, 0))],
        out_shape=[jax.ShapeDtypeStruct((nseq, T, W), BF16),
                   jax.ShapeDtypeStruct((depth, nseq, nheads * S_HEADDIM, S_STATE), F32),
                   jax.ShapeDtypeStruct((depth, nseq, SUBLANES, CD), F32)],
        scratch_shapes=[pltpu.VMEM((G, nheads * S_HEADDIM, S_STATE), F32),
                        pltpu.VMEM((G, SUBLANES + L, CD), F32),
                        pltpu.VMEM((G, L, W), F32)],
        input_output_aliases=aliases,
        compiler_params=_params(("parallel", "arbitrary")),
        name="ssd",
    )(*args)
    return y.reshape(nseq * T, W), S, cv


def _outproj_kernel(hm_ref, hs_ref, wt_ref, wb_ref, x_ref, g_ref, b_ref, y_ref, yb_ref, *, alpha):
    mix = _dot(hm_ref[...], wt_ref[0]) + _dot(hs_ref[...], wb_ref[0])
    y = _layer_norm(alpha * x_ref[...] + mix, g_ref[...], b_ref[...])
    y_ref[...] = y
    yb_ref[...] = y.astype(BF16)


def _outproj(hm, hs, w_out, x, ln_g, ln_b, *, alpha):
    m, d = x.shape
    w = hm.shape[1]
    tm = _largest_divisor(m, 512, 16)
    kern = functools.partial(_outproj_kernel, alpha=alpha)
    return pl.pallas_call(
        kern,
        grid=(m // tm,),
        in_specs=[pl.BlockSpec((tm, w), lambda i: (i, 0)),
                  pl.BlockSpec((tm, w), lambda i: (i, 0)),
                  pl.BlockSpec((1, w, d), lambda i: (0, 0, 0)),
                  pl.BlockSpec((1, w, d), lambda i: (1, 0, 0)),
                  pl.BlockSpec((tm, d), lambda i: (i, 0)),
                  pl.BlockSpec((1, d), lambda i: (0, 0)),
                  pl.BlockSpec((1, d), lambda i: (0, 0))],
        out_specs=[pl.BlockSpec((tm, d), lambda i: (i, 0)),
                   pl.BlockSpec((tm, d), lambda i: (i, 0))],
        out_shape=[jax.ShapeDtypeStruct((m, d), F32),
                   jax.ShapeDtypeStruct((m, d), BF16)],
        compiler_params=_params(("parallel",)),
        name="out_proj_ln",
    )(hm, hs, w_out, w_out, x, ln_g, ln_b)


def _ffn_kernel(*refs, sh, hr, tps, rc, alpha, has_init):
    xb_ref, xf_ref, wu_ref, cw_ref, cb_ref, wd_ref = refs[:6]
    pos = 6
    if has_init:
        h0_ref = refs[pos]
        pos += 1
    lg_ref, lb_ref, y_ref, yb_ref, oh_ref, acc, ext, act, hal = refs[pos:pos + 9]
    i = pl.program_id(0)
    j = pl.program_id(1)
    nj = pl.num_programs(1)
    tm = xb_ref.shape[0]
    tf = wd_ref.shape[1]
    first = (i % tps) == 0

    @pl.when(first)
    def _():
        if has_init:
            ext[0:hr, :] = h0_ref[0, 0]
        else:
            ext[0:hr, :] = jnp.zeros((hr, ext.shape[1]), F32)

    @pl.when(jnp.logical_not(first))
    def _():
        ext[0:hr, :] = hal[j]

    ext[hr:hr + tm, :] = _dot(xb_ref[...], wu_ref[0])
    last = ext[tm:tm + hr, :]
    hal[j] = last
    oh_ref[0, 0] = last

    for r0 in range(0, tm, rc):
        c = (ext[hr - 2 * sh + r0:hr - 2 * sh + r0 + rc, :] * cw_ref[0, 0:1, :]
             + ext[hr - sh + r0:hr - sh + r0 + rc, :] * cw_ref[0, 1:2, :]
             + ext[hr + r0:hr + r0 + rc, :] * cw_ref[0, 2:3, :] + cb_ref[0])
        cg = c[:, :tf]
        act[r0:r0 + rc, :] = (cg * _sigmoid(cg) * c[:, tf:]).astype(BF16)

    @pl.when(j == 0)
    def _():
        acc[...] = jnp.zeros_like(acc)

    acc[...] += _dot(act[...], wd_ref[0])

    @pl.when(j == nj - 1)
    def _():
        y = _layer_norm(alpha * xf_ref[...] + acc[...], lg_ref[...], lb_ref[...])
        y_ref[...] = y
        yb_ref[...] = y.astype(BF16)


def _ffn(xb, xf, w_u, cw, cb, w_d, h0, ln_g, ln_b, *, nseq, tm, sh, hr, alpha, single_buffer):
    m, d = xf.shape
    nj, tf = w_d.shape[0], w_d.shape[1]
    ni = m // tm
    tps = ni // nseq
    kern = functools.partial(_ffn_kernel, sh=sh, hr=hr, tps=tps, rc=FFN_ROW_CHUNK, alpha=alpha,
                             has_init=h0 is not None)
    mode = dict(pipeline_mode=pl.Buffered(1)) if single_buffer else {}
    nbuf = 1 if single_buffer else 2
    vmem_bytes = (nbuf * tm * d * (2 + 4 + 4 + 2)
                  + 2 * (d * 2 * tf * 2 + tf * d * 2)
                  + 2 * 2 * hr * 2 * tf * 4
                  + tm * d * 4 + (hr + tm) * 2 * tf * 4 + tm * tf * 2 + nj * hr * 2 * tf * 4
                  + FFN_COMPILER_SCRATCH_BYTES)
    in_specs = [pl.BlockSpec((tm, d), lambda i, j: (i, 0), **mode),
                pl.BlockSpec((tm, d), lambda i, j: (i, 0), **mode),
                pl.BlockSpec((1, d, 2 * tf), lambda i, j: (j, 0, 0)),
                pl.BlockSpec((1, FFN_CONV, 2 * tf), lambda i, j: (j, 0, 0)),
                pl.BlockSpec((1, 1, 2 * tf), lambda i, j: (j, 0, 0)),
                pl.BlockSpec((1, tf, d), lambda i, j: (j, 0, 0))]
    args = [xb, xf, w_u, cw, cb, w_d]
    if h0 is not None:
        in_specs.append(pl.BlockSpec((1, 1, hr, 2 * tf), lambda i, j: (i // tps, j, 0, 0)))
        args.append(h0)
    in_specs += [pl.BlockSpec((1, d), lambda i, j: (0, 0))] * 2
    args += [ln_g, ln_b]
    return pl.pallas_call(
        kern,
        grid=(ni, nj),
        in_specs=in_specs,
        out_specs=[pl.BlockSpec((tm, d), lambda i, j: (i, 0), **mode),
                   pl.BlockSpec((tm, d), lambda i, j: (i, 0), **mode),
                   pl.BlockSpec((1, 1, hr, 2 * tf),
                                lambda i, j: (i // tps, jnp.where(i % tps == tps - 1, j, 0), 0, 0))],
        out_shape=[jax.ShapeDtypeStruct((m, d), F32),
                   jax.ShapeDtypeStruct((m, d), BF16),
                   jax.ShapeDtypeStruct((nseq, nj, hr, 2 * tf), F32)],
        scratch_shapes=[pltpu.VMEM((tm, d), F32),
                        pltpu.VMEM((hr + tm, 2 * tf), F32),
                        pltpu.VMEM((tm, tf), BF16),
                        pltpu.VMEM((nj, hr, 2 * tf), F32)],
        compiler_params=_params(("arbitrary", "arbitrary"), vmem_limit_bytes=vmem_bytes),
        name="conv_ffn_ln",
    )(*args)


def _ffn_tiles(g, v, tf):
    nj = g.shape[-1] // tf
    lead = g.shape[:-1]
    t = jnp.stack([g.reshape(lead + (nj, tf)), v.reshape(lead + (nj, tf))], axis=-2)
    return jnp.moveaxis(t.reshape(lead + (nj, 2 * tf)), -2, 0)


def _ffn_untile(t, tf):
    nj = t.shape[-3]
    t = jnp.moveaxis(t, -3, -2)
    lead = t.shape[:-2]
    return (t[..., :tf].reshape(lead + (nj * tf,)), t[..., tf:].reshape(lead + (nj * tf,)))


def _tile_up_kernel(w_ref, o_ref, *, dff, tf):
    nj = o_ref.shape[0]
    for j in range(nj):
        lo = j * tf
        width = min(tf, dff - lo)
        for half, base in ((0, 0), (1, dff)):
            o_ref[j, :, half * tf:half * tf + width] = w_ref[0, :, base + lo:base + lo + width].astype(BF16)
            if width < tf:
                o_ref[j, :, half * tf + width:(half + 1) * tf] = jnp.zeros((o_ref.shape[1], tf - width), BF16)


def _tile_up(w_up, layer, tf):
    _, d, two_dff = w_up.shape
    dff = two_dff // 2
    assert dff % LANES == 0
    nj = -(-dff // tf)
    tr = _largest_divisor(d, 128, 16)
    return pl.pallas_call(
        functools.partial(_tile_up_kernel, dff=dff, tf=tf),
        grid=(d // tr,),
        in_specs=[pl.BlockSpec((1, tr, two_dff), lambda r: (layer, r, 0))],
        out_specs=pl.BlockSpec((nj, tr, 2 * tf), lambda r: (0, r, 0)),
        out_shape=jax.ShapeDtypeStruct((nj, d, 2 * tf), BF16),
        compiler_params=_params(("parallel",)),
        name="tile_w_up",
    )(w_up)


def _tile_down_kernel(w_ref, o_ref, *, dff):
    j = pl.program_id(0)
    tf = o_ref.shape[1]
    rows = lax.broadcasted_iota(jnp.int32, o_ref.shape[1:], 0) + j * tf
    o_ref[0] = jnp.where(rows < dff, w_ref[0], 0.0).astype(BF16)


def _tile_down(w_down, layer, tf):
    _, dff, d = w_down.shape
    nj = -(-dff // tf)
    return pl.pallas_call(
        functools.partial(_tile_down_kernel, dff=dff),
        grid=(nj,),
        in_specs=[pl.BlockSpec((1, tf, d), lambda j: (layer, j, 0))],
        out_specs=pl.BlockSpec((1, tf, d), lambda j: (j, 0, 0)),
        out_shape=jax.ShapeDtypeStruct((nj, tf, d), BF16),
        compiler_params=_params(("parallel",)),
        name="tile_w_down",
    )(w_down)


def _prep_layer(l, w_in, b_i, b_f, m_norm_w, s_conv_w, s_conv_b, dt_bias, A_log, D_skip, s_norm_w, w_out,
                ln1_g, ln1_b, w_up, f_conv_w, f_conv_b, w_down, ln2_g, ln2_b):
    mw = m_norm_w.shape[1]
    sw = s_norm_w.shape[1]
    cd = s_conv_w.shape[2]
    nh = dt_bias.shape[1]
    dff = w_down.shape[1]
    tf = FFN_TILE
    fp = -(-dff // tf) * tf
    o = 0
    cols = {}
    for name, size in (("qkv", 3 * mw), ("o", mw), ("i", M_HEADS), ("f", M_HEADS),
                       ("z", sw), ("xbc", cd), ("dt", nh)):
        cols[name] = (o, o + size)
        o += size
    wl = w_in[l]
    cut = lambda name: wl[:, cols[name][0]:cols[name][1]]
    p = {}
    wg = jnp.concatenate([cut("i"), cut("f"), cut("dt")], axis=1)
    p["w_in"] = [cut("qkv").astype(BF16), cut("o").astype(BF16), cut("z").astype(BF16), cut("xbc").astype(BF16),
                 jnp.pad(wg, ((0, 0), (0, LANES - wg.shape[1]))).astype(BF16)]
    gb = jnp.concatenate([b_i[l], b_f[l]])
    p["gate_bias"] = jnp.pad(gb, (0, LANES - gb.shape[0]))[None, :]
    p["m_norm_w"] = m_norm_w[l][None, :]
    p["s_conv_w"] = s_conv_w[l]
    p["s_conv_b"] = s_conv_b[l][None, :]
    hp = jnp.stack([dt_bias[l], A_log[l], D_skip[l]])
    p["head_params"] = jnp.pad(hp, ((0, SUBLANES - 3), (GATE_DT, LANES - GATE_DT - nh)))
    p["s_norm_w"] = s_norm_w[l][None, :]
    p["w_out"] = w_out[l].astype(BF16).reshape(2, mw, w_out.shape[2])
    p["ln1_g"] = ln1_g[l][None, :]
    p["ln1_b"] = ln1_b[l][None, :]
    padc = lambda a: jnp.pad(a, ((0, 0), (0, fp - dff)))
    p["w_up"] = _tile_up(w_up, l, tf)
    p["f_conv_w"] = _ffn_tiles(padc(f_conv_w[l][:, :dff]), padc(f_conv_w[l][:, dff:]), tf)
    p["f_conv_b"] = _ffn_tiles(padc(f_conv_b[l][None, :dff]), padc(f_conv_b[l][None, dff:]), tf)
    p["w_down"] = _tile_down(w_down, l, tf)
    p["ln2_g"] = ln2_g[l][None, :]
    p["ln2_b"] = ln2_b[l][None, :]
    p["dff"] = dff
    p["fp"] = fp
    return p


def _layer(xf, xb, p, init, prev, *, layer, depth, nseq, T, L, time_major_ffn, alpha):
    qkv, o_pre, z, xbc, gates = _in_proj(xb, p["w_in"], [BF16, BF16, BF16, BF16, F32], tm_target=1024)
    hm, C, n, m = _mlstm(qkv, o_pre, gates, p["gate_bias"], p["m_norm_w"],
                         None if init is None else init["mlstm"], None if prev is None else prev["mlstm"],
                         layer=layer, depth=depth, nseq=nseq, T=T, L=L)
    hs, S, sconv = _ssd(z, xbc, gates, p["s_conv_w"], p["s_conv_b"], p["head_params"], p["s_norm_w"],
                        None if init is None else init["ssd"], None if prev is None else prev["ssd"],
                        layer=layer, depth=depth, nseq=nseq, T=T, L=L)
    x1f, x1b = _outproj(hm, hs, p["w_out"], xf, p["ln1_g"], p["ln1_b"], alpha=alpha)
    d = xf.shape[1]
    dff = p["dff"]
    ffn_w = (p["w_up"], p["f_conv_w"], p["f_conv_b"], p["w_down"])
    h0 = None if init is None else init["ffn"]
    if time_major_ffn:
        tr = lambda a: a.reshape(nseq, T, d).transpose(1, 0, 2).reshape(nseq * T, d)
        x1f_t, x1b_t = tr(x1f), tr(x1b)
        tm = _largest_divisor(nseq * T, 512, (FFN_CONV - 1) * nseq)
        y_f, y_b, oh = _ffn(x1b_t, x1f_t, *ffn_w, h0, p["ln2_g"], p["ln2_b"], nseq=1, tm=tm, sh=nseq,
                            hr=(FFN_CONV - 1) * nseq, alpha=alpha, single_buffer=True)
        tb = lambda a: a.reshape(T, nseq, d).transpose(1, 0, 2).reshape(nseq * T, d)
        y_f, y_b = tb(y_f), tb(y_b)
        og, ov = _ffn_untile(oh, FFN_TILE)
        fconv = jnp.concatenate([og[0, :, :dff], ov[0, :, :dff]], axis=1)
        fconv = fconv.reshape(FFN_CONV - 1, nseq, 2 * dff).transpose(1, 0, 2)
    else:
        tm = _largest_divisor(T, 512, FFN_ROW_CHUNK)
        y_f, y_b, oh = _ffn(x1b, x1f, *ffn_w, h0, p["ln2_g"], p["ln2_b"],
                            nseq=nseq, tm=tm, sh=1, hr=SUBLANES, alpha=alpha, single_buffer=False)
        og, ov = _ffn_untile(oh, FFN_TILE)
        k = FFN_CONV - 1
        fconv = jnp.concatenate([og[:, SUBLANES - k:, :dff], ov[:, SUBLANES - k:, :dff]], axis=2)
    return y_f, y_b, dict(mlstm=(C, n, m), ssd=(S, sconv)), fconv


def kernel(x_prompt, x_sample, state_mlstm_C, state_mlstm_n, state_mlstm_m, state_ssm, state_ssm_conv,
           state_ffn_conv, w_in, mlstm_b_i, mlstm_b_f, mlstm_norm_w, ssm_conv_w, ssm_conv_b, ssm_dt_bias,
           ssm_A_log, ssm_D, ssm_norm_w, w_out, ln1_g, ln1_b, ffn_w_up, ffn_conv_w, ffn_conv_b,
           ffn_w_down, ln2_g, ln2_b):
    weights = (w_in, mlstm_b_i, mlstm_b_f, mlstm_norm_w, ssm_conv_w, ssm_conv_b, ssm_dt_bias, ssm_A_log,
               ssm_D, ssm_norm_w, w_out, ln1_g, ln1_b, ffn_w_up, ffn_conv_w, ffn_conv_b, ffn_w_down,
               ln2_g, ln2_b)
    depth = w_in.shape[0]
    alpha = float((2 * depth) ** 0.25)
    B, T, D = x_prompt.shape
    Bs, Ts, _ = x_sample.shape
    nh = ssm_dt_bias.shape[1]
    Lp = PROMPT_CHUNK if T % PROMPT_CHUNK == 0 else T
    Ls = PROMPT_CHUNK if Ts % PROMPT_CHUNK == 0 else Ts

    xpf = x_prompt.reshape(B * T, D)
    xpb = xpf.astype(BF16)
    xsf = x_sample.reshape(Bs * Ts, D)
    xsb = xsf.astype(BF16)
    m_in = jnp.pad(state_mlstm_m, ((0, 0), (0, 0), (GATE_F, LANES - GATE_F - M_HEADS)))[:, :, None, :]
    s_in = state_ssm.reshape(depth, Bs, nh * S_HEADDIM, S_STATE)
    p_prev, s_prev, p_fconv, s_fconv = None, None, [], []
    for l in range(depth):
        p = _prep_layer(l, *weights)
        fp, dff = p["fp"], p["dff"]
        xpf, xpb, p_prev, fc = _layer(xpf, xpb, p, None, p_prev, layer=l, depth=depth, nseq=B, T=T, L=Lp,
                                      time_major_ffn=False, alpha=alpha)
        p_fconv.append(fc)
        fc0 = state_ffn_conv[l].transpose(1, 0, 2).reshape((FFN_CONV - 1) * Bs, 2 * dff)
        padf = lambda a: jnp.pad(a, ((0, 0), (0, fp - dff)))
        init = dict(mlstm=(state_mlstm_C, state_mlstm_n, m_in), ssd=(s_in, state_ssm_conv),
                    ffn=_ffn_tiles(padf(fc0[:, :dff]), padf(fc0[:, dff:]), FFN_TILE)[None])
        xsf, xsb, s_prev, fc = _layer(xsf, xsb, p, init, s_prev, layer=l, depth=depth, nseq=Bs, T=Ts, L=Ls,
                                      time_major_ffn=True, alpha=alpha)
        s_fconv.append(fc)

    def unpack(st, nseq):
        C, n, m = st["mlstm"]
        S, sconv = st["ssd"]
        return (C, n, m[:, :, 0, GATE_F:GATE_F + M_HEADS], S.reshape(depth, nseq, nh, S_HEADDIM, S_STATE),
                sconv[:, :, SUBLANES - (S_CONV - 1):, :])

    pC, pn, pm, pS, psc = unpack(p_prev, B)
    sC, sn, sm, sS, ssc = unpack(s_prev, Bs)
    return (xpf.reshape(B, T, D), xsf.reshape(Bs, Ts, D), pC, pn, pm, pS, psc, jnp.stack(p_fconv),
            sC, sn, sm, sS, ssc, jnp.stack(s_fconv))
```

```python
import functools

import jax
import jax.numpy as jnp
from jax import lax
from jax.experimental import pallas as pl
from jax.experimental.pallas import tpu as pltpu

F32 = jnp.float32
BF16 = jnp.bfloat16
HIGHEST = lax.Precision.HIGHEST

M_HEADS = 4
S_HEADDIM = 64
S_GROUPS = 2
S_STATE = 128
S_CONV = 4
FFN_CONV = 3
LN_EPS = 1e-5
GN_EPS = 1e-6
NEG_BIG = -1e30

LANES = 128
SUBLANES = 8
MXU_DIM = 256
VMEM_LIMIT_BYTES = 52 * 1024 * 1024

GATE_I = 0
GATE_F = 4
GATE_DT = 8

PROMPT_CHUNK = 128
SEQS_PER_STEP = 8
VREG_FILE = 64
IN_PROJ_TILE = 4 * MXU_DIM
FFN_TILE = 2 * MXU_DIM
FFN_ROW_CHUNK = 32
FFN_COMPILER_SCRATCH_BYTES = 8 * 1024 * 1024


def _largest_divisor(n, target, mult):
    best = None
    for d in range(mult, min(n, target) + 1, mult):
        if n % d == 0:
            best = d
    if best is None:
        raise ValueError(f"no tile for {n=} {target=} {mult=}")
    return best


def _params(sem, vmem_limit_bytes=VMEM_LIMIT_BYTES):
    return pltpu.CompilerParams(dimension_semantics=sem, vmem_limit_bytes=vmem_limit_bytes)


def _dot(a, b):
    return jnp.dot(a, b, preferred_element_type=F32)


def _dot_nt(a, b, precision=None):
    return lax.dot_general(a, b, (((1,), (1,)), ((), ())), precision=precision, preferred_element_type=F32)


def _dot_tn(a, b):
    return lax.dot_general(a, b, (((0,), (0,)), ((), ())), preferred_element_type=F32)


def _sigmoid(x):
    return 1.0 / (1.0 + jnp.exp(-x))


def _softplus(x):
    return jnp.maximum(x, 0.0) + jnp.log1p(jnp.exp(-jnp.abs(x)))


def _layer_norm(r, g, b):
    mu = jnp.mean(r, axis=-1, keepdims=True)
    d = r - mu
    var = jnp.mean(d * d, axis=-1, keepdims=True)
    return d * lax.rsqrt(var + LN_EPS) * g + b


def _in_proj_kernel(*refs, groups):
    x_ref = refs[0]
    ng = len(groups)
    w_refs = refs[1:1 + ng]
    o_refs = refs[1 + ng:1 + 2 * ng]
    j = pl.program_id(1)
    for (start, count), w_ref, o_ref in zip(groups, w_refs, o_refs):
        @pl.when(jnp.logical_and(j >= start, j < start + count))
        def _(w_ref=w_ref, o_ref=o_ref):
            o_ref[...] = _dot(x_ref[...], w_ref[...]).astype(o_ref.dtype)


def _in_proj(x, weights, out_dtypes, tm_target):
    m, k = x.shape
    tm = _largest_divisor(m, tm_target, 16)
    tiles, groups, start = [], [], 0
    for w in weights:
        n = w.shape[1]
        tn = _largest_divisor(n, IN_PROJ_TILE, LANES)
        tiles.append(tn)
        groups.append((start, n // tn))
        start += n // tn

    def clamp(s, c):
        return lambda i, j: (0, jnp.clip(j - s, 0, c - 1))

    def clamp_out(s, c):
        return lambda i, j: (i, jnp.clip(j - s, 0, c - 1))

    one = lambda c: dict(pipeline_mode=pl.Buffered(1)) if c == 1 else {}
    in_specs = [pl.BlockSpec((tm, k), lambda i, j: (i, 0))]
    in_specs += [pl.BlockSpec((k, tn), clamp(s, c), **one(c)) for tn, (s, c) in zip(tiles, groups)]
    out_specs = [pl.BlockSpec((tm, tn), clamp_out(s, c)) for tn, (s, c) in zip(tiles, groups)]
    out_shape = [jax.ShapeDtypeStruct((m, w.shape[1]), dt) for w, dt in zip(weights, out_dtypes)]
    return pl.pallas_call(
        functools.partial(_in_proj_kernel, groups=tuple(groups)),
        grid=(m // tm, start),
        in_specs=in_specs,
        out_specs=out_specs,
        out_shape=out_shape,
        compiler_params=_params(("parallel", "arbitrary")),
        name="in_proj",
    )(x, *weights)


def _mlstm_kernel(*refs, L, dk, nc, has_init, has_prev):
    q_ref, k_ref, v_ref, o_ref, g_ref, gb_ref, nw_ref = refs[:7]
    pos = 7
    if has_init:
        c0_ref, n0_ref, m0_ref = refs[pos:pos + 3]
        pos += 3
    if has_prev:
        pos += 3
    h_ref, c_out, n_out, m_out, c_s, n_s, m_s = refs[pos:pos + 7]
    c = pl.program_id(1)

    @pl.when(c == 0)
    def _():
        if has_init:
            c_s[...] = c0_ref[0]
            n_s[...] = n0_ref[0]
            m_s[...] = m0_ref[0]
        else:
            c_s[...] = jnp.zeros_like(c_s)
            n_s[...] = jnp.zeros_like(n_s)
            m_s[...] = jnp.zeros_like(m_s)

    row = lax.broadcasted_iota(jnp.int32, (L, L), 0)
    col = lax.broadcasted_iota(jnp.int32, (L, L), 1)
    causal = col <= row
    tri = jnp.where(causal, 1.0, 0.0).astype(F32)
    lane = lax.broadcasted_iota(jnp.int32, (L, LANES), 1)
    sel = jnp.where(lax.broadcasted_iota(jnp.int32, (SUBLANES, LANES), 0)
                    == lax.broadcasted_iota(jnp.int32, (SUBLANES, LANES), 1), 1.0, 0.0).astype(F32)
    m_lane = lax.broadcasted_iota(jnp.int32, (1, LANES), 1)
    k_scale = dk ** -0.5

    nsq = q_ref.shape[0]
    pairs = [(s_i, h) for s_i in range(nsq) for h in range(M_HEADS)]
    hsl = lambda h: slice(h * dk, (h + 1) * dk)


    gs, bcums, rowss, m_vecs = [], [], [], []
    for s_i in range(nsq):
        g = g_ref[s_i] + gb_ref[...]
        lf = jnp.minimum(g, 0.0) - jnp.log1p(jnp.exp(-jnp.abs(g)))
        bcum = jnp.dot(tri, lf, precision=HIGHEST, preferred_element_type=F32)
        packed = jnp.where(lane < GATE_F, g, bcum)
        gs.append(g)
        bcums.append(bcum)
        rowss.append(_dot_nt(sel, packed, precision=HIGHEST))
        m_vecs.append(m_s[s_i])

    st = []
    for s_i, h in pairs:
        bc = bcums[s_i][:, GATE_F + h:GATE_F + h + 1]
        igr = rowss[s_i][GATE_I + h:GATE_I + h + 1, :]
        br = rowss[s_i][GATE_F + h:GATE_F + h + 1, :]
        m_h = m_vecs[s_i][:, GATE_F + h:GATE_F + h + 1]
        log_w = jnp.where(causal, bc - br + igr, NEG_BIG)
        log_inter = bc + m_h
        m_t = jnp.maximum(log_inter, jnp.max(log_w, axis=1, keepdims=True))
        st.append(dict(bc=bc, m_h=m_h, m_t=m_t, w_intra=jnp.exp(log_w - m_t), w_inter=jnp.exp(log_inter - m_t)))

    for (s_i, h), e in zip(pairs, st):
        q = q_ref[s_i, :, hsl(h)]
        kf = k_ref[s_i, :, hsl(h)].astype(F32) * k_scale
        v = v_ref[s_i, :, hsl(h)]
        s = _dot_nt(q, kf.astype(BF16)) * e["w_intra"]
        num = _dot(s.astype(BF16), v) + e["w_inter"] * _dot(q, c_s[s_i, h].astype(BF16))
        qn = jnp.sum(q.astype(F32) * n_s[s_i, h:h + 1, :], axis=1, keepdims=True)
        den = jnp.sum(s, axis=1, keepdims=True) + e["w_inter"] * qn
        e["hh"] = num / jnp.maximum(jnp.abs(den), jnp.exp(-e["m_t"]))

    m_news = list(m_vecs)
    for (s_i, h), e in zip(pairs, st):
        bc, m_h, m_t = e["bc"], e["m_h"], e["m_t"]
        igc = gs[s_i][:, GATE_I + h:GATE_I + h + 1]
        m_end = m_t[L - 1:L, :]
        b_end = bc[L - 1:L, :]
        w_end = jnp.exp(b_end - bc + igc - m_end)
        decay = jnp.exp(b_end + m_h - m_end)
        kw = k_ref[s_i, :, hsl(h)].astype(F32) * k_scale * w_end
        c_s[s_i, h] = decay * c_s[s_i, h] + _dot_tn(kw.astype(BF16), v_ref[s_i, :, hsl(h)])
        n_s[s_i, h:h + 1, :] = decay * n_s[s_i, h:h + 1, :] + jnp.sum(kw, axis=0, keepdims=True)
        m_news[s_i] = jnp.where(m_lane == GATE_F + h, m_end, m_news[s_i])
    for s_i in range(nsq):
        m_s[s_i] = m_news[s_i]

    for (s_i, h), e in zip(pairs, st):
        hh = e["hh"]
        mu = jnp.mean(hh, axis=1, keepdims=True)
        d = hh - mu
        var = jnp.mean(d * d, axis=1, keepdims=True)
        hn = d * lax.rsqrt(var + GN_EPS) * nw_ref[:, hsl(h)]
        h_ref[s_i, :, hsl(h)] = (hn * _sigmoid(o_ref[s_i, :, hsl(h)].astype(F32))).astype(h_ref.dtype)

    @pl.when(c == nc - 1)
    def _():
        c_out[0] = c_s[...]
        n_out[0] = n_s[...]
        m_out[0] = m_s[...]


def _mlstm(qkv, o_pre, gates, gate_bias, norm_w, init, prev, *, layer, depth, nseq, T, L):
    W = qkv.shape[1] // 3
    dk = W // M_HEADS
    nc = T // L
    G = _largest_divisor(nseq, SEQS_PER_STEP, 1)
    qkv, o_pre, gates = (a.reshape(nseq, T, a.shape[1]) for a in (qkv, o_pre, gates))
    kern = functools.partial(_mlstm_kernel, L=L, dk=dk, nc=nc, has_init=init is not None,
                             has_prev=prev is not None)
    st_specs = [pl.BlockSpec((1, G, M_HEADS, dk, dk), lambda b, c: (layer, b, 0, 0, 0)),
                pl.BlockSpec((1, G, M_HEADS, dk), lambda b, c: (layer, b, 0, 0)),
                pl.BlockSpec((1, G, 1, LANES), lambda b, c: (layer, b, 0, 0))]
    in_specs = [pl.BlockSpec((G, L, W), lambda b, c: (b, c, 0)),
                pl.BlockSpec((G, L, W), lambda b, c: (b, c, 1)),
                pl.BlockSpec((G, L, W), lambda b, c: (b, c, 2)),
                pl.BlockSpec((G, L, W), lambda b, c: (b, c, 0)),
                pl.BlockSpec((G, L, LANES), lambda b, c: (b, c, 0)),
                pl.BlockSpec((1, LANES), lambda b, c: (0, 0)),
                pl.BlockSpec((1, W), lambda b, c: (0, 0))]
    args = [qkv, qkv, qkv, o_pre, gates, gate_bias, norm_w]
    if init is not None:
        in_specs += st_specs
        args += list(init)
    aliases = {}
    if prev is not None:
        aliases = {len(args) + t: 1 + t for t in range(3)}
        in_specs += [pl.BlockSpec(memory_space=pl.ANY)] * 3
        args += list(prev)
    h, C, n, m = pl.pallas_call(
        kern,
        grid=(nseq // G, nc),
        in_specs=in_specs,
        out_specs=[pl.BlockSpec((G, L, W), lambda b, c: (b, c, 0))] + st_specs,
        out_shape=[jax.ShapeDtypeStruct((nseq, T, W), BF16),
                   jax.ShapeDtypeStruct((depth, nseq, M_HEADS, dk, dk), F32),
                   jax.ShapeDtypeStruct((depth, nseq, M_HEADS, dk), F32),
                   jax.ShapeDtypeStruct((depth, nseq, 1, LANES), F32)],
        scratch_shapes=[pltpu.VMEM((G, M_HEADS, dk, dk), F32),
                        pltpu.VMEM((G, M_HEADS, dk), F32),
                        pltpu.VMEM((G, 1, LANES), F32)],
        input_output_aliases=aliases,
        compiler_params=_params(("parallel", "arbitrary")),
        name="mlstm",
    )(*args)
    return h.reshape(nseq * T, W), C, n, m


def _ssd_kernel(*refs, L, nheads, has_init, has_prev):
    z_ref, x_ref, g_ref, cw_ref, cb_ref, hp_ref, nw_ref = refs[:7]
    pos = 7
    if has_init:
        s0_ref, cv0_ref = refs[pos:pos + 2]
        pos += 2
    if has_prev:
        pos += 2
    y_ref, s_out, cv_out, s_s, xp_s, y_s = refs[pos:pos + 6]
    c = pl.program_id(1)
    nc = pl.num_programs(1)
    P = S_HEADDIM
    N = S_STATE
    hg = nheads // S_GROUPS
    W = nheads * P
    halo = SUBLANES
    kc = S_CONV

    G = z_ref.shape[0]

    @pl.when(c == 0)
    def _():
        xp_s[:, 0:halo, :] = jnp.zeros((G, halo, xp_s.shape[2]), F32)
        if has_init:
            s_s[...] = s0_ref[0]
            xp_s[:, halo - (kc - 1):halo, :] = cv0_ref[0]
        else:
            s_s[...] = jnp.zeros_like(s_s)

    hp = hp_ref[...]
    row = lax.broadcasted_iota(jnp.int32, (L, L), 0)
    col = lax.broadcasted_iota(jnp.int32, (L, L), 1)
    causal = col <= row
    tri = jnp.where(causal, 1.0, 0.0).astype(F32)
    sel = jnp.where(lax.broadcasted_iota(jnp.int32, (nheads, LANES), 0) + GATE_DT
                    == lax.broadcasted_iota(jnp.int32, (nheads, LANES), 1), 1.0, 0.0).astype(F32)
    gw = W // S_GROUPS

    seqs = range(G)
    heads = [(s_i, gi, hh) for s_i in seqs for gi in range(S_GROUPS) for hh in range(hg)]

    sq = []
    for s_i in seqs:
        xp_s[s_i, halo:halo + L, :] = x_ref[s_i].astype(F32)
        conv = cb_ref[...] + xp_s[s_i, halo:halo + L, :] * cw_ref[kc - 1:kc, :]
        for j in range(kc - 1):
            off = halo - (kc - 1) + j
            conv = conv + xp_s[s_i, off:off + L, :] * cw_ref[j:j + 1, :]
        tail = xp_s[s_i, L:L + halo, :]
        xp_s[s_i, 0:halo, :] = tail
        cv_out[0, s_i] = tail
        xbc = conv * _sigmoid(conv)
        dt = _softplus(g_ref[s_i] + hp[0:1, :])
        a = dt * (-jnp.exp(hp[1:2, :]))
        bcum = jnp.dot(tri, a, precision=HIGHEST, preferred_element_type=F32)
        b_last = bcum[L - 1:L, :]
        sq.append(dict(
            xs=xbc[:, 0:W], Bm=xbc[:, W:W + S_GROUPS * N], Cm=xbc[:, W + S_GROUPS * N:W + 2 * S_GROUPS * N],
            bcum=bcum, b_rows=_dot_nt(sel, bcum, precision=HIGHEST), dt_rows=_dot_nt(sel, dt, precision=HIGHEST),
            eb=jnp.exp(bcum), w_end=jnp.exp(b_last - bcum) * dt, dec_end=jnp.exp(b_last)))

    gq = {}
    for s_i in seqs:
        for gi in range(S_GROUPS):
            Bg = sq[s_i]["Bm"][:, gi * N:(gi + 1) * N].astype(BF16)
            Cg = sq[s_i]["Cm"][:, gi * N:(gi + 1) * N].astype(BF16)
            s_g = s_s[s_i, gi * hg * P:(gi + 1) * hg * P, :]
            gq[s_i, gi] = dict(Bg=Bg, cb=_dot_nt(Cg, Bg), y_inter=_dot_nt(Cg, s_g.astype(BF16)))

    mw_vregs = -(-L // (2 * SUBLANES)) * -(-L // LANES)
    batch = max(1, VREG_FILE // mw_vregs)
    for b0 in range(0, len(heads), batch):
        mws = {}
        for s_i, gi, hh in heads[b0:b0 + batch]:
            idx = gi * hg + hh
            ln = GATE_DT + idx
            e = sq[s_i]
            decay = jnp.exp(jnp.where(causal, e["bcum"][:, ln:ln + 1] - e["b_rows"][idx:idx + 1, :], NEG_BIG))
            mws[s_i, gi, hh] = (gq[s_i, gi]["cb"] * decay * e["dt_rows"][idx:idx + 1, :]).astype(BF16)

        for s_i, gi, hh in heads[b0:b0 + batch]:
            idx = gi * hg + hh
            ln = GATE_DT + idx
            e = sq[s_i]
            xs_h = e["xs"][:, idx * P:(idx + 1) * P]
            y_s[s_i, :, idx * P:(idx + 1) * P] = (
                _dot(mws[s_i, gi, hh], xs_h.astype(BF16))
                + e["eb"][:, ln:ln + 1] * gq[s_i, gi]["y_inter"][:, hh * P:(hh + 1) * P]
                + hp[2:3, ln:ln + 1] * xs_h)

    for s_i in seqs:
        e = sq[s_i]
        for gi in range(S_GROUPS):
            xw = jnp.concatenate(
                [(e["xs"][:, (gi * hg + hh) * P:(gi * hg + hh + 1) * P]
                  * e["w_end"][:, GATE_DT + gi * hg + hh:GATE_DT + gi * hg + hh + 1]).astype(BF16)
                 for hh in range(hg)], axis=1)
            upd = _dot_tn(xw, gq[s_i, gi]["Bg"])
            for hh in range(hg):
                ln = GATE_DT + gi * hg + hh
                r0 = (gi * hg + hh) * P
                s_s[s_i, r0:r0 + P, :] = (e["dec_end"][:, ln:ln + 1] * s_s[s_i, r0:r0 + P, :]
                                          + upd[hh * P:(hh + 1) * P, :])

    for s_i in seqs:
        for gi in range(S_GROUPS):
            sl = slice(gi * gw, (gi + 1) * gw)
            z = z_ref[s_i, :, sl].astype(F32)
            gg = y_s[s_i, :, sl] * (z * _sigmoid(z))
            gg = gg * lax.rsqrt(jnp.mean(gg * gg, axis=1, keepdims=True) + GN_EPS)
            y_ref[s_i, :, sl] = (gg * nw_ref[:, sl]).astype(y_ref.dtype)

    @pl.when(c == nc - 1)
    def _():
        s_out[0] = s_s[...]


def _ssd(z, xbc, gates, conv_w, conv_b, head_params, norm_w, init, prev, *, layer, depth, nseq, T, L):
    W = z.shape[1]
    nheads = W // S_HEADDIM
    CD = xbc.shape[1]
    nc = T // L
    G = _largest_divisor(nseq, SEQS_PER_STEP, 1) if L < PROMPT_CHUNK else 1
    z, xbc, gates = (a.reshape(nseq, T, a.shape[1]) for a in (z, xbc, gates))
    kern = functools.partial(_ssd_kernel, L=L, nheads=nheads, has_init=init is not None, has_prev=prev is not None)
    s_spec = pl.BlockSpec((1, G, nheads * S_HEADDIM, S_STATE), lambda b, c: (layer, b, 0, 0))
    in_specs = [pl.BlockSpec((G, L, W), lambda b, c: (b, c, 0)),
                pl.BlockSpec((G, L, CD), lambda b, c: (b, c, 0)),
                pl.BlockSpec((G, L, LANES), lambda b, c: (b, c, 0)),
                pl.BlockSpec((S_CONV, CD), lambda b, c: (0, 0)),
                pl.BlockSpec((1, CD), lambda b, c: (0, 0)),
                pl.BlockSpec((SUBLANES, LANES), lambda b, c: (0, 0)),
                pl.BlockSpec((1, W), lambda b, c: (0, 0))]
    args = [z, xbc, gates, conv_w, conv_b, head_params, norm_w]
    if init is not None:
        in_specs += [s_spec, pl.BlockSpec((1, G, S_CONV - 1, CD), lambda b, c: (layer, b, 0, 0))]
        args += list(init)
    aliases = {}
    if prev is not None:
        aliases = {len(args) + t: 1 + t for t in range(2)}
        in_specs += [pl.BlockSpec(memory_space=pl.ANY)] * 2
        args += list(prev)
    y, S, cv = pl.pallas_call(
        kern,
        grid=(nseq // G, nc),
        in_specs=in_specs,
        out_specs=[pl.BlockSpec((G, L, W), lambda b, c: (b, c, 0)),
                   s_spec,
                   pl.BlockSpec((1, G, SUBLANES, CD), lambda b, c: (layer, b, 0, 0))],
        out_shape=[jax.ShapeDtypeStruct((nseq, T, W), BF16),
                   jax.ShapeDtypeStruct((depth, nseq, nheads * S_HEADDIM, S_STATE), F32),
                   jax.ShapeDtypeStruct((depth, nseq, SUBLANES, CD), F32)],
        scratch_shapes=[pltpu.VMEM((G, nheads * S_HEADDIM, S_STATE), F32),
                        pltpu.VMEM((G, SUBLANES + L, CD), F32),
                        pltpu.VMEM((G, L, W), F32)],
        input_output_aliases=aliases,
        compiler_params=_params(("parallel", "arbitrary")),
        name="ssd",
    )(*args)
    return y.reshape(nseq * T, W), S, cv


def _outproj_kernel(hm_ref, hs_ref, wt_ref, wb_ref, x_ref, g_ref, b_ref, y_ref, yb_ref, *, alpha):
    mix = _dot(hm_ref[...], wt_ref[0]) + _dot(hs_ref[...], wb_ref[0])
    y = _layer_norm(alpha * x_ref[...] + mix, g_ref[...], b_ref[...])
    y_ref[...] = y
    yb_ref[...] = y.astype(BF16)


def _outproj(hm, hs, w_out, x, ln_g, ln_b, *, alpha):
    m, d = x.shape
    w = hm.shape[1]
    tm = _largest_divisor(m, 512, 16)
    kern = functools.partial(_outproj_kernel, alpha=alpha)
    return pl.pallas_call(
        kern,
        grid=(m // tm,),
        in_specs=[pl.BlockSpec((tm, w), lambda i: (i, 0)),
                  pl.BlockSpec((tm, w), lambda i: (i, 0)),
                  pl.BlockSpec((1, w, d), lambda i: (0, 0, 0)),
                  pl.BlockSpec((1, w, d), lambda i: (1, 0, 0)),
                  pl.BlockSpec((tm, d), lambda i: (i, 0)),
                  pl.BlockSpec((1, d), lambda i: (0, 0)),
                  pl.BlockSpec((1, d), lambda i: (0, 0))],
        out_specs=[pl.BlockSpec((tm, d), lambda i: (i, 0)),
                   pl.BlockSpec((tm, d), lambda i: (i, 0))],
        out_shape=[jax.ShapeDtypeStruct((m, d), F32),
                   jax.ShapeDtypeStruct((m, d), BF16)],
        compiler_params=_params(("parallel",)),
        name="out_proj_ln",
    )(hm, hs, w_out, w_out, x, ln_g, ln_b)


def _ffn_kernel(*refs, sh, hr, tps, rc, alpha, has_init):
    xb_ref, xf_ref, wu_ref, cw_ref, cb_ref, wd_ref = refs[:6]
    pos = 6
    if has_init:
        h0_ref = refs[pos]
        pos += 1
    lg_ref, lb_ref, y_ref, yb_ref, oh_ref, acc, ext, act, hal = refs[pos:pos + 9]
    i = pl.program_id(0)
    j = pl.program_id(1)
    nj = pl.num_programs(1)
    tm = xb_ref.shape[0]
    tf = wd_ref.shape[1]
    first = (i % tps) == 0

    @pl.when(first)
    def _():
        if has_init:
            ext[0:hr, :] = h0_ref[0, 0]
        else:
            ext[0:hr, :] = jnp.zeros((hr, ext.shape[1]), F32)

    @pl.when(jnp.logical_not(first))
    def _():
        ext[0:hr, :] = hal[j]

    ext[hr:hr + tm, :] = _dot(xb_ref[...], wu_ref[0])
    last = ext[tm:tm + hr, :]
    hal[j] = last
    oh_ref[0, 0] = last

    for r0 in range(0, tm, rc):
        c = (ext[hr - 2 * sh + r0:hr - 2 * sh + r0 + rc, :] * cw_ref[0, 0:1, :]
             + ext[hr - sh + r0:hr - sh + r0 + rc, :] * cw_ref[0, 1:2, :]
             + ext[hr + r0:hr + r0 + rc, :] * cw_ref[0, 2:3, :] + cb_ref[0])
        cg = c[:, :tf]
        act[r0:r0 + rc, :] = (cg * _sigmoid(cg) * c[:, tf:]).astype(BF16)

    @pl.when(j == 0)
    def _():
        acc[...] = jnp.zeros_like(acc)

    acc[...] += _dot(act[...], wd_ref[0])

    @pl.when(j == nj - 1)
    def _():
        y = _layer_norm(alpha * xf_ref[...] + acc[...], lg_ref[...], lb_ref[...])
        y_ref[...] = y
        yb_ref[...] = y.astype(BF16)


def _ffn(xb, xf, w_u, cw, cb, w_d, h0, ln_g, ln_b, *, nseq, tm, sh, hr, alpha, single_buffer):
    m, d = xf.shape
    nj, tf = w_d.shape[0], w_d.shape[1]
    ni = m // tm
    tps = ni // nseq
    kern = functools.partial(_ffn_kernel, sh=sh, hr=hr, tps=tps, rc=FFN_ROW_CHUNK, alpha=alpha,
                             has_init=h0 is not None)
    mode = dict(pipeline_mode=pl.Buffered(1)) if single_buffer else {}
    nbuf = 1 if single_buffer else 2
    vmem_bytes = (nbuf * tm * d * (2 + 4 + 4 + 2)
                  + 2 * (d * 2 * tf * 2 + tf * d * 2)
                  + 2 * 2 * hr * 2 * tf * 4
                  + tm * d * 4 + (hr + tm) * 2 * tf * 4 + tm * tf * 2 + nj * hr * 2 * tf * 4
                  + FFN_COMPILER_SCRATCH_BYTES)
    in_specs = [pl.BlockSpec((tm, d), lambda i, j: (i, 0), **mode),
                pl.BlockSpec((tm, d), lambda i, j: (i, 0), **mode),
                pl.BlockSpec((1, d, 2 * tf), lambda i, j: (j, 0, 0)),
                pl.BlockSpec((1, FFN_CONV, 2 * tf), lambda i, j: (j, 0, 0)),
                pl.BlockSpec((1, 1, 2 * tf), lambda i, j: (j, 0, 0)),
                pl.BlockSpec((1, tf, d), lambda i, j: (j, 0, 0))]
    args = [xb, xf, w_u, cw, cb, w_d]
    if h0 is not None:
        in_specs.append(pl.BlockSpec((1, 1, hr, 2 * tf), lambda i, j: (i // tps, j, 0, 0)))
        args.append(h0)
    in_specs += [pl.BlockSpec((1, d), lambda i, j: (0, 0))] * 2
    args += [ln_g, ln_b]
    return pl.pallas_call(
        kern,
        grid=(ni, nj),
        in_specs=in_specs,
        out_specs=[pl.BlockSpec((tm, d), lambda i, j: (i, 0), **mode),
                   pl.BlockSpec((tm, d), lambda i, j: (i, 0), **mode),
                   pl.BlockSpec((1, 1, hr, 2 * tf),
                                lambda i, j: (i // tps, jnp.where(i % tps == tps - 1, j, 0), 0, 0))],
        out_shape=[jax.ShapeDtypeStruct((m, d), F32),
                   jax.ShapeDtypeStruct((m, d), BF16),
                   jax.ShapeDtypeStruct((nseq, nj, hr, 2 * tf), F32)],
        scratch_shapes=[pltpu.VMEM((tm, d), F32),
                        pltpu.VMEM((hr + tm, 2 * tf), F32),
                        pltpu.VMEM((tm, tf), BF16),
                        pltpu.VMEM((nj, hr, 2 * tf), F32)],
        compiler_params=_params(("arbitrary", "arbitrary"), vmem_limit_bytes=vmem_bytes),
        name="conv_ffn_ln",
    )(*args)


def _ffn_tiles(g, v, tf):
    nj = g.shape[-1] // tf
    lead = g.shape[:-1]
    t = jnp.stack([g.reshape(lead + (nj, tf)), v.reshape(lead + (nj, tf))], axis=-2)
    return jnp.moveaxis(t.reshape(lead + (nj, 2 * tf)), -2, 0)


def _ffn_untile(t, tf):
    nj = t.shape[-3]
    t = jnp.moveaxis(t, -3, -2)
    lead = t.shape[:-2]
    return (t[..., :tf].reshape(lead + (nj * tf,)), t[..., tf:].reshape(lead + (nj * tf,)))


def _tile_up_kernel(w_ref, o_ref, *, dff, tf):
    nj = o_ref.shape[0]
    for j in range(nj):
        lo = j * tf
        width = min(tf, dff - lo)
        for half, base in ((0, 0), (1, dff)):
            o_ref[j, :, half * tf:half * tf + width] = w_ref[0, :, base + lo:base + lo + width].astype(BF16)
            if width < tf:
                o_ref[j, :, half * tf + width:(half + 1) * tf] = jnp.zeros((o_ref.shape[1], tf - width), BF16)


def _tile_up(w_up, layer, tf):
    _, d, two_dff = w_up.shape
    dff = two_dff // 2
    assert dff % LANES == 0
    nj = -(-dff // tf)
    tr = _largest_divisor(d, 128, 16)
    return pl.pallas_call(
        functools.partial(_tile_up_kernel, dff=dff, tf=tf),
        grid=(d // tr,),
        in_specs=[pl.BlockSpec((1, tr, two_dff), lambda r: (layer, r, 0))],
        out_specs=pl.BlockSpec((nj, tr, 2 * tf), lambda r: (0, r, 0)),
        out_shape=jax.ShapeDtypeStruct((nj, d, 2 * tf), BF16),
        compiler_params=_params(("parallel",)),
        name="tile_w_up",
    )(w_up)


def _tile_down_kernel(w_ref, o_ref, *, dff):
    j = pl.program_id(0)
    tf = o_ref.shape[1]
    rows = lax.broadcasted_iota(jnp.int32, o_ref.shape[1:], 0) + j * tf
    o_ref[0] = jnp.where(rows < dff, w_ref[0], 0.0).astype(BF16)


def _tile_down(w_down, layer, tf):
    _, dff, d = w_down.shape
    nj = -(-dff // tf)
    return pl.pallas_call(
        functools.partial(_tile_down_kernel, dff=dff),
        grid=(nj,),
        in_specs=[pl.BlockSpec((1, tf, d), lambda j: (layer, j, 0))],
        out_specs=pl.BlockSpec((1, tf, d), lambda j: (j, 0, 0)),
        out_shape=jax.ShapeDtypeStruct((nj, tf, d), BF16),
        compiler_params=_params(("parallel",)),
        name="tile_w_down",
    )(w_down)


def _prep_layer(l, w_in, b_i, b_f, m_norm_w, s_conv_w, s_conv_b, dt_bias, A_log, D_skip, s_norm_w, w_out,
                ln1_g, ln1_b, w_up, f_conv_w, f_conv_b, w_down, ln2_g, ln2_b):
    mw = m_norm_w.shape[1]
    sw = s_norm_w.shape[1]
    cd = s_conv_w.shape[2]
    nh = dt_bias.shape[1]
    dff = w_down.shape[1]
    tf = FFN_TILE
    fp = -(-dff // tf) * tf
    o = 0
    cols = {}
    for name, size in (("qkv", 3 * mw), ("o", mw), ("i", M_HEADS), ("f", M_HEADS),
                       ("z", sw), ("xbc", cd), ("dt", nh)):
        cols[name] = (o, o + size)
        o += size
    wl = w_in[l]
    cut = lambda name: wl[:, cols[name][0]:cols[name][1]]
    p = {}
    wg = jnp.concatenate([cut("i"), cut("f"), cut("dt")], axis=1)
    p["w_in"] = [cut("qkv").astype(BF16), cut("o").astype(BF16), cut("z").astype(BF16), cut("xbc").astype(BF16),
                 jnp.pad(wg, ((0, 0), (0, LANES - wg.shape[1]))).astype(BF16)]
    gb = jnp.concatenate([b_i[l], b_f[l]])
    p["gate_bias"] = jnp.pad(gb, (0, LANES - gb.shape[0]))[None, :]
    p["m_norm_w"] = m_norm_w[l][None, :]
    p["s_conv_w"] = s_conv_w[l]
    p["s_conv_b"] = s_conv_b[l][None, :]
    hp = jnp.stack([dt_bias[l], A_log[l], D_skip[l]])
    p["head_params"] = jnp.pad(hp, ((0, SUBLANES - 3), (GATE_DT, LANES - GATE_DT - nh)))
    p["s_norm_w"] = s_norm_w[l][None, :]
    p["w_out"] = w_out[l].astype(BF16).reshape(2, mw, w_out.shape[2])
    p["ln1_g"] = ln1_g[l][None, :]
    p["ln1_b"] = ln1_b[l][None, :]
    padc = lambda a: jnp.pad(a, ((0, 0), (0, fp - dff)))
    p["w_up"] = _tile_up(w_up, l, tf)
    p["f_conv_w"] = _ffn_tiles(padc(f_conv_w[l][:, :dff]), padc(f_conv_w[l][:, dff:]), tf)
    p["f_conv_b"] = _ffn_tiles(padc(f_conv_b[l][None, :dff]), padc(f_conv_b[l][None, dff:]), tf)
    p["w_down"] = _tile_down(w_down, l, tf)
    p["ln2_g"] = ln2_g[l][None, :]
    p["ln2_b"] = ln2_b[l][None, :]
    p["dff"] = dff
    p["fp"] = fp
    return p


def _layer(xf, xb, p, init, prev, *, layer, depth, nseq, T, L, time_major_ffn, alpha):
    qkv, o_pre, z, xbc, gates = _in_proj(xb, p["w_in"], [BF16, BF16, BF16, BF16, F32], tm_target=1024)
    hm, C, n, m = _mlstm(qkv, o_pre, gates, p["gate_bias"], p["m_norm_w"],
                         None if init is None else init["mlstm"], None if prev is None else prev["mlstm"],
                         layer=layer, depth=depth, nseq=nseq, T=T, L=L)
    hs, S, sconv = _ssd(z, xbc, gates, p["s_conv_w"], p["s_conv_b"], p["head_params"], p["s_norm_w"],
                        None if init is None else init["ssd"], None if prev is None else prev["ssd"],
                        layer=layer, depth=depth, nseq=nseq, T=T, L=L)
    x1f, x1b = _outproj(hm, hs, p["w_out"], xf, p["ln1_g"], p["ln1_b"], alpha=alpha)
    d = xf.shape[1]
    dff = p["dff"]
    ffn_w = (p["w_up"], p["f_conv_w"], p["f_conv_b"], p["w_down"])
    h0 = None if init is None else init["ffn"]
    if time_major_ffn:
        tr = lambda a: a.reshape(nseq, T, d).transpose(1, 0, 2).reshape(nseq * T, d)
        x1f_t, x1b_t = tr(x1f), tr(x1b)
        tm = _largest_divisor(nseq * T, 512, (FFN_CONV - 1) * nseq)
        y_f, y_b, oh = _ffn(x1b_t, x1f_t, *ffn_w, h0, p["ln2_g"], p["ln2_b"], nseq=1, tm=tm, sh=nseq,
                            hr=(FFN_CONV - 1) * nseq, alpha=alpha, single_buffer=True)
        tb = lambda a: a.reshape(T, nseq, d).transpose(1, 0, 2).reshape(nseq * T, d)
        y_f, y_b = tb(y_f), tb(y_b)
        og, ov = _ffn_untile(oh, FFN_TILE)
        fconv = jnp.concatenate([og[0, :, :dff], ov[0, :, :dff]], axis=1)
        fconv = fconv.reshape(FFN_CONV - 1, nseq, 2 * dff).transpose(1, 0, 2)
    else:
        tm = _largest_divisor(T, 512, FFN_ROW_CHUNK)
        y_f, y_b, oh = _ffn(x1b, x1f, *ffn_w, h0, p["ln2_g"], p["ln2_b"],
                            nseq=nseq, tm=tm, sh=1, hr=SUBLANES, alpha=alpha, single_buffer=False)
        og, ov = _ffn_untile(oh, FFN_TILE)
        k = FFN_CONV - 1
        fconv = jnp.concatenate([og[:, SUBLANES - k:, :dff], ov[:, SUBLANES - k:, :dff]], axis=2)
    return y_f, y_b, dict(mlstm=(C, n, m), ssd=(S, sconv)), fconv


def kernel(x_prompt, x_sample, state_mlstm_C, state_mlstm_n, state_mlstm_m, state_ssm, state_ssm_conv,
           state_ffn_conv, w_in, mlstm_b_i, mlstm_b_f, mlstm_norm_w, ssm_conv_w, ssm_conv_b, ssm_dt_bias,
           ssm_A_log, ssm_D, ssm_norm_w, w_out, ln1_g, ln1_b, ffn_w_up, ffn_conv_w, ffn_conv_b,
           ffn_w_down, ln2_g, ln2_b):
    weights = (w_in, mlstm_b_i, mlstm_b_f, mlstm_norm_w, ssm_conv_w, ssm_conv_b, ssm_dt_bias, ssm_A_log,
               ssm_D, ssm_norm_w, w_out, ln1_g, ln1_b, ffn_w_up, ffn_conv_w, ffn_conv_b, ffn_w_down,
               ln2_g, ln2_b)
    depth = w_in.shape[0]
    alpha = float((2 * depth) ** 0.25)
    B, T, D = x_prompt.shape
    Bs, Ts, _ = x_sample.shape
    nh = ssm_dt_bias.shape[1]
    Lp = PROMPT_CHUNK if T % PROMPT_CHUNK == 0 else T
    Ls = PROMPT_CHUNK if Ts % PROMPT_CHUNK == 0 else Ts

    xpf = x_prompt.reshape(B * T, D)
    xpb = xpf.astype(BF16)
    xsf = x_sample.reshape(Bs * Ts, D)
    xsb = xsf.astype(BF16)
    m_in = jnp.pad(state_mlstm_m, ((0, 0), (0, 0), (GATE_F, LANES - GATE_F - M_HEADS)))[:, :, None, :]
    s_in = state_ssm.reshape(depth, Bs, nh * S_HEADDIM, S_STATE)
    p_prev, s_prev, p_fconv, s_fconv = None, None, [], []
    for l in range(depth):
        p = _prep_layer(l, *weights)
        fp, dff = p["fp"], p["dff"]
        xpf, xpb, p_prev, fc = _layer(xpf, xpb, p, None, p_prev, layer=l, depth=depth, nseq=B, T=T, L=Lp,
                                      time_major_ffn=False, alpha=alpha)
        p_fconv.append(fc)
        fc0 = state_ffn_conv[l].transpose(1, 0, 2).reshape((FFN_CONV - 1) * Bs, 2 * dff)
        padf = lambda a: jnp.pad(a, ((0, 0), (0, fp - dff)))
        init = dict(mlstm=(state_mlstm_C, state_mlstm_n, m_in), ssd=(s_in, state_ssm_conv),
                    ffn=_ffn_tiles(padf(fc0[:, :dff]), padf(fc0[:, dff:]), FFN_TILE)[None])
        xsf, xsb, s_prev, fc = _layer(xsf, xsb, p, init, s_prev, layer=l, depth=depth, nseq=Bs, T=Ts, L=Ls,
                                      time_major_ffn=True, alpha=alpha)
        s_fconv.append(fc)

    def unpack(st, nseq):
        C, n, m = st["mlstm"]
        S, sconv = st["ssd"]
        return (C, n, m[:, :, 0, GATE_F:GATE_F + M_HEADS], S.reshape(depth, nseq, nh, S_HEADDIM, S_STATE),
                sconv[:, :, SUBLANES - (S_CONV - 1):, :])

    pC, pn, pm, pS, psc = unpack(p_prev, B)
    sC, sn, sm, sS, ssc = unpack(s_prev, Bs)
    return (xpf.reshape(B, T, D), xsf.reshape(Bs, Ts, D), pC, pn, pm, pS, psc, jnp.stack(p_fconv),
            sC, sn, sm, sS, ssc, jnp.stack(s_fconv))
```

```python
import functools

import jax
import jax.numpy as jnp
from jax import lax
from jax.experimental import pallas as pl
from jax.experimental.pallas import tpu as pltpu

F32 = jnp.float32
BF16 = jnp.bfloat16
HIGHEST = lax.Precision.HIGHEST

M_HEADS = 4
S_HEADDIM = 64
S_GROUPS = 2
S_STATE = 128
S_CONV = 4
FFN_CONV = 3
LN_EPS = 1e-5
GN_EPS = 1e-6
NEG_BIG = -1e30

LANES = 128
SUBLANES = 8
MXU_DIM = 256
VMEM_LIMIT_BYTES = 52 * 1024 * 1024

GATE_I = 0
GATE_F = 4
GATE_DT = 8

PROMPT_CHUNK = 128
SEQS_PER_STEP = 8
VREG_FILE = 64
IN_PROJ_TILE = 2 * MXU_DIM
FFN_TILE = 2 * MXU_DIM
FFN_ROW_CHUNK = 32
FFN_COMPILER_SCRATCH_BYTES = 8 * 1024 * 1024


def _largest_divisor(n, target, mult):
    best = None
    for d in range(mult, min(n, target) + 1, mult):
        if n % d == 0:
            best = d
    if best is None:
        raise ValueError(f"no tile for {n=} {target=} {mult=}")
    return best


def _params(sem, vmem_limit_bytes=VMEM_LIMIT_BYTES):
    return pltpu.CompilerParams(dimension_semantics=sem, vmem_limit_bytes=vmem_limit_bytes)


def _dot(a, b):
    return jnp.dot(a, b, preferred_element_type=F32)


def _dot_nt(a, b, precision=None):
    return lax.dot_general(a, b, (((1,), (1,)), ((), ())), precision=precision, preferred_element_type=F32)


def _dot_tn(a, b):
    return lax.dot_general(a, b, (((0,), (0,)), ((), ())), preferred_element_type=F32)


def _sigmoid(x):
    return 1.0 / (1.0 + jnp.exp(-x))


def _softplus(x):
    return jnp.maximum(x, 0.0) + jnp.log1p(jnp.exp(-jnp.abs(x)))


def _layer_norm(r, g, b):
    mu = jnp.mean(r, axis=-1, keepdims=True)
    d = r - mu
    var = jnp.mean(d * d, axis=-1, keepdims=True)
    return d * lax.rsqrt(var + LN_EPS) * g + b


def _in_proj_kernel(*refs, groups):
    x_ref = refs[0]
    ng = len(groups)
    w_refs = refs[1:1 + ng]
    o_refs = refs[1 + ng:1 + 2 * ng]
    j = pl.program_id(1)
    for (start, count), w_ref, o_ref in zip(groups, w_refs, o_refs):
        @pl.when(jnp.logical_and(j >= start, j < start + count))
        def _(w_ref=w_ref, o_ref=o_ref):
            o_ref[...] = _dot(x_ref[...], w_ref[...]).astype(o_ref.dtype)


def _in_proj(x, weights, out_dtypes, tm_target):
    m, k = x.shape
    tm = _largest_divisor(m, tm_target, 16)
    tiles, groups, start = [], [], 0
    for w in weights:
        n = w.shape[1]
        tn = _largest_divisor(n, IN_PROJ_TILE, LANES)
        tiles.append(tn)
        groups.append((start, n // tn))
        start += n // tn

    def clamp(s, c):
        return lambda i, j: (0, jnp.clip(j - s, 0, c - 1))

    def clamp_out(s, c):
        return lambda i, j: (i, jnp.clip(j - s, 0, c - 1))

    one = lambda c: dict(pipeline_mode=pl.Buffered(1)) if c == 1 else {}
    in_specs = [pl.BlockSpec((tm, k), lambda i, j: (i, 0), pipeline_mode=pl.Buffered(1))]
    in_specs += [pl.BlockSpec((k, tn), clamp(s, c), **one(c)) for tn, (s, c) in zip(tiles, groups)]
    out_specs = [pl.BlockSpec((tm, tn), clamp_out(s, c)) for tn, (s, c) in zip(tiles, groups)]
    out_shape = [jax.ShapeDtypeStruct((m, w.shape[1]), dt) for w, dt in zip(weights, out_dtypes)]
    return pl.pallas_call(
        functools.partial(_in_proj_kernel, groups=tuple(groups)),
        grid=(m // tm, start),
        in_specs=in_specs,
        out_specs=out_specs,
        out_shape=out_shape,
        compiler_params=_params(("parallel", "arbitrary")),
        name="in_proj",
    )(x, *weights)


def _mlstm_kernel(*refs, L, dk, nc, has_init, has_prev):
    q_ref, k_ref, v_ref, o_ref, g_ref, gb_ref, nw_ref = refs[:7]
    pos = 7
    if has_init:
        c0_ref, n0_ref, m0_ref = refs[pos:pos + 3]
        pos += 3
    if has_prev:
        pos += 3
    h_ref, c_out, n_out, m_out, c_s, n_s, m_s = refs[pos:pos + 7]
    c = pl.program_id(1)

    @pl.when(c == 0)
    def _():
        if has_init:
            c_s[...] = c0_ref[0]
            n_s[...] = n0_ref[0]
            m_s[...] = m0_ref[0]
        else:
            c_s[...] = jnp.zeros_like(c_s)
            n_s[...] = jnp.zeros_like(n_s)
            m_s[...] = jnp.zeros_like(m_s)

    row = lax.broadcasted_iota(jnp.int32, (L, L), 0)
    col = lax.broadcasted_iota(jnp.int32, (L, L), 1)
    causal = col <= row
    tri = jnp.where(causal, 1.0, 0.0).astype(F32)
    lane = lax.broadcasted_iota(jnp.int32, (L, LANES), 1)
    sel = jnp.where(lax.broadcasted_iota(jnp.int32, (SUBLANES, LANES), 0)
                    == lax.broadcasted_iota(jnp.int32, (SUBLANES, LANES), 1), 1.0, 0.0).astype(F32)
    m_lane = lax.broadcasted_iota(jnp.int32, (1, LANES), 1)
    k_scale = dk ** -0.5

    nsq = q_ref.shape[0]
    pairs = [(s_i, h) for s_i in range(nsq) for h in range(M_HEADS)]
    hsl = lambda h: slice(h * dk, (h + 1) * dk)


    gs, bcums, rowss, m_vecs = [], [], [], []
    for s_i in range(nsq):
        g = g_ref[s_i] + gb_ref[...]
        lf = jnp.minimum(g, 0.0) - jnp.log1p(jnp.exp(-jnp.abs(g)))
        bcum = jnp.dot(tri, lf, precision=HIGHEST, preferred_element_type=F32)
        packed = jnp.where(lane < GATE_F, g, bcum)
        gs.append(g)
        bcums.append(bcum)
        rowss.append(_dot_nt(sel, packed, precision=HIGHEST))
        m_vecs.append(m_s[s_i])

    st = []
    for s_i, h in pairs:
        bc = bcums[s_i][:, GATE_F + h:GATE_F + h + 1]
        igr = rowss[s_i][GATE_I + h:GATE_I + h + 1, :]
        br = rowss[s_i][GATE_F + h:GATE_F + h + 1, :]
        m_h = m_vecs[s_i][:, GATE_F + h:GATE_F + h + 1]
        log_w = jnp.where(causal, bc - br + igr, NEG_BIG)
        log_inter = bc + m_h
        m_t = jnp.maximum(log_inter, jnp.max(log_w, axis=1, keepdims=True))
        st.append(dict(bc=bc, m_h=m_h, m_t=m_t, w_intra=jnp.exp(log_w - m_t), w_inter=jnp.exp(log_inter - m_t)))

    for (s_i, h), e in zip(pairs, st):
        q = q_ref[s_i, :, hsl(h)]
        kf = k_ref[s_i, :, hsl(h)].astype(F32) * k_scale
        v = v_ref[s_i, :, hsl(h)]
        s = _dot_nt(q, kf.astype(BF16)) * e["w_intra"]
        num = _dot(s.astype(BF16), v) + e["w_inter"] * _dot(q, c_s[s_i, h].astype(BF16))
        qn = jnp.sum(q.astype(F32) * n_s[s_i, h:h + 1, :], axis=1, keepdims=True)
        den = jnp.sum(s, axis=1, keepdims=True) + e["w_inter"] * qn
        e["hh"] = num / jnp.maximum(jnp.abs(den), jnp.exp(-e["m_t"]))

    m_news = list(m_vecs)
    for (s_i, h), e in zip(pairs, st):
        bc, m_h, m_t = e["bc"], e["m_h"], e["m_t"]
        igc = gs[s_i][:, GATE_I + h:GATE_I + h + 1]
        m_end = m_t[L - 1:L, :]
        b_end = bc[L - 1:L, :]
        w_end = jnp.exp(b_end - bc + igc - m_end)
        decay = jnp.exp(b_end + m_h - m_end)
        kw = k_ref[s_i, :, hsl(h)].astype(F32) * k_scale * w_end
        c_s[s_i, h] = decay * c_s[s_i, h] + _dot_tn(kw.astype(BF16), v_ref[s_i, :, hsl(h)])
        n_s[s_i, h:h + 1, :] = decay * n_s[s_i, h:h + 1, :] + jnp.sum(kw, axis=0, keepdims=True)
        m_news[s_i] = jnp.where(m_lane == GATE_F + h, m_end, m_news[s_i])
    for s_i in range(nsq):
        m_s[s_i] = m_news[s_i]

    for (s_i, h), e in zip(pairs, st):
        hh = e["hh"]
        mu = jnp.mean(hh, axis=1, keepdims=True)
        d = hh - mu
        var = jnp.mean(d * d, axis=1, keepdims=True)
        hn = d * lax.rsqrt(var + GN_EPS) * nw_ref[:, hsl(h)]
        h_ref[s_i, :, hsl(h)] = (hn * _sigmoid(o_ref[s_i, :, hsl(h)].astype(F32))).astype(h_ref.dtype)

    @pl.when(c == nc - 1)
    def _():
        c_out[0] = c_s[...]
        n_out[0] = n_s[...]
        m_out[0] = m_s[...]


def _mlstm(qkv, o_pre, gates, gate_bias, norm_w, init, prev, *, layer, depth, nseq, T, L):
    W = qkv.shape[1] // 3
    dk = W // M_HEADS
    nc = T // L
    G = _largest_divisor(nseq, SEQS_PER_STEP, 1)
    qkv, o_pre, gates = (a.reshape(nseq, T, a.shape[1]) for a in (qkv, o_pre, gates))
    kern = functools.partial(_mlstm_kernel, L=L, dk=dk, nc=nc, has_init=init is not None,
                             has_prev=prev is not None)
    st_specs = [pl.BlockSpec((1, G, M_HEADS, dk, dk), lambda b, c: (layer, b, 0, 0, 0)),
                pl.BlockSpec((1, G, M_HEADS, dk), lambda b, c: (layer, b, 0, 0)),
                pl.BlockSpec((1, G, 1, LANES), lambda b, c: (layer, b, 0, 0))]
    in_specs = [pl.BlockSpec((G, L, W), lambda b, c: (b, c, 0)),
                pl.BlockSpec((G, L, W), lambda b, c: (b, c, 1)),
                pl.BlockSpec((G, L, W), lambda b, c: (b, c, 2)),
                pl.BlockSpec((G, L, W), lambda b, c: (b, c, 0)),
                pl.BlockSpec((G, L, LANES), lambda b, c: (b, c, 0)),
                pl.BlockSpec((1, LANES), lambda b, c: (0, 0)),
                pl.BlockSpec((1, W), lambda b, c: (0, 0))]
    args = [qkv, qkv, qkv, o_pre, gates, gate_bias, norm_w]
    if init is not None:
        in_specs += st_specs
        args += list(init)
    aliases = {}
    if prev is not None:
        aliases = {len(args) + t: 1 + t for t in range(3)}
        in_specs += [pl.BlockSpec(memory_space=pl.ANY)] * 3
        args += list(prev)
    h, C, n, m = pl.pallas_call(
        kern,
        grid=(nseq // G, nc),
        in_specs=in_specs,
        out_specs=[pl.BlockSpec((G, L, W), lambda b, c: (b, c, 0))] + st_specs,
        out_shape=[jax.ShapeDtypeStruct((nseq, T, W), BF16),
                   jax.ShapeDtypeStruct((depth, nseq, M_HEADS, dk, dk), F32),
                   jax.ShapeDtypeStruct((depth, nseq, M_HEADS, dk), F32),
                   jax.ShapeDtypeStruct((depth, nseq, 1, LANES), F32)],
        scratch_shapes=[pltpu.VMEM((G, M_HEADS, dk, dk), F32),
                        pltpu.VMEM((G, M_HEADS, dk), F32),
                        pltpu.VMEM((G, 1, LANES), F32)],
        input_output_aliases=aliases,
        compiler_params=_params(("parallel", "arbitrary")),
        name="mlstm",
    )(*args)
    return h.reshape(nseq * T, W), C, n, m


def _ssd_kernel(*refs, L, nheads, has_init, has_prev):
    z_ref, x_ref, g_ref, cw_ref, cb_ref, hp_ref, nw_ref = refs[:7]
    pos = 7
    if has_init:
        s0_ref, cv0_ref = refs[pos:pos + 2]
        pos += 2
    if has_prev:
        pos += 2
    y_ref, s_out, cv_out, s_s, xp_s, y_s = refs[pos:pos + 6]
    c = pl.program_id(1)
    nc = pl.num_programs(1)
    P = S_HEADDIM
    N = S_STATE
    hg = nheads // S_GROUPS
    W = nheads * P
    halo = SUBLANES
    kc = S_CONV

    G = z_ref.shape[0]

    @pl.when(c == 0)
    def _():
        xp_s[:, 0:halo, :] = jnp.zeros((G, halo, xp_s.shape[2]), F32)
        if has_init:
            s_s[...] = s0_ref[0]
            xp_s[:, halo - (kc - 1):halo, :] = cv0_ref[0]
        else:
            s_s[...] = jnp.zeros_like(s_s)

    hp = hp_ref[...]
    row = lax.broadcasted_iota(jnp.int32, (L, L), 0)
    col = lax.broadcasted_iota(jnp.int32, (L, L), 1)
    causal = col <= row
    tri = jnp.where(causal, 1.0, 0.0).astype(F32)
    sel = jnp.where(lax.broadcasted_iota(jnp.int32, (nheads, LANES), 0) + GATE_DT
                    == lax.broadcasted_iota(jnp.int32, (nheads, LANES), 1), 1.0, 0.0).astype(F32)
    gw = W // S_GROUPS

    seqs = range(G)
    heads = [(s_i, gi, hh) for s_i in seqs for gi in range(S_GROUPS) for hh in range(hg)]

    sq = []
    for s_i in seqs:
        xp_s[s_i, halo:halo + L, :] = x_ref[s_i].astype(F32)
        conv = cb_ref[...] + xp_s[s_i, halo:halo + L, :] * cw_ref[kc - 1:kc, :]
        for j in range(kc - 1):
            off = halo - (kc - 1) + j
            conv = conv + xp_s[s_i, off:off + L, :] * cw_ref[j:j + 1, :]
        tail = xp_s[s_i, L:L + halo, :]
        xp_s[s_i, 0:halo, :] = tail
        cv_out[0, s_i] = tail
        xbc = conv * _sigmoid(conv)
        dt = _softplus(g_ref[s_i] + hp[0:1, :])
        a = dt * (-jnp.exp(hp[1:2, :]))
        bcum = jnp.dot(tri, a, precision=HIGHEST, preferred_element_type=F32)
        b_last = bcum[L - 1:L, :]
        sq.append(dict(
            xs=xbc[:, 0:W], Bm=xbc[:, W:W + S_GROUPS * N], Cm=xbc[:, W + S_GROUPS * N:W + 2 * S_GROUPS * N],
            bcum=bcum, b_rows=_dot_nt(sel, bcum, precision=HIGHEST), dt_rows=_dot_nt(sel, dt, precision=HIGHEST),
            eb=jnp.exp(bcum), w_end=jnp.exp(b_last - bcum) * dt, dec_end=jnp.exp(b_last)))

    gq = {}
    for s_i in seqs:
        for gi in range(S_GROUPS):
            Bg = sq[s_i]["Bm"][:, gi * N:(gi + 1) * N].astype(BF16)
            Cg = sq[s_i]["Cm"][:, gi * N:(gi + 1) * N].astype(BF16)
            s_g = s_s[s_i, gi * hg * P:(gi + 1) * hg * P, :]
            gq[s_i, gi] = dict(Bg=Bg, cb=_dot_nt(Cg, Bg), y_inter=_dot_nt(Cg, s_g.astype(BF16)))

    mw_vregs = -(-L // (2 * SUBLANES)) * -(-L // LANES)
    batch = max(1, VREG_FILE // mw_vregs)
    for b0 in range(0, len(heads), batch):
        mws = {}
        for s_i, gi, hh in heads[b0:b0 + batch]:
            idx = gi * hg + hh
            ln = GATE_DT + idx
            e = sq[s_i]
            decay = jnp.exp(jnp.where(causal, e["bcum"][:, ln:ln + 1] - e["b_rows"][idx:idx + 1, :], NEG_BIG))
            mws[s_i, gi, hh] = (gq[s_i, gi]["cb"] * decay * e["dt_rows"][idx:idx + 1, :]).astype(BF16)

        for s_i, gi, hh in heads[b0:b0 + batch]:
            idx = gi * hg + hh
            ln = GATE_DT + idx
            e = sq[s_i]
            xs_h = e["xs"][:, idx * P:(idx + 1) * P]
            y_s[s_i, :, idx * P:(idx + 1) * P] = (
                _dot(mws[s_i, gi, hh], xs_h.astype(BF16))
                + e["eb"][:, ln:ln + 1] * gq[s_i, gi]["y_inter"][:, hh * P:(hh + 1) * P]
                + hp[2:3, ln:ln + 1] * xs_h)

    for s_i in seqs:
        e = sq[s_i]
        for gi in range(S_GROUPS):
            xw = jnp.concatenate(
                [(e["xs"][:, (gi * hg + hh) * P:(gi * hg + hh + 1) * P]
                  * e["w_end"][:, GATE_DT + gi * hg + hh:GATE_DT + gi * hg + hh + 1]).astype(BF16)
                 for hh in range(hg)], axis=1)
            upd = _dot_tn(xw, gq[s_i, gi]["Bg"])
            for hh in range(hg):
                ln = GATE_DT + gi * hg + hh
                r0 = (gi * hg + hh) * P
                s_s[s_i, r0:r0 + P, :] = (e["dec_end"][:, ln:ln + 1] * s_s[s_i, r0:r0 + P, :]
                                          + upd[hh * P:(hh + 1) * P, :])

    for s_i in seqs:
        for gi in range(S_GROUPS):
            sl = slice(gi * gw, (gi + 1) * gw)
            z = z_ref[s_i, :, sl].astype(F32)
            gg = y_s[s_i, :, sl] * (z * _sigmoid(z))
            gg = gg * lax.rsqrt(jnp.mean(gg * gg, axis=1, keepdims=True) + GN_EPS)
            y_ref[s_i, :, sl] = (gg * nw_ref[:, sl]).astype(y_ref.dtype)

    @pl.when(c == nc - 1)
    def _():
        s_out[0] = s_s[...]


def _ssd(z, xbc, gates, conv_w, conv_b, head_params, norm_w, init, prev, *, layer, depth, nseq, T, L):
    W = z.shape[1]
    nheads = W // S_HEADDIM
    CD = xbc.shape[1]
    nc = T // L
    G = _largest_divisor(nseq, SEQS_PER_STEP, 1) if L < PROMPT_CHUNK else 1
    z, xbc, gates = (a.reshape(nseq, T, a.shape[1]) for a in (z, xbc, gates))
    kern = functools.partial(_ssd_kernel, L=L, nheads=nheads, has_init=init is not None, has_prev=prev is not None)
    s_spec = pl.BlockSpec((1, G, nheads * S_HEADDIM, S_STATE), lambda b, c: (layer, b, 0, 0))
    in_specs = [pl.BlockSpec((G, L, W), lambda b, c: (b, c, 0)),
                pl.BlockSpec((G, L, CD), lambda b, c: (b, c, 0)),
                pl.BlockSpec((G, L, LANES), lambda b, c: (b, c, 0)),
                pl.BlockSpec((S_CONV, CD), lambda b, c: (0, 0)),
                pl.BlockSpec((1, CD), lambda b, c: (0, 0)),
                pl.BlockSpec((SUBLANES, LANES), lambda b, c: (0, 0)),
                pl.BlockSpec((1, W), lambda b, c: (0, 0))]
    args = [z, xbc, gates, conv_w, conv_b, head_params, norm_w]
    if init is not None:
        in_specs += [s_spec, pl.BlockSpec((1, G, S_CONV - 1, CD), lambda b, c: (layer, b, 0, 0))]
        args += list(init)
    aliases = {}
    if prev is not None:
        aliases = {len(args) + t: 1 + t for t in range(2)}
        in_specs += [pl.BlockSpec(memory_space=pl.ANY)] * 2
        args += list(prev)
    y, S, cv = pl.pallas_call(
        kern,
        grid=(nseq // G, nc),
        in_specs=in_specs,
        out_specs=[pl.BlockSpec((G, L, W), lambda b, c: (b, c, 0)),
                   s_spec,
                   pl.BlockSpec((1, G, SUBLANES, CD), lambda b, c: (layer, b, 0, 0))],
        out_shape=[jax.ShapeDtypeStruct((nseq, T, W), BF16),
                   jax.ShapeDtypeStruct((depth, nseq, nheads * S_HEADDIM, S_STATE), F32),
                   jax.ShapeDtypeStruct((depth, nseq, SUBLANES, CD), F32)],
        scratch_shapes=[pltpu.VMEM((G, nheads * S_HEADDIM, S_STATE), F32),
                        pltpu.VMEM((G, SUBLANES + L, CD), F32),
                        pltpu.VMEM((G, L, W), F32)],
        input_output_aliases=aliases,
        compiler_params=_params(("parallel", "arbitrary")),
        name="ssd",
    )(*args)
    return y.reshape(nseq * T, W), S, cv


def _outproj_kernel(hm_ref, hs_ref, wt_ref, wb_ref, x_ref, g_ref, b_ref, y_ref, yb_ref, *, alpha):
    mix = _dot(hm_ref[...], wt_ref[0]) + _dot(hs_ref[...], wb_ref[0])
    y = _layer_norm(alpha * x_ref[...] + mix, g_ref[...], b_ref[...])
    y_ref[...] = y
    yb_ref[...] = y.astype(BF16)


def _outproj(hm, hs, w_out, x, ln_g, ln_b, *, alpha):
    m, d = x.shape
    w = hm.shape[1]
    tm = _largest_divisor(m, 512, 16)
    kern = functools.partial(_outproj_kernel, alpha=alpha)
    return pl.pallas_call(
        kern,
        grid=(m // tm,),
        in_specs=[pl.BlockSpec((tm, w), lambda i: (i, 0)),
                  pl.BlockSpec((tm, w), lambda i: (i, 0)),
                  pl.BlockSpec((1, w, d), lambda i: (0, 0, 0)),
                  pl.BlockSpec((1, w, d), lambda i: (1, 0, 0)),
                  pl.BlockSpec((tm, d), lambda i: (i, 0)),
                  pl.BlockSpec((1, d), lambda i: (0, 0)),
                  pl.BlockSpec((1, d), lambda i: (0, 0))],
        out_specs=[pl.BlockSpec((tm, d), lambda i: (i, 0)),
                   pl.BlockSpec((tm, d), lambda i: (i, 0))],
        out_shape=[jax.ShapeDtypeStruct((m, d), F32),
                   jax.ShapeDtypeStruct((m, d), BF16)],
        compiler_params=_params(("parallel",)),
        name="out_proj_ln",
    )(hm, hs, w_out, w_out, x, ln_g, ln_b)


def _ffn_kernel(*refs, sh, hr, tps, rc, alpha, has_init):
    xb_ref, xf_ref, wu_ref, cw_ref, cb_ref, wd_ref = refs[:6]
    pos = 6
    if has_init:
        h0_ref = refs[pos]
        pos += 1
    lg_ref, lb_ref, y_ref, yb_ref, oh_ref, acc, ext, act, hal = refs[pos:pos + 9]
    i = pl.program_id(0)
    j = pl.program_id(1)
    nj = pl.num_programs(1)
    tm = xb_ref.shape[0]
    tf = wd_ref.shape[1]
    first = (i % tps) == 0

    @pl.when(first)
    def _():
        if has_init:
            ext[0:hr, :] = h0_ref[0, 0]
        else:
            ext[0:hr, :] = jnp.zeros((hr, ext.shape[1]), F32)

    @pl.when(jnp.logical_not(first))
    def _():
        ext[0:hr, :] = hal[j]

    ext[hr:hr + tm, :] = _dot(xb_ref[...], wu_ref[0])
    last = ext[tm:tm + hr, :]
    hal[j] = last
    oh_ref[0, 0] = last

    for r0 in range(0, tm, rc):
        c = (ext[hr - 2 * sh + r0:hr - 2 * sh + r0 + rc, :] * cw_ref[0, 0:1, :]
             + ext[hr - sh + r0:hr - sh + r0 + rc, :] * cw_ref[0, 1:2, :]
             + ext[hr + r0:hr + r0 + rc, :] * cw_ref[0, 2:3, :] + cb_ref[0])
        cg = c[:, :tf]
        act[r0:r0 + rc, :] = (cg * _sigmoid(cg) * c[:, tf:]).astype(BF16)

    @pl.when(j == 0)
    def _():
        acc[...] = jnp.zeros_like(acc)

    acc[...] += _dot(act[...], wd_ref[0])

    @pl.when(j == nj - 1)
    def _():
        y = _layer_norm(alpha * xf_ref[...] + acc[...], lg_ref[...], lb_ref[...])
        y_ref[...] = y
        yb_ref[...] = y.astype(BF16)


def _ffn(xb, xf, w_u, cw, cb, w_d, h0, ln_g, ln_b, *, nseq, tm, sh, hr, alpha, single_buffer):
    m, d = xf.shape
    nj, tf = w_d.shape[0], w_d.shape[1]
    ni = m // tm
    tps = ni // nseq
    kern = functools.partial(_ffn_kernel, sh=sh, hr=hr, tps=tps, rc=FFN_ROW_CHUNK, alpha=alpha,
                             has_init=h0 is not None)
    mode = dict(pipeline_mode=pl.Buffered(1)) if single_buffer else {}
    nbuf = 1 if single_buffer else 2
    vmem_bytes = (nbuf * tm * d * (2 + 4 + 4 + 2)
                  + 2 * (d * 2 * tf * 2 + tf * d * 2)
                  + 2 * 2 * hr * 2 * tf * 4
                  + tm * d * 4 + (hr + tm) * 2 * tf * 4 + tm * tf * 2 + nj * hr * 2 * tf * 4
                  + FFN_COMPILER_SCRATCH_BYTES)
    in_specs = [pl.BlockSpec((tm, d), lambda i, j: (i, 0), **mode),
                pl.BlockSpec((tm, d), lambda i, j: (i, 0), **mode),
                pl.BlockSpec((1, d, 2 * tf), lambda i, j: (j, 0, 0)),
                pl.BlockSpec((1, FFN_CONV, 2 * tf), lambda i, j: (j, 0, 0)),
                pl.BlockSpec((1, 1, 2 * tf), lambda i, j: (j, 0, 0)),
                pl.BlockSpec((1, tf, d), lambda i, j: (j, 0, 0))]
    args = [xb, xf, w_u, cw, cb, w_d]
    if h0 is not None:
        in_specs.append(pl.BlockSpec((1, 1, hr, 2 * tf), lambda i, j: (i // tps, j, 0, 0)))
        args.append(h0)
    in_specs += [pl.BlockSpec((1, d), lambda i, j: (0, 0))] * 2
    args += [ln_g, ln_b]
    return pl.pallas_call(
        kern,
        grid=(ni, nj),
        in_specs=in_specs,
        out_specs=[pl.BlockSpec((tm, d), lambda i, j: (i, 0), **mode),
                   pl.BlockSpec((tm, d), lambda i, j: (i, 0), **mode),
                   pl.BlockSpec((1, 1, hr, 2 * tf),
                                lambda i, j: (i // tps, jnp.where(i % tps == tps - 1, j, 0), 0, 0))],
        out_shape=[jax.ShapeDtypeStruct((m, d), F32),
                   jax.ShapeDtypeStruct((m, d), BF16),
                   jax.ShapeDtypeStruct((nseq, nj, hr, 2 * tf), F32)],
        scratch_shapes=[pltpu.VMEM((tm, d), F32),
                        pltpu.VMEM((hr + tm, 2 * tf), F32),
                        pltpu.VMEM((tm, tf), BF16),
                        pltpu.VMEM((nj, hr, 2 * tf), F32)],
        compiler_params=_params(("arbitrary", "arbitrary"), vmem_limit_bytes=vmem_bytes),
        name="conv_ffn_ln",
    )(*args)


def _ffn_tiles(g, v, tf):
    nj = g.shape[-1] // tf
    lead = g.shape[:-1]
    t = jnp.stack([g.reshape(lead + (nj, tf)), v.reshape(lead + (nj, tf))], axis=-2)
    return jnp.moveaxis(t.reshape(lead + (nj, 2 * tf)), -2, 0)


def _ffn_untile(t, tf):
    nj = t.shape[-3]
    t = jnp.moveaxis(t, -3, -2)
    lead = t.shape[:-2]
    return (t[..., :tf].reshape(lead + (nj * tf,)), t[..., tf:].reshape(lead + (nj * tf,)))


def _tile_up_kernel(w_ref, o_ref, *, dff, tf):
    nj = o_ref.shape[0]
    for j in range(nj):
        lo = j * tf
        width = min(tf, dff - lo)
        for half, base in ((0, 0), (1, dff)):
            o_ref[j, :, half * tf:half * tf + width] = w_ref[0, :, base + lo:base + lo + width].astype(BF16)
            if width < tf:
                o_ref[j, :, half * tf + width:(half + 1) * tf] = jnp.zeros((o_ref.shape[1], tf - width), BF16)


def _tile_up(w_up, layer, tf):
    _, d, two_dff = w_up.shape
    dff = two_dff // 2
    assert dff % LANES == 0
    nj = -(-dff // tf)
    tr = _largest_divisor(d, 128, 16)
    return pl.pallas_call(
        functools.partial(_tile_up_kernel, dff=dff, tf=tf),
        grid=(d // tr,),
        in_specs=[pl.BlockSpec((1, tr, two_dff), lambda r: (layer, r, 0))],
        out_specs=pl.BlockSpec((nj, tr, 2 * tf), lambda r: (0, r, 0)),
        out_shape=jax.ShapeDtypeStruct((nj, d, 2 * tf), BF16),
        compiler_params=_params(("parallel",)),
        name="tile_w_up",
    )(w_up)


def _tile_down_kernel(w_ref, o_ref, *, dff):
    j = pl.program_id(0)
    tf = o_ref.shape[1]
    rows = lax.broadcasted_iota(jnp.int32, o_ref.shape[1:], 0) + j * tf
    o_ref[0] = jnp.where(rows < dff, w_ref[0], 0.0).astype(BF16)


def _tile_down(w_down, layer, tf):
    _, dff, d = w_down.shape
    nj = -(-dff // tf)
    return pl.pallas_call(
        functools.partial(_tile_down_kernel, dff=dff),
        grid=(nj,),
        in_specs=[pl.BlockSpec((1, tf, d), lambda j: (layer, j, 0))],
        out_specs=pl.BlockSpec((1, tf, d), lambda j: (j, 0, 0)),
        out_shape=jax.ShapeDtypeStruct((nj, tf, d), BF16),
        compiler_params=_params(("parallel",)),
        name="tile_w_down",
    )(w_down)


def _prep_layer(l, w_in, b_i, b_f, m_norm_w, s_conv_w, s_conv_b, dt_bias, A_log, D_skip, s_norm_w, w_out,
                ln1_g, ln1_b, w_up, f_conv_w, f_conv_b, w_down, ln2_g, ln2_b):
    mw = m_norm_w.shape[1]
    sw = s_norm_w.shape[1]
    cd = s_conv_w.shape[2]
    nh = dt_bias.shape[1]
    dff = w_down.shape[1]
    tf = FFN_TILE
    fp = -(-dff // tf) * tf
    o = 0
    cols = {}
    for name, size in (("qkv", 3 * mw), ("o", mw), ("i", M_HEADS), ("f", M_HEADS),
                       ("z", sw), ("xbc", cd), ("dt", nh)):
        cols[name] = (o, o + size)
        o += size
    wl = w_in[l]
    cut = lambda name: wl[:, cols[name][0]:cols[name][1]]
    p = {}
    wg = jnp.concatenate([cut("i"), cut("f"), cut("dt")], axis=1)
    p["w_in"] = [cut("qkv").astype(BF16), cut("o").astype(BF16), cut("z").astype(BF16), cut("xbc").astype(BF16),
                 jnp.pad(wg, ((0, 0), (0, LANES - wg.shape[1]))).astype(BF16)]
    gb = jnp.concatenate([b_i[l], b_f[l]])
    p["gate_bias"] = jnp.pad(gb, (0, LANES - gb.shape[0]))[None, :]
    p["m_norm_w"] = m_norm_w[l][None, :]
    p["s_conv_w"] = s_conv_w[l]
    p["s_conv_b"] = s_conv_b[l][None, :]
    hp = jnp.stack([dt_bias[l], A_log[l], D_skip[l]])
    p["head_params"] = jnp.pad(hp, ((0, SUBLANES - 3), (GATE_DT, LANES - GATE_DT - nh)))
    p["s_norm_w"] = s_norm_w[l][None, :]
    p["w_out"] = w_out[l].astype(BF16).reshape(2, mw, w_out.shape[2])
    p["ln1_g"] = ln1_g[l][None, :]
    p["ln1_b"] = ln1_b[l][None, :]
    padc = lambda a: jnp.pad(a, ((0, 0), (0, fp - dff)))
    p["w_up"] = _tile_up(w_up, l, tf)
    p["f_conv_w"] = _ffn_tiles(padc(f_conv_w[l][:, :dff]), padc(f_conv_w[l][:, dff:]), tf)
    p["f_conv_b"] = _ffn_tiles(padc(f_conv_b[l][None, :dff]), padc(f_conv_b[l][None, dff:]), tf)
    p["w_down"] = _tile_down(w_down, l, tf)
    p["ln2_g"] = ln2_g[l][None, :]
    p["ln2_b"] = ln2_b[l][None, :]
    p["dff"] = dff
    p["fp"] = fp
    return p


def _layer(xf, xb, p, init, prev, *, layer, depth, nseq, T, L, time_major_ffn, alpha):
    qkv, o_pre, z, xbc, gates = _in_proj(xb, p["w_in"], [BF16, BF16, BF16, BF16, F32], tm_target=2048)
    hm, C, n, m = _mlstm(qkv, o_pre, gates, p["gate_bias"], p["m_norm_w"],
                         None if init is None else init["mlstm"], None if prev is None else prev["mlstm"],
                         layer=layer, depth=depth, nseq=nseq, T=T, L=L)
    hs, S, sconv = _ssd(z, xbc, gates, p["s_conv_w"], p["s_conv_b"], p["head_params"], p["s_norm_w"],
                        None if init is None else init["ssd"], None if prev is None else prev["ssd"],
                        layer=layer, depth=depth, nseq=nseq, T=T, L=L)
    x1f, x1b = _outproj(hm, hs, p["w_out"], xf, p["ln1_g"], p["ln1_b"], alpha=alpha)
    d = xf.shape[1]
    dff = p["dff"]
    ffn_w = (p["w_up"], p["f_conv_w"], p["f_conv_b"], p["w_down"])
    h0 = None if init is None else init["ffn"]
    if time_major_ffn:
        tr = lambda a: a.reshape(nseq, T, d).transpose(1, 0, 2).reshape(nseq * T, d)
        x1f_t, x1b_t = tr(x1f), tr(x1b)
        tm = _largest_divisor(nseq * T, 512, (FFN_CONV - 1) * nseq)
        y_f, y_b, oh = _ffn(x1b_t, x1f_t, *ffn_w, h0, p["ln2_g"], p["ln2_b"], nseq=1, tm=tm, sh=nseq,
                            hr=(FFN_CONV - 1) * nseq, alpha=alpha, single_buffer=True)
        tb = lambda a: a.reshape(T, nseq, d).transpose(1, 0, 2).reshape(nseq * T, d)
        y_f, y_b = tb(y_f), tb(y_b)
        og, ov = _ffn_untile(oh, FFN_TILE)
        fconv = jnp.concatenate([og[0, :, :dff], ov[0, :, :dff]], axis=1)
        fconv = fconv.reshape(FFN_CONV - 1, nseq, 2 * dff).transpose(1, 0, 2)
    else:
        tm = _largest_divisor(T, 512, FFN_ROW_CHUNK)
        y_f, y_b, oh = _ffn(x1b, x1f, *ffn_w, h0, p["ln2_g"], p["ln2_b"],
                            nseq=nseq, tm=tm, sh=1, hr=SUBLANES, alpha=alpha, single_buffer=False)
        og, ov = _ffn_untile(oh, FFN_TILE)
        k = FFN_CONV - 1
        fconv = jnp.concatenate([og[:, SUBLANES - k:, :dff], ov[:, SUBLANES - k:, :dff]], axis=2)
    return y_f, y_b, dict(mlstm=(C, n, m), ssd=(S, sconv)), fconv


def kernel(x_prompt, x_sample, state_mlstm_C, state_mlstm_n, state_mlstm_m, state_ssm, state_ssm_conv,
           state_ffn_conv, w_in, mlstm_b_i, mlstm_b_f, mlstm_norm_w, ssm_conv_w, ssm_conv_b, ssm_dt_bias,
           ssm_A_log, ssm_D, ssm_norm_w, w_out, ln1_g, ln1_b, ffn_w_up, ffn_conv_w, ffn_conv_b,
           ffn_w_down, ln2_g, ln2_b):
    weights = (w_in, mlstm_b_i, mlstm_b_f, mlstm_norm_w, ssm_conv_w, ssm_conv_b, ssm_dt_bias, ssm_A_log,
               ssm_D, ssm_norm_w, w_out, ln1_g, ln1_b, ffn_w_up, ffn_conv_w, ffn_conv_b, ffn_w_down,
               ln2_g, ln2_b)
    depth = w_in.shape[0]
    alpha = float((2 * depth) ** 0.25)
    B, T, D = x_prompt.shape
    Bs, Ts, _ = x_sample.shape
    nh = ssm_dt_bias.shape[1]
    Lp = PROMPT_CHUNK if T % PROMPT_CHUNK == 0 else T
    Ls = PROMPT_CHUNK if Ts % PROMPT_CHUNK == 0 else Ts

    xpf = x_prompt.reshape(B * T, D)
    xpb = xpf.astype(BF16)
    xsf = x_sample.reshape(Bs * Ts, D)
    xsb = xsf.astype(BF16)
    m_in = jnp.pad(state_mlstm_m, ((0, 0), (0, 0), (GATE_F, LANES - GATE_F - M_HEADS)))[:, :, None, :]
    s_in = state_ssm.reshape(depth, Bs, nh * S_HEADDIM, S_STATE)
    p_prev, s_prev, p_fconv, s_fconv = None, None, [], []
    for l in range(depth):
        p = _prep_layer(l, *weights)
        fp, dff = p["fp"], p["dff"]
        xpf, xpb, p_prev, fc = _layer(xpf, xpb, p, None, p_prev, layer=l, depth=depth, nseq=B, T=T, L=Lp,
                                      time_major_ffn=False, alpha=alpha)
        p_fconv.append(fc)
        fc0 = state_ffn_conv[l].transpose(1, 0, 2).reshape((FFN_CONV - 1) * Bs, 2 * dff)
        padf = lambda a: jnp.pad(a, ((0, 0), (0, fp - dff)))
        init = dict(mlstm=(state_mlstm_C, state_mlstm_n, m_in), ssd=(s_in, state_ssm_conv),
                    ffn=_ffn_tiles(padf(fc0[:, :dff]), padf(fc0[:, dff:]), FFN_TILE)[None])
        xsf, xsb, s_prev, fc = _layer(xsf, xsb, p, init, s_prev, layer=l, depth=depth, nseq=Bs, T=Ts, L=Ls,
                                      time_major_ffn=True, alpha=alpha)
        s_fconv.append(fc)

    def unpack(st, nseq):
        C, n, m = st["mlstm"]
        S, sconv = st["ssd"]
        return (C, n, m[:, :, 0, GATE_F:GATE_F + M_HEADS], S.reshape(depth, nseq, nh, S_HEADDIM, S_STATE),
                sconv[:, :, SUBLANES - (S_CONV - 1):, :])

    pC, pn, pm, pS, psc = unpack(p_prev, B)
    sC, sn, sm, sS, ssc = unpack(s_prev, Bs)
    return (xpf.reshape(B, T, D), xsf.reshape(Bs, Ts, D), pC, pn, pm, pS, psc, jnp.stack(p_fconv),
            sC, sn, sm, sS, ssc, jnp.stack(s_fconv))
```

```python
import functools

import jax
import jax.numpy as jnp
from jax import lax
from jax.experimental import pallas as pl
from jax.experimental.pallas import tpu as pltpu

F32 = jnp.float32
BF16 = jnp.bfloat16
HIGHEST = lax.Precision.HIGHEST

M_HEADS = 4
S_HEADDIM = 64
S_GROUPS = 2
S_STATE = 128
S_CONV = 4
FFN_CONV = 3
LN_EPS = 1e-5
GN_EPS = 1e-6
NEG_BIG = -1e30

LANES = 128
SUBLANES = 8
MXU_DIM = 256
VMEM_LIMIT_BYTES = 52 * 1024 * 1024

GATE_I = 0
GATE_F = 4
GATE_DT = 8

PROMPT_CHUNK = 128
SEQS_PER_STEP = 8
VREG_FILE = 64
IN_PROJ_TILE = 2 * MXU_DIM
FFN_TILE = 2 * MXU_DIM
FFN_ROW_CHUNK = 32
LN_ROW_CHUNK = 16
FFN_COMPILER_SCRATCH_BYTES = 8 * 1024 * 1024


def _largest_divisor(n, target, mult):
    best = None
    for d in range(mult, min(n, target) + 1, mult):
        if n % d == 0:
            best = d
    if best is None:
        raise ValueError(f"no tile for {n=} {target=} {mult=}")
    return best


def _params(sem, vmem_limit_bytes=VMEM_LIMIT_BYTES):
    return pltpu.CompilerParams(dimension_semantics=sem, vmem_limit_bytes=vmem_limit_bytes)


def _dot(a, b):
    return jnp.dot(a, b, preferred_element_type=F32)


def _dot_nt(a, b, precision=None):
    return lax.dot_general(a, b, (((1,), (1,)), ((), ())), precision=precision, preferred_element_type=F32)


def _dot_tn(a, b):
    return lax.dot_general(a, b, (((0,), (0,)), ((), ())), preferred_element_type=F32)


def _sigmoid(x):
    return 1.0 / (1.0 + jnp.exp(-x))


def _softplus(x):
    return jnp.maximum(x, 0.0) + jnp.log1p(jnp.exp(-jnp.abs(x)))


def _layer_norm(r, g, b):
    mu = jnp.mean(r, axis=-1, keepdims=True)
    d = r - mu
    var = jnp.mean(d * d, axis=-1, keepdims=True)
    return d * lax.rsqrt(var + LN_EPS) * g + b


def _residual_ln_rows(x_ref, f_ref, g_ref, b_ref, y_ref, yb_ref, alpha):
    for r0 in range(0, x_ref.shape[0], LN_ROW_CHUNK):
        rows = slice(r0, r0 + LN_ROW_CHUNK)
        y = _layer_norm(alpha * x_ref[rows, :] + f_ref[rows, :], g_ref[...], b_ref[...])
        y_ref[rows, :] = y
        yb_ref[rows, :] = y.astype(BF16)


def _in_proj_kernel(*refs, groups):
    x_ref = refs[0]
    ng = len(groups)
    w_refs = refs[1:1 + ng]
    o_refs = refs[1 + ng:1 + 2 * ng]
    j = pl.program_id(1)
    for (start, count), w_ref, o_ref in zip(groups, w_refs, o_refs):
        @pl.when(jnp.logical_and(j >= start, j < start + count))
        def _(w_ref=w_ref, o_ref=o_ref):
            o_ref[...] = _dot(x_ref[...], w_ref[...]).astype(o_ref.dtype)


def _in_proj(x, weights, out_dtypes, tm_target):
    m, k = x.shape
    tm = _largest_divisor(m, tm_target, 16)
    tiles, groups, start = [], [], 0
    for w in weights:
        n = w.shape[1]
        tn = _largest_divisor(n, IN_PROJ_TILE, LANES)
        tiles.append(tn)
        groups.append((start, n // tn))
        start += n // tn

    def clamp(s, c):
        return lambda i, j: (0, jnp.clip(j - s, 0, c - 1))

    def clamp_out(s, c):
        return lambda i, j: (i, jnp.clip(j - s, 0, c - 1))

    one = lambda c: dict(pipeline_mode=pl.Buffered(1)) if c == 1 else {}
    in_specs = [pl.BlockSpec((tm, k), lambda i, j: (i, 0), pipeline_mode=pl.Buffered(1))]
    in_specs += [pl.BlockSpec((k, tn), clamp(s, c), **one(c)) for tn, (s, c) in zip(tiles, groups)]
    out_specs = [pl.BlockSpec((tm, tn), clamp_out(s, c)) for tn, (s, c) in zip(tiles, groups)]
    out_shape = [jax.ShapeDtypeStruct((m, w.shape[1]), dt) for w, dt in zip(weights, out_dtypes)]
    return pl.pallas_call(
        functools.partial(_in_proj_kernel, groups=tuple(groups)),
        grid=(m // tm, start),
        in_specs=in_specs,
        out_specs=out_specs,
        out_shape=out_shape,
        compiler_params=_params(("parallel", "arbitrary")),
        name="in_proj",
    )(x, *weights)


def _mlstm_kernel(*refs, L, dk, nc, has_init, has_prev):
    q_ref, k_ref, v_ref, o_ref, g_ref, gb_ref, nw_ref = refs[:7]
    pos = 7
    if has_init:
        c0_ref, n0_ref, m0_ref = refs[pos:pos + 3]
        pos += 3
    if has_prev:
        pos += 3
    h_ref, c_out, n_out, m_out, c_s, n_s, m_s = refs[pos:pos + 7]
    c = pl.program_id(1)

    @pl.when(c == 0)
    def _():
        if has_init:
            c_s[...] = c0_ref[0]
            n_s[...] = n0_ref[0]
            m_s[...] = m0_ref[0]
        else:
            c_s[...] = jnp.zeros_like(c_s)
            n_s[...] = jnp.zeros_like(n_s)
            m_s[...] = jnp.zeros_like(m_s)

    row = lax.broadcasted_iota(jnp.int32, (L, L), 0)
    col = lax.broadcasted_iota(jnp.int32, (L, L), 1)
    causal = col <= row
    tri = jnp.where(causal, 1.0, 0.0).astype(F32)
    lane = lax.broadcasted_iota(jnp.int32, (L, LANES), 1)
    sel = jnp.where(lax.broadcasted_iota(jnp.int32, (SUBLANES, LANES), 0)
                    == lax.broadcasted_iota(jnp.int32, (SUBLANES, LANES), 1), 1.0, 0.0).astype(F32)
    m_lane = lax.broadcasted_iota(jnp.int32, (1, LANES), 1)
    k_scale = dk ** -0.5

    nsq = q_ref.shape[0]
    pairs = [(s_i, h) for s_i in range(nsq) for h in range(M_HEADS)]
    hsl = lambda h: slice(h * dk, (h + 1) * dk)


    gs, bcums, rowss, m_vecs = [], [], [], []
    for s_i in range(nsq):
        g = g_ref[s_i] + gb_ref[...]
        lf = jnp.minimum(g, 0.0) - jnp.log1p(jnp.exp(-jnp.abs(g)))
        bcum = jnp.dot(tri, lf, precision=HIGHEST, preferred_element_type=F32)
        packed = jnp.where(lane < GATE_F, g, bcum)
        gs.append(g)
        bcums.append(bcum)
        rowss.append(_dot_nt(sel, packed, precision=HIGHEST))
        m_vecs.append(m_s[s_i])

    st = []
    for s_i, h in pairs:
        bc = bcums[s_i][:, GATE_F + h:GATE_F + h + 1]
        igr = rowss[s_i][GATE_I + h:GATE_I + h + 1, :]
        br = rowss[s_i][GATE_F + h:GATE_F + h + 1, :]
        m_h = m_vecs[s_i][:, GATE_F + h:GATE_F + h + 1]
        log_w = jnp.where(causal, bc - br + igr, NEG_BIG)
        log_inter = bc + m_h
        m_t = jnp.maximum(log_inter, jnp.max(log_w, axis=1, keepdims=True))
        st.append(dict(bc=bc, m_h=m_h, m_t=m_t, w_intra=jnp.exp(log_w - m_t), w_inter=jnp.exp(log_inter - m_t)))

    for (s_i, h), e in zip(pairs, st):
        q = q_ref[s_i, :, hsl(h)]
        kf = k_ref[s_i, :, hsl(h)].astype(F32) * k_scale
        v = v_ref[s_i, :, hsl(h)]
        s = _dot_nt(q, kf.astype(BF16)) * e["w_intra"]
        num = _dot(s.astype(BF16), v) + e["w_inter"] * _dot(q, c_s[s_i, h].astype(BF16))
        qn = jnp.sum(q.astype(F32) * n_s[s_i, h:h + 1, :], axis=1, keepdims=True)
        den = jnp.sum(s, axis=1, keepdims=True) + e["w_inter"] * qn
        e["hh"] = num / jnp.maximum(jnp.abs(den), jnp.exp(-e["m_t"]))

    m_news = list(m_vecs)
    for (s_i, h), e in zip(pairs, st):
        bc, m_h, m_t = e["bc"], e["m_h"], e["m_t"]
        igc = gs[s_i][:, GATE_I + h:GATE_I + h + 1]
        m_end = m_t[L - 1:L, :]
        b_end = bc[L - 1:L, :]
        w_end = jnp.exp(b_end - bc + igc - m_end)
        decay = jnp.exp(b_end + m_h - m_end)
        kw = k_ref[s_i, :, hsl(h)].astype(F32) * k_scale * w_end
        c_s[s_i, h] = decay * c_s[s_i, h] + _dot_tn(kw.astype(BF16), v_ref[s_i, :, hsl(h)])
        n_s[s_i, h:h + 1, :] = decay * n_s[s_i, h:h + 1, :] + jnp.sum(kw, axis=0, keepdims=True)
        m_news[s_i] = jnp.where(m_lane == GATE_F + h, m_end, m_news[s_i])
    for s_i in range(nsq):
        m_s[s_i] = m_news[s_i]

    for (s_i, h), e in zip(pairs, st):
        hh = e["hh"]
        mu = jnp.mean(hh, axis=1, keepdims=True)
        d = hh - mu
        var = jnp.mean(d * d, axis=1, keepdims=True)
        hn = d * lax.rsqrt(var + GN_EPS) * nw_ref[:, hsl(h)]
        h_ref[s_i, :, hsl(h)] = (hn * _sigmoid(o_ref[s_i, :, hsl(h)].astype(F32))).astype(h_ref.dtype)

    @pl.when(c == nc - 1)
    def _():
        c_out[0] = c_s[...]
        n_out[0] = n_s[...]
        m_out[0] = m_s[...]


def _mlstm(qkv, o_pre, gates, gate_bias, norm_w, init, prev, *, layer, depth, nseq, T, L):
    W = qkv.shape[1] // 3
    dk = W // M_HEADS
    nc = T // L
    G = _largest_divisor(nseq, SEQS_PER_STEP, 1)
    qkv, o_pre, gates = (a.reshape(nseq, T, a.shape[1]) for a in (qkv, o_pre, gates))
    kern = functools.partial(_mlstm_kernel, L=L, dk=dk, nc=nc, has_init=init is not None,
                             has_prev=prev is not None)
    st_specs = [pl.BlockSpec((1, G, M_HEADS, dk, dk), lambda b, c: (layer, b, 0, 0, 0)),
                pl.BlockSpec((1, G, M_HEADS, dk), lambda b, c: (layer, b, 0, 0)),
                pl.BlockSpec((1, G, 1, LANES), lambda b, c: (layer, b, 0, 0))]
    in_specs = [pl.BlockSpec((G, L, W), lambda b, c: (b, c, 0)),
                pl.BlockSpec((G, L, W), lambda b, c: (b, c, 1)),
                pl.BlockSpec((G, L, W), lambda b, c: (b, c, 2)),
                pl.BlockSpec((G, L, W), lambda b, c: (b, c, 0)),
                pl.BlockSpec((G, L, LANES), lambda b, c: (b, c, 0)),
                pl.BlockSpec((1, LANES), lambda b, c: (0, 0)),
                pl.BlockSpec((1, W), lambda b, c: (0, 0))]
    args = [qkv, qkv, qkv, o_pre, gates, gate_bias, norm_w]
    if init is not None:
        in_specs += st_specs
        args += list(init)
    aliases = {}
    if prev is not None:
        aliases = {len(args) + t: 1 + t for t in range(3)}
        in_specs += [pl.BlockSpec(memory_space=pl.ANY)] * 3
        args += list(prev)
    h, C, n, m = pl.pallas_call(
        kern,
        grid=(nseq // G, nc),
        in_specs=in_specs,
        out_specs=[pl.BlockSpec((G, L, W), lambda b, c: (b, c, 0))] + st_specs,
        out_shape=[jax.ShapeDtypeStruct((nseq, T, W), BF16),
                   jax.ShapeDtypeStruct((depth, nseq, M_HEADS, dk, dk), F32),
                   jax.ShapeDtypeStruct((depth, nseq, M_HEADS, dk), F32),
                   jax.ShapeDtypeStruct((depth, nseq, 1, LANES), F32)],
        scratch_shapes=[pltpu.VMEM((G, M_HEADS, dk, dk), F32),
                        pltpu.VMEM((G, M_HEADS, dk), F32),
                        pltpu.VMEM((G, 1, LANES), F32)],
        input_output_aliases=aliases,
        compiler_params=_params(("parallel", "arbitrary")),
        name="mlstm",
    )(*args)
    return h.reshape(nseq * T, W), C, n, m


def _ssd_kernel(*refs, L, nheads, has_init, has_prev):
    z_ref, x_ref, g_ref, cw_ref, cb_ref, hp_ref, nw_ref = refs[:7]
    pos = 7
    if has_init:
        s0_ref, cv0_ref = refs[pos:pos + 2]
        pos += 2
    if has_prev:
        pos += 2
    y_ref, s_out, cv_out, s_s, xp_s, y_s = refs[pos:pos + 6]
    c = pl.program_id(1)
    nc = pl.num_programs(1)
    P = S_HEADDIM
    N = S_STATE
    hg = nheads // S_GROUPS
    W = nheads * P
    halo = SUBLANES
    kc = S_CONV

    G = z_ref.shape[0]

    @pl.when(c == 0)
    def _():
        xp_s[:, 0:halo, :] = jnp.zeros((G, halo, xp_s.shape[2]), F32)
        if has_init:
            s_s[...] = s0_ref[0]
            xp_s[:, halo - (kc - 1):halo, :] = cv0_ref[0]
        else:
            s_s[...] = jnp.zeros_like(s_s)

    hp = hp_ref[...]
    row = lax.broadcasted_iota(jnp.int32, (L, L), 0)
    col = lax.broadcasted_iota(jnp.int32, (L, L), 1)
    causal = col <= row
    tri = jnp.where(causal, 1.0, 0.0).astype(F32)
    sel = jnp.where(lax.broadcasted_iota(jnp.int32, (nheads, LANES), 0) + GATE_DT
                    == lax.broadcasted_iota(jnp.int32, (nheads, LANES), 1), 1.0, 0.0).astype(F32)
    gw = W // S_GROUPS

    seqs = range(G)
    heads = [(s_i, gi, hh) for s_i in seqs for gi in range(S_GROUPS) for hh in range(hg)]

    sq = []
    for s_i in seqs:
        xp_s[s_i, halo:halo + L, :] = x_ref[s_i].astype(F32)
        conv = cb_ref[...] + xp_s[s_i, halo:halo + L, :] * cw_ref[kc - 1:kc, :]
        for j in range(kc - 1):
            off = halo - (kc - 1) + j
            conv = conv + xp_s[s_i, off:off + L, :] * cw_ref[j:j + 1, :]
        tail = xp_s[s_i, L:L + halo, :]
        xp_s[s_i, 0:halo, :] = tail
        cv_out[0, s_i] = tail
        xbc = conv * _sigmoid(conv)
        dt = _softplus(g_ref[s_i] + hp[0:1, :])
        a = dt * (-jnp.exp(hp[1:2, :]))
        bcum = jnp.dot(tri, a, precision=HIGHEST, preferred_element_type=F32)
        b_last = bcum[L - 1:L, :]
        sq.append(dict(
            xs=xbc[:, 0:W], Bm=xbc[:, W:W + S_GROUPS * N], Cm=xbc[:, W + S_GROUPS * N:W + 2 * S_GROUPS * N],
            bcum=bcum, b_rows=_dot_nt(sel, bcum, precision=HIGHEST), dt_rows=_dot_nt(sel, dt, precision=HIGHEST),
            eb=jnp.exp(bcum), w_end=jnp.exp(b_last - bcum) * dt, dec_end=jnp.exp(b_last)))

    gq = {}
    for s_i in seqs:
        for gi in range(S_GROUPS):
            Bg = sq[s_i]["Bm"][:, gi * N:(gi + 1) * N].astype(BF16)
            Cg = sq[s_i]["Cm"][:, gi * N:(gi + 1) * N].astype(BF16)
            s_g = s_s[s_i, gi * hg * P:(gi + 1) * hg * P, :]
            gq[s_i, gi] = dict(Bg=Bg, cb=_dot_nt(Cg, Bg), y_inter=_dot_nt(Cg, s_g.astype(BF16)))

    mw_vregs = -(-L // (2 * SUBLANES)) * -(-L // LANES)
    batch = max(1, VREG_FILE // mw_vregs)
    for b0 in range(0, len(heads), batch):
        mws = {}
        for s_i, gi, hh in heads[b0:b0 + batch]:
            idx = gi * hg + hh
            ln = GATE_DT + idx
            e = sq[s_i]
            decay = jnp.exp(jnp.where(causal, e["bcum"][:, ln:ln + 1] - e["b_rows"][idx:idx + 1, :], NEG_BIG))
            mws[s_i, gi, hh] = (gq[s_i, gi]["cb"] * decay * e["dt_rows"][idx:idx + 1, :]).astype(BF16)

        for s_i, gi, hh in heads[b0:b0 + batch]:
            idx = gi * hg + hh
            ln = GATE_DT + idx
            e = sq[s_i]
            xs_h = e["xs"][:, idx * P:(idx + 1) * P]
            y_s[s_i, :, idx * P:(idx + 1) * P] = (
                _dot(mws[s_i, gi, hh], xs_h.astype(BF16))
                + e["eb"][:, ln:ln + 1] * gq[s_i, gi]["y_inter"][:, hh * P:(hh + 1) * P]
                + hp[2:3, ln:ln + 1] * xs_h)

    for s_i in seqs:
        e = sq[s_i]
        for gi in range(S_GROUPS):
            xw = jnp.concatenate(
                [(e["xs"][:, (gi * hg + hh) * P:(gi * hg + hh + 1) * P]
                  * e["w_end"][:, GATE_DT + gi * hg + hh:GATE_DT + gi * hg + hh + 1]).astype(BF16)
                 for hh in range(hg)], axis=1)
            upd = _dot_tn(xw, gq[s_i, gi]["Bg"])
            for hh in range(hg):
                ln = GATE_DT + gi * hg + hh
                r0 = (gi * hg + hh) * P
                s_s[s_i, r0:r0 + P, :] = (e["dec_end"][:, ln:ln + 1] * s_s[s_i, r0:r0 + P, :]
                                          + upd[hh * P:(hh + 1) * P, :])

    for s_i in seqs:
        for gi in range(S_GROUPS):
            sl = slice(gi * gw, (gi + 1) * gw)
            z = z_ref[s_i, :, sl].astype(F32)
            gg = y_s[s_i, :, sl] * (z * _sigmoid(z))
            gg = gg * lax.rsqrt(jnp.mean(gg * gg, axis=1, keepdims=True) + GN_EPS)
            y_ref[s_i, :, sl] = (gg * nw_ref[:, sl]).astype(y_ref.dtype)

    @pl.when(c == nc - 1)
    def _():
        s_out[0] = s_s[...]


def _ssd(z, xbc, gates, conv_w, conv_b, head_params, norm_w, init, prev, *, layer, depth, nseq, T, L):
    W = z.shape[1]
    nheads = W // S_HEADDIM
    CD = xbc.shape[1]
    nc = T // L
    G = _largest_divisor(nseq, SEQS_PER_STEP, 1) if L < PROMPT_CHUNK else 1
    z, xbc, gates = (a.reshape(nseq, T, a.shape[1]) for a in (z, xbc, gates))
    kern = functools.partial(_ssd_kernel, L=L, nheads=nheads, has_init=init is not None, has_prev=prev is not None)
    s_spec = pl.BlockSpec((1, G, nheads * S_HEADDIM, S_STATE), lambda b, c: (layer, b, 0, 0))
    in_specs = [pl.BlockSpec((G, L, W), lambda b, c: (b, c, 0)),
                pl.BlockSpec((G, L, CD), lambda b, c: (b, c, 0)),
                pl.BlockSpec((G, L, LANES), lambda b, c: (b, c, 0)),
                pl.BlockSpec((S_CONV, CD), lambda b, c: (0, 0)),
                pl.BlockSpec((1, CD), lambda b, c: (0, 0)),
                pl.BlockSpec((SUBLANES, LANES), lambda b, c: (0, 0)),
                pl.BlockSpec((1, W), lambda b, c: (0, 0))]
    args = [z, xbc, gates, conv_w, conv_b, head_params, norm_w]
    if init is not None:
        in_specs += [s_spec, pl.BlockSpec((1, G, S_CONV - 1, CD), lambda b, c: (layer, b, 0, 0))]
        args += list(init)
    aliases = {}
    if prev is not None:
        aliases = {len(args) + t: 1 + t for t in range(2)}
        in_specs += [pl.BlockSpec(memory_space=pl.ANY)] * 2
        args += list(prev)
    y, S, cv = pl.pallas_call(
        kern,
        grid=(nseq // G, nc),
        in_specs=in_specs,
        out_specs=[pl.BlockSpec((G, L, W), lambda b, c: (b, c, 0)),
                   s_spec,
                   pl.BlockSpec((1, G, SUBLANES, CD), lambda b, c: (layer, b, 0, 0))],
        out_shape=[jax.ShapeDtypeStruct((nseq, T, W), BF16),
                   jax.ShapeDtypeStruct((depth, nseq, nheads * S_HEADDIM, S_STATE), F32),
                   jax.ShapeDtypeStruct((depth, nseq, SUBLANES, CD), F32)],
        scratch_shapes=[pltpu.VMEM((G, nheads * S_HEADDIM, S_STATE), F32),
                        pltpu.VMEM((G, SUBLANES + L, CD), F32),
                        pltpu.VMEM((G, L, W), F32)],
        input_output_aliases=aliases,
        compiler_params=_params(("parallel", "arbitrary")),
        name="ssd",
    )(*args)
    return y.reshape(nseq * T, W), S, cv


def _outproj_kernel(hm_ref, hs_ref, wt_ref, wb_ref, x_ref, g_ref, b_ref, y_ref, yb_ref, *, alpha):
    y_ref[...] = _dot(hm_ref[...], wt_ref[0]) + _dot(hs_ref[...], wb_ref[0])
    _residual_ln_rows(x_ref, y_ref, g_ref, b_ref, y_ref, yb_ref, alpha)


def _outproj(hm, hs, w_out, x, ln_g, ln_b, *, alpha):
    m, d = x.shape
    w = hm.shape[1]
    tm = _largest_divisor(m, 512, 16)
    kern = functools.partial(_outproj_kernel, alpha=alpha)
    return pl.pallas_call(
        kern,
        grid=(m // tm,),
        in_specs=[pl.BlockSpec((tm, w), lambda i: (i, 0)),
                  pl.BlockSpec((tm, w), lambda i: (i, 0)),
                  pl.BlockSpec((1, w, d), lambda i: (0, 0, 0)),
                  pl.BlockSpec((1, w, d), lambda i: (1, 0, 0)),
                  pl.BlockSpec((tm, d), lambda i: (i, 0)),
                  pl.BlockSpec((1, d), lambda i: (0, 0)),
                  pl.BlockSpec((1, d), lambda i: (0, 0))],
        out_specs=[pl.BlockSpec((tm, d), lambda i: (i, 0)),
                   pl.BlockSpec((tm, d), lambda i: (i, 0))],
        out_shape=[jax.ShapeDtypeStruct((m, d), F32),
                   jax.ShapeDtypeStruct((m, d), BF16)],
        compiler_params=_params(("parallel",)),
        name="out_proj_ln",
    )(hm, hs, w_out, w_out, x, ln_g, ln_b)


def _ffn_kernel(*refs, sh, hr, tps, rc, alpha, has_init):
    xb_ref, xf_ref, wu_ref, cw_ref, cb_ref, wd_ref = refs[:6]
    pos = 6
    if has_init:
        h0_ref = refs[pos]
        pos += 1
    lg_ref, lb_ref, y_ref, yb_ref, oh_ref, acc, ext, act, hal = refs[pos:pos + 9]
    i = pl.program_id(0)
    j = pl.program_id(1)
    nj = pl.num_programs(1)
    tm = xb_ref.shape[0]
    tf = wd_ref.shape[1]
    first = (i % tps) == 0

    @pl.when(first)
    def _():
        if has_init:
            ext[0:hr, :] = h0_ref[0, 0]
        else:
            ext[0:hr, :] = jnp.zeros((hr, ext.shape[1]), F32)

    @pl.when(jnp.logical_not(first))
    def _():
        ext[0:hr, :] = hal[j]

    ext[hr:hr + tm, :] = _dot(xb_ref[...], wu_ref[0])
    last = ext[tm:tm + hr, :]
    hal[j] = last
    oh_ref[0, 0] = last

    for r0 in range(0, tm, rc):
        c = (ext[hr - 2 * sh + r0:hr - 2 * sh + r0 + rc, :] * cw_ref[0, 0:1, :]
             + ext[hr - sh + r0:hr - sh + r0 + rc, :] * cw_ref[0, 1:2, :]
             + ext[hr + r0:hr + r0 + rc, :] * cw_ref[0, 2:3, :] + cb_ref[0])
        cg = c[:, :tf]
        act[r0:r0 + rc, :] = (cg * _sigmoid(cg) * c[:, tf:]).astype(BF16)

    @pl.when(j == 0)
    def _():
        acc[...] = jnp.zeros_like(acc)

    acc[...] += _dot(act[...], wd_ref[0])

    @pl.when(j == nj - 1)
    def _():
        _residual_ln_rows(xf_ref, acc, lg_ref, lb_ref, y_ref, yb_ref, alpha)


def _ffn(xb, xf, w_u, cw, cb, w_d, h0, ln_g, ln_b, *, nseq, tm, sh, hr, alpha, single_buffer):
    m, d = xf.shape
    nj, tf = w_d.shape[0], w_d.shape[1]
    ni = m // tm
    tps = ni // nseq
    kern = functools.partial(_ffn_kernel, sh=sh, hr=hr, tps=tps, rc=FFN_ROW_CHUNK, alpha=alpha,
                             has_init=h0 is not None)
    mode = dict(pipeline_mode=pl.Buffered(1)) if single_buffer else {}
    nbuf = 1 if single_buffer else 2
    vmem_bytes = (nbuf * tm * d * (2 + 4 + 4 + 2)
                  + 2 * (d * 2 * tf * 2 + tf * d * 2)
                  + 2 * 2 * hr * 2 * tf * 4
                  + tm * d * 4 + (hr + tm) * 2 * tf * 4 + tm * tf * 2 + nj * hr * 2 * tf * 4
                  + FFN_COMPILER_SCRATCH_BYTES)
    in_specs = [pl.BlockSpec((tm, d), lambda i, j: (i, 0), **mode),
                pl.BlockSpec((tm, d), lambda i, j: (i, 0), **mode),
                pl.BlockSpec((1, d, 2 * tf), lambda i, j: (j, 0, 0)),
                pl.BlockSpec((1, FFN_CONV, 2 * tf), lambda i, j: (j, 0, 0)),
                pl.BlockSpec((1, 1, 2 * tf), lambda i, j: (j, 0, 0)),
                pl.BlockSpec((1, tf, d), lambda i, j: (j, 0, 0))]
    args = [xb, xf, w_u, cw, cb, w_d]
    if h0 is not None:
        in_specs.append(pl.BlockSpec((1, 1, hr, 2 * tf), lambda i, j: (i // tps, j, 0, 0)))
        args.append(h0)
    in_specs += [pl.BlockSpec((1, d), lambda i, j: (0, 0))] * 2
    args += [ln_g, ln_b]
    return pl.pallas_call(
        kern,
        grid=(ni, nj),
        in_specs=in_specs,
        out_specs=[pl.BlockSpec((tm, d), lambda i, j: (i, 0), **mode),
                   pl.BlockSpec((tm, d), lambda i, j: (i, 0), **mode),
                   pl.BlockSpec((1, 1, hr, 2 * tf),
                                lambda i, j: (i // tps, jnp.where(i % tps == tps - 1, j, 0), 0, 0))],
        out_shape=[jax.ShapeDtypeStruct((m, d), F32),
                   jax.ShapeDtypeStruct((m, d), BF16),
                   jax.ShapeDtypeStruct((nseq, nj, hr, 2 * tf), F32)],
        scratch_shapes=[pltpu.VMEM((tm, d), F32),
                        pltpu.VMEM((hr + tm, 2 * tf), F32),
                        pltpu.VMEM((tm, tf), BF16),
                        pltpu.VMEM((nj, hr, 2 * tf), F32)],
        compiler_params=_params(("arbitrary", "arbitrary"), vmem_limit_bytes=vmem_bytes),
        name="conv_ffn_ln",
    )(*args)


def _ffn_tiles(g, v, tf):
    nj = g.shape[-1] // tf
    lead = g.shape[:-1]
    t = jnp.stack([g.reshape(lead + (nj, tf)), v.reshape(lead + (nj, tf))], axis=-2)
    return jnp.moveaxis(t.reshape(lead + (nj, 2 * tf)), -2, 0)


def _ffn_untile(t, tf):
    nj = t.shape[-3]
    t = jnp.moveaxis(t, -3, -2)
    lead = t.shape[:-2]
    return (t[..., :tf].reshape(lead + (nj * tf,)), t[..., tf:].reshape(lead + (nj * tf,)))


def _tile_up_kernel(w_ref, o_ref, *, dff, tf):
    nj = o_ref.shape[0]
    for j in range(nj):
        lo = j * tf
        width = min(tf, dff - lo)
        for half, base in ((0, 0), (1, dff)):
            o_ref[j, :, half * tf:half * tf + width] = w_ref[0, :, base + lo:base + lo + width].astype(BF16)
            if width < tf:
                o_ref[j, :, half * tf + width:(half + 1) * tf] = jnp.zeros((o_ref.shape[1], tf - width), BF16)


def _tile_up(w_up, layer, tf):
    _, d, two_dff = w_up.shape
    dff = two_dff // 2
    assert dff % LANES == 0
    nj = -(-dff // tf)
    tr = _largest_divisor(d, 128, 16)
    return pl.pallas_call(
        functools.partial(_tile_up_kernel, dff=dff, tf=tf),
        grid=(d // tr,),
        in_specs=[pl.BlockSpec((1, tr, two_dff), lambda r: (layer, r, 0))],
        out_specs=pl.BlockSpec((nj, tr, 2 * tf), lambda r: (0, r, 0)),
        out_shape=jax.ShapeDtypeStruct((nj, d, 2 * tf), BF16),
        compiler_params=_params(("parallel",)),
        name="tile_w_up",
    )(w_up)


def _tile_down_kernel(w_ref, o_ref, *, dff):
    j = pl.program_id(0)
    tf = o_ref.shape[1]
    rows = lax.broadcasted_iota(jnp.int32, o_ref.shape[1:], 0) + j * tf
    o_ref[0] = jnp.where(rows < dff, w_ref[0], 0.0).astype(BF16)


def _tile_down(w_down, layer, tf):
    _, dff, d = w_down.shape
    nj = -(-dff // tf)
    return pl.pallas_call(
        functools.partial(_tile_down_kernel, dff=dff),
        grid=(nj,),
        in_specs=[pl.BlockSpec((1, tf, d), lambda j: (layer, j, 0))],
        out_specs=pl.BlockSpec((1, tf, d), lambda j: (j, 0, 0)),
        out_shape=jax.ShapeDtypeStruct((nj, tf, d), BF16),
        compiler_params=_params(("parallel",)),
        name="tile_w_down",
    )(w_down)


def _prep_layer(l, w_in, b_i, b_f, m_norm_w, s_conv_w, s_conv_b, dt_bias, A_log, D_skip, s_norm_w, w_out,
                ln1_g, ln1_b, w_up, f_conv_w, f_conv_b, w_down, ln2_g, ln2_b):
    mw = m_norm_w.shape[1]
    sw = s_norm_w.shape[1]
    cd = s_conv_w.shape[2]
    nh = dt_bias.shape[1]
    dff = w_down.shape[1]
    tf = FFN_TILE
    fp = -(-dff // tf) * tf
    o = 0
    cols = {}
    for name, size in (("qkv", 3 * mw), ("o", mw), ("i", M_HEADS), ("f", M_HEADS),
                       ("z", sw), ("xbc", cd), ("dt", nh)):
        cols[name] = (o, o + size)
        o += size
    wl = w_in[l]
    cut = lambda name: wl[:, cols[name][0]:cols[name][1]]
    p = {}
    wg = jnp.concatenate([cut("i"), cut("f"), cut("dt")], axis=1)
    p["w_in"] = [cut("qkv").astype(BF16), cut("o").astype(BF16), cut("z").astype(BF16), cut("xbc").astype(BF16),
                 jnp.pad(wg, ((0, 0), (0, LANES - wg.shape[1]))).astype(BF16)]
    gb = jnp.concatenate([b_i[l], b_f[l]])
    p["gate_bias"] = jnp.pad(gb, (0, LANES - gb.shape[0]))[None, :]
    p["m_norm_w"] = m_norm_w[l][None, :]
    p["s_conv_w"] = s_conv_w[l]
    p["s_conv_b"] = s_conv_b[l][None, :]
    hp = jnp.stack([dt_bias[l], A_log[l], D_skip[l]])
    p["head_params"] = jnp.pad(hp, ((0, SUBLANES - 3), (GATE_DT, LANES - GATE_DT - nh)))
    p["s_norm_w"] = s_norm_w[l][None, :]
    p["w_out"] = w_out[l].astype(BF16).reshape(2, mw, w_out.shape[2])
    p["ln1_g"] = ln1_g[l][None, :]
    p["ln1_b"] = ln1_b[l][None, :]
    padc = lambda a: jnp.pad(a, ((0, 0), (0, fp - dff)))
    p["w_up"] = _tile_up(w_up, l, tf)
    p["f_conv_w"] = _ffn_tiles(padc(f_conv_w[l][:, :dff]), padc(f_conv_w[l][:, dff:]), tf)
    p["f_conv_b"] = _ffn_tiles(padc(f_conv_b[l][None, :dff]), padc(f_conv_b[l][None, dff:]), tf)
    p["w_down"] = _tile_down(w_down, l, tf)
    p["ln2_g"] = ln2_g[l][None, :]
    p["ln2_b"] = ln2_b[l][None, :]
    p["dff"] = dff
    p["fp"] = fp
    return p


def _layer(xf, xb, p, init, prev, *, layer, depth, nseq, T, L, time_major_ffn, alpha):
    qkv, o_pre, z, xbc, gates = _in_proj(xb, p["w_in"], [BF16, BF16, BF16, BF16, F32], tm_target=2048)
    hm, C, n, m = _mlstm(qkv, o_pre, gates, p["gate_bias"], p["m_norm_w"],
                         None if init is None else init["mlstm"], None if prev is None else prev["mlstm"],
                         layer=layer, depth=depth, nseq=nseq, T=T, L=L)
    hs, S, sconv = _ssd(z, xbc, gates, p["s_conv_w"], p["s_conv_b"], p["head_params"], p["s_norm_w"],
                        None if init is None else init["ssd"], None if prev is None else prev["ssd"],
                        layer=layer, depth=depth, nseq=nseq, T=T, L=L)
    x1f, x1b = _outproj(hm, hs, p["w_out"], xf, p["ln1_g"], p["ln1_b"], alpha=alpha)
    d = xf.shape[1]
    dff = p["dff"]
    ffn_w = (p["w_up"], p["f_conv_w"], p["f_conv_b"], p["w_down"])
    h0 = None if init is None else init["ffn"]
    if time_major_ffn:
        tr = lambda a: a.reshape(nseq, T, d).transpose(1, 0, 2).reshape(nseq * T, d)
        x1f_t, x1b_t = tr(x1f), tr(x1b)
        tm = _largest_divisor(nseq * T, 512, (FFN_CONV - 1) * nseq)
        y_f, y_b, oh = _ffn(x1b_t, x1f_t, *ffn_w, h0, p["ln2_g"], p["ln2_b"], nseq=1, tm=tm, sh=nseq,
                            hr=(FFN_CONV - 1) * nseq, alpha=alpha, single_buffer=True)
        tb = lambda a: a.reshape(T, nseq, d).transpose(1, 0, 2).reshape(nseq * T, d)
        y_f, y_b = tb(y_f), tb(y_b)
        og, ov = _ffn_untile(oh, FFN_TILE)
        fconv = jnp.concatenate([og[0, :, :dff], ov[0, :, :dff]], axis=1)
        fconv = fconv.reshape(FFN_CONV - 1, nseq, 2 * dff).transpose(1, 0, 2)
    else:
        tm = _largest_divisor(T, 512, FFN_ROW_CHUNK)
        y_f, y_b, oh = _ffn(x1b, x1f, *ffn_w, h0, p["ln2_g"], p["ln2_b"],
                            nseq=nseq, tm=tm, sh=1, hr=SUBLANES, alpha=alpha, single_buffer=False)
        og, ov = _ffn_untile(oh, FFN_TILE)
        k = FFN_CONV - 1
        fconv = jnp.concatenate([og[:, SUBLANES - k:, :dff], ov[:, SUBLANES - k:, :dff]], axis=2)
    return y_f, y_b, dict(mlstm=(C, n, m), ssd=(S, sconv)), fconv


def kernel(x_prompt, x_sample, state_mlstm_C, state_mlstm_n, state_mlstm_m, state_ssm, state_ssm_conv,
           state_ffn_conv, w_in, mlstm_b_i, mlstm_b_f, mlstm_norm_w, ssm_conv_w, ssm_conv_b, ssm_dt_bias,
           ssm_A_log, ssm_D, ssm_norm_w, w_out, ln1_g, ln1_b, ffn_w_up, ffn_conv_w, ffn_conv_b,
           ffn_w_down, ln2_g, ln2_b):
    weights = (w_in, mlstm_b_i, mlstm_b_f, mlstm_norm_w, ssm_conv_w, ssm_conv_b, ssm_dt_bias, ssm_A_log,
               ssm_D, ssm_norm_w, w_out, ln1_g, ln1_b, ffn_w_up, ffn_conv_w, ffn_conv_b, ffn_w_down,
               ln2_g, ln2_b)
    depth = w_in.shape[0]
    alpha = float((2 * depth) ** 0.25)
    B, T, D = x_prompt.shape
    Bs, Ts, _ = x_sample.shape
    nh = ssm_dt_bias.shape[1]
    Lp = PROMPT_CHUNK if T % PROMPT_CHUNK == 0 else T
    Ls = PROMPT_CHUNK if Ts % PROMPT_CHUNK == 0 else Ts

    xpf = x_prompt.reshape(B * T, D)
    xpb = xpf.astype(BF16)
    xsf = x_sample.reshape(Bs * Ts, D)
    xsb = xsf.astype(BF16)
    m_in = jnp.pad(state_mlstm_m, ((0, 0), (0, 0), (GATE_F, LANES - GATE_F - M_HEADS)))[:, :, None, :]
    s_in = state_ssm.reshape(depth, Bs, nh * S_HEADDIM, S_STATE)
    p_prev, s_prev, p_fconv, s_fconv = None, None, [], []
    for l in range(depth):
        p = _prep_layer(l, *weights)
        fp, dff = p["fp"], p["dff"]
        xpf, xpb, p_prev, fc = _layer(xpf, xpb, p, None, p_prev, layer=l, depth=depth, nseq=B, T=T, L=Lp,
                                      time_major_ffn=False, alpha=alpha)
        p_fconv.append(fc)
        fc0 = state_ffn_conv[l].transpose(1, 0, 2).reshape((FFN_CONV - 1) * Bs, 2 * dff)
        padf = lambda a: jnp.pad(a, ((0, 0), (0, fp - dff)))
        init = dict(mlstm=(state_mlstm_C, state_mlstm_n, m_in), ssd=(s_in, state_ssm_conv),
                    ffn=_ffn_tiles(padf(fc0[:, :dff]), padf(fc0[:, dff:]), FFN_TILE)[None])
        xsf, xsb, s_prev, fc = _layer(xsf, xsb, p, init, s_prev, layer=l, depth=depth, nseq=Bs, T=Ts, L=Ls,
                                      time_major_ffn=True, alpha=alpha)
        s_fconv.append(fc)

    def unpack(st, nseq):
        C, n, m = st["mlstm"]
        S, sconv = st["ssd"]
        return (C, n, m[:, :, 0, GATE_F:GATE_F + M_HEADS], S.reshape(depth, nseq, nh, S_HEADDIM, S_STATE),
                sconv[:, :, SUBLANES - (S_CONV - 1):, :])

    pC, pn, pm, pS, psc = unpack(p_prev, B)
    sC, sn, sm, sS, ssc = unpack(s_prev, Bs)
    return (xpf.reshape(B, T, D), xsf.reshape(Bs, Ts, D), pC, pn, pm, pS, psc, jnp.stack(p_fconv),
            sC, sn, sm, sS, ssc, jnp.stack(s_fconv))
```
